```python
import jax
import jax.numpy as jnp
from jax import lax
import numpy as np

D_MODEL = 1024
BATCH = 8
SEQ = 4096
DEPTH = 2

GRID_W = 64
CTX_LEN = 256
N_EVEN = (DEPTH + 1) // 2
N_ODD = DEPTH // 2
EPS = 1e-6

MLSTM_HEADS = 4
MLSTM_DH = 256
MLSTM_W = MLSTM_HEADS * MLSTM_DH
MLSTM_CHUNK = 64
M_INIT = -1e30
LRU_W = 1024
LRU_BLOCKS = 16
LRU_BW = LRU_W // LRU_BLOCKS
LRU_C = 8.0
CONV_W = 4
CONV_LEFT = 2
EVEN_SPLITS = (MLSTM_W, 2 * MLSTM_W, 3 * MLSTM_W, 4 * MLSTM_W, 4 * MLSTM_W + 4 * MLSTM_HEADS, 4 * MLSTM_W + 4 * MLSTM_HEADS + LRU_W)
EVEN_IN = 4 * MLSTM_W + 4 * MLSTM_HEADS + 2 * LRU_W
EVEN_MIX = MLSTM_W + LRU_W
ATT_HEADS = 8
ATT_KV_HEADS = 2
ATT_GROUP = ATT_HEADS // ATT_KV_HEADS
ATT_DH = 128
ATT_Q_W = ATT_HEADS * ATT_DH
ATT_KV_W = ATT_KV_HEADS * ATT_DH
ATT_IN = ATT_Q_W + 2 * ATT_KV_W
ATT_BLOCK = 128
ROPE_AXIS_DIM = ATT_DH // 2
ROPE_THETA = 10000.0
N_EXPERTS = 32
TOP_K = 4
D_FF = 1024
SWIGLU_ALPHA = 1.702
SWIGLU_LIMIT = 7.0
MOE_BLOCK = 128

kernel_name = 'hybrid_mlstm_rglru_gqa_moe_diffusion'


def rmsnorm(x, g):
    xf = x.astype(jnp.float32)
    y = xf * lax.rsqrt(jnp.mean(xf * xf, axis=-1, keepdims=True) + EPS)
    return (y * g.astype(jnp.float32)).astype(x.dtype)


def dwconv_centred(x, w, b):
    y = lax.conv_general_dilated(
        x, w[:, None, :].astype(x.dtype), window_strides=(1,),
        padding=[(CONV_LEFT, CONV_W - 1 - CONV_LEFT)],
        dimension_numbers=('NWC', 'WIO', 'NWC'), feature_group_count=x.shape[-1])
    return y + b.astype(x.dtype)


def _maybe_flip(t, axis, rev):
    return jnp.flip(t, axis) if rev else t


def mlstm_chunk_scan(q, k, v, ig, lf, state):
    bsz, nh, seqlen, dh = q.shape
    n_chunks = seqlen // MLSTM_CHUNK

    def to_chunks(a):
        a = a.reshape(bsz, nh, n_chunks, MLSTM_CHUNK, *a.shape[3:])
        return jnp.moveaxis(a, 2, 0)

    lower = jnp.tril(jnp.ones((MLSTM_CHUNK, MLSTM_CHUNK), dtype=bool))

    def step(carry, inp):
        c_mat, n_vec, m_prev = carry
        qc, kc, vc, ic, fc = inp
        cum_f = jnp.cumsum(fc, axis=-1)
        log_intra = cum_f[..., :, None] - cum_f[..., None, :] + ic[..., None, :]
        log_intra = jnp.where(lower, log_intra, -jnp.inf)
        log_inter = cum_f + m_prev[..., None]
        m_t = jnp.maximum(log_inter, jnp.max(log_intra, axis=-1))
        w_inter = jnp.exp(log_inter - m_t)
        scores = jnp.einsum('bhtd,bhsd->bhts', qc, kc) * jnp.exp(log_intra - m_t[..., None])
        num = (w_inter[..., None] * jnp.einsum('bhed,bhtd->bhte', c_mat, qc)
               + jnp.einsum('bhts,bhse->bhte', scores, vc))
        den = w_inter * jnp.einsum('bhd,bhtd->bht', n_vec, qc) + jnp.sum(scores, axis=-1)
        h = num / jnp.maximum(jnp.abs(den), jnp.exp(-m_t))[..., None]
        total_f = cum_f[..., -1]
        log_w = total_f[..., None] - cum_f + ic
        m_new = jnp.maximum(total_f + m_prev, jnp.max(log_w, axis=-1))
        decay = jnp.exp(total_f + m_prev - m_new)
        w_state = jnp.exp(log_w - m_new[..., None])
        c_new = decay[..., None, None] * c_mat + jnp.einsum('bhse,bhsd->bhed', vc * w_state[..., None], kc)
        n_new = decay[..., None] * n_vec + jnp.einsum('bhs,bhsd->bhd', w_state, kc)
        return (c_new, n_new, m_new), h

    state, h = lax.scan(step, state, (to_chunks(q), to_chunks(k), to_chunks(v), to_chunks(ig), to_chunks(lf)))
    h = jnp.moveaxis(h, 0, 2).reshape(bsz, nh, seqlen, dh)
    return h, state


def linear_scan(a, b, h0):
    def combine(left, right):
        return left[0] * right[0], right[0] * left[1] + right[1]
    a_cum, b_cum = lax.associative_scan(combine, (a, b), axis=1)
    h = a_cum * h0[:, None, :] + b_cum
    return h, h[:, -1]


def even_prepare(u, w_in, qk_conv_w, qk_conv_b, gate_b, lru_conv_w, lru_conv_b, lru_wa, lru_ba, lru_wx, lru_bx, lru_lam):
    bsz, n, _ = u.shape
    z = u @ w_in
    q_pre, k_pre, v, o_pre, g, xr, y_gate = jnp.split(z, EVEN_SPLITS, axis=-1)
    qk = jax.nn.silu(dwconv_centred(jnp.concatenate([q_pre, k_pre], axis=-1), qk_conv_w, qk_conv_b))
    q, k = jnp.split(qk, 2, axis=-1)

    def heads(t):
        return t.reshape(bsz, n, MLSTM_HEADS, MLSTM_DH).transpose(0, 2, 1, 3).astype(jnp.float32)

    q, k, v = heads(q), heads(k) * (MLSTM_DH ** -0.5), heads(v)
    g = (g + gate_b).astype(jnp.float32).reshape(bsz, n, 2, 2, MLSTM_HEADS).transpose(2, 3, 0, 4, 1)
    ig, lf = g[:, 0], jax.nn.log_sigmoid(g[:, 1])
    xr = dwconv_centred(xr, lru_conv_w, lru_conv_b).astype(jnp.float32)
    xb = xr.reshape(bsz, n, LRU_BLOCKS, LRU_BW)
    r = jax.nn.sigmoid(jnp.einsum('blnc,zncd->zblnd', xb, lru_wa.astype(jnp.float32)) + lru_ba.astype(jnp.float32)[:, None, None])
    i = jax.nn.sigmoid(jnp.einsum('blnc,zncd->zblnd', xb, lru_wx.astype(jnp.float32)) + lru_bx.astype(jnp.float32)[:, None, None])
    log_a = -LRU_C * r.reshape(2, bsz, n, LRU_W) * jax.nn.softplus(-lru_lam.astype(jnp.float32))[:, None, None, :]
    a = jnp.exp(log_a)
    bx = jnp.sqrt(-jnp.expm1(2.0 * log_a)) * i.reshape(2, bsz, n, LRU_W) * xr
    return q, k, v, ig, lf, a, bx, o_pre, y_gate


def even_recur(q, k, v, ig, lf, a, bx, states):
    h_m, h_l, finals = 0.0, 0.0, []
    for d in range(2):
        rev = d == 1
        m_state, l_state = states[d]
        hm, m_state = mlstm_chunk_scan(_maybe_flip(q, 2, rev), _maybe_flip(k, 2, rev), _maybe_flip(v, 2, rev),
                                       _maybe_flip(ig[d], 2, rev), _maybe_flip(lf[d], 2, rev), m_state)
        hl, l_state = linear_scan(_maybe_flip(a[d], 1, rev), _maybe_flip(bx[d], 1, rev), l_state)
        h_m = h_m + _maybe_flip(hm, 2, rev)
        h_l = h_l + _maybe_flip(hl, 1, rev)
        finals.append((m_state, l_state))
    return h_m, h_l, finals


def even_output(h_m, h_l, o_pre, y_gate, mnorm_g, w_out, dtype):
    bsz, nh, n, dh = h_m.shape
    hm = h_m.transpose(0, 2, 1, 3)
    hm = hm * lax.rsqrt(jnp.mean(hm * hm, axis=-1, keepdims=True) + EPS) * mnorm_g.astype(jnp.float32).reshape(nh, dh)
    hm = hm.reshape(bsz, n, MLSTM_W) * jax.nn.sigmoid(o_pre.astype(jnp.float32))
    hl = h_l * jax.nn.gelu(y_gate.astype(jnp.float32))
    return jnp.concatenate([hm, hl], axis=-1).astype(dtype) @ w_out


def even_mixer(u_lat, u_ctx, w_in, qk_conv_w, qk_conv_b, gate_b, mnorm_g, lru_conv_w, lru_conv_b,
               lru_wa, lru_ba, lru_wx, lru_bx, lru_lam, w_out, ctx_out):
    pc = even_prepare(u_ctx, w_in, qk_conv_w, qk_conv_b, gate_b, lru_conv_w, lru_conv_b, lru_wa, lru_ba, lru_wx, lru_bx, lru_lam)
    pl = even_prepare(u_lat, w_in, qk_conv_w, qk_conv_b, gate_b, lru_conv_w, lru_conv_b, lru_wa, lru_ba, lru_wx, lru_bx, lru_lam)
    bsz = u_lat.shape[0]
    zero_m = (jnp.zeros((bsz, MLSTM_HEADS, MLSTM_DH, MLSTM_DH), jnp.float32),
              jnp.zeros((bsz, MLSTM_HEADS, MLSTM_DH), jnp.float32),
              jnp.full((bsz, MLSTM_HEADS), M_INIT, jnp.float32))
    zero_l = jnp.zeros((bsz, LRU_W), jnp.float32)
    hm_c, hl_c, ctx_states = even_recur(*pc[:7], [(zero_m, zero_l), (zero_m, zero_l)])
    hm_l, hl_l, _ = even_recur(*pl[:7], ctx_states)
    y_lat = even_output(hm_l, hl_l, pl[7], pl[8], mnorm_g, w_out, u_lat.dtype)
    y_ctx = even_output(hm_c, hl_c, pc[7], pc[8], mnorm_g, w_out, u_ctx.dtype) if ctx_out else None
    return y_lat, y_ctx


def odd_project(u, w_in, q_norm_g, k_norm_g, with_q):
    bsz, n, _ = u.shape
    if with_q:
        q, k, v = jnp.split(u @ w_in, [ATT_Q_W, ATT_Q_W + ATT_KV_W], axis=-1)
        q = rmsnorm(q.reshape(bsz, n, ATT_HEADS, ATT_DH), q_norm_g)
    else:
        k, v = jnp.split(u @ w_in[:, ATT_Q_W:], [ATT_KV_W], axis=-1)
        q = None
    k = rmsnorm(k.reshape(bsz, n, ATT_KV_HEADS, ATT_DH), k_norm_g)
    v = v.reshape(bsz, n, ATT_KV_HEADS, ATT_DH)
    return q, k, v


def axial_rope(t):
    n = t.shape[1]
    rows = n // GRID_W
    row = jnp.repeat(jnp.arange(rows), GRID_W)
    col = jnp.tile(jnp.arange(GRID_W), rows)
    inv_freq = ROPE_THETA ** (-jnp.arange(0, ROPE_AXIS_DIM, 2, dtype=jnp.float32) / ROPE_AXIS_DIM)
    tf = t.astype(jnp.float32)

    def rotate(seg, pos):
        ang = pos.astype(jnp.float32)[:, None] * inv_freq
        cos = jnp.cos(ang)[None, :, None, :]
        sin = jnp.sin(ang)[None, :, None, :]
        s1, s2 = jnp.split(seg, 2, axis=-1)
        return jnp.concatenate([s1 * cos - s2 * sin, s1 * sin + s2 * cos], axis=-1)

    out = jnp.concatenate([rotate(tf[..., :ROPE_AXIS_DIM], row), rotate(tf[..., ROPE_AXIS_DIM:], col)], axis=-1)
    return out.astype(t.dtype)


def attend_blocks(q, k, v):
    bsz, lq = q.shape[:2]
    nb = lq // ATT_BLOCK
    qb = q.reshape(bsz, nb, ATT_BLOCK, ATT_KV_HEADS, ATT_GROUP, ATT_DH).transpose(1, 0, 2, 3, 4, 5)
    scale = ATT_DH ** -0.5

    def block(qi):
        s = jnp.einsum('bqhgd,bkhd->bhgqk', qi, k).astype(jnp.float32) * scale
        p = jax.nn.softmax(s, axis=-1).astype(v.dtype)
        return jnp.einsum('bhgqk,bkhd->bqhgd', p, v)

    o = lax.map(block, qb)
    return o.transpose(1, 0, 2, 3, 4, 5).reshape(bsz, lq, ATT_Q_W)


def odd_mixer(u_lat, u_ctx, w_in, q_norm_g, k_norm_g, w_out, ctx_out):
    q, k, v = odd_project(u_lat, w_in, q_norm_g, k_norm_g, True)
    qc, kc, vc = odd_project(u_ctx, w_in, q_norm_g, k_norm_g, ctx_out)
    q, k = axial_rope(q), axial_rope(k)
    y_lat = attend_blocks(q, jnp.concatenate([kc, k], axis=1), jnp.concatenate([vc, v], axis=1)) @ w_out
    y_ctx = attend_blocks(qc, kc, vc) @ w_out if ctx_out else None
    return y_lat, y_ctx


def moe_ffn(u, w_r, b_r, w1, b1, w2, b2):
    d = u.shape[-1]

    def per_sample(xt):
        n_tok = xt.shape[0]
        n_rows = n_tok * TOP_K
        n_blocks = -(-n_rows // MOE_BLOCK) + N_EXPERTS
        logits = (xt @ w_r + b_r).astype(jnp.float32)
        top_logit, top_idx = lax.top_k(logits, TOP_K)
        weight = jax.nn.softmax(top_logit, axis=-1)
        expert = top_idx.reshape(-1)
        order = jnp.argsort(expert)
        expert_sorted = expert[order]
        token = order // TOP_K
        sizes = jnp.bincount(expert, length=N_EXPERTS)
        start = jnp.cumsum(sizes) - sizes
        padded = (sizes + MOE_BLOCK - 1) // MOE_BLOCK * MOE_BLOCK
        pad_end = jnp.cumsum(padded)
        pos = pad_end[expert_sorted] - padded[expert_sorted] + jnp.arange(n_rows) - start[expert_sorted]
        buf = jnp.zeros((n_blocks * MOE_BLOCK, d), xt.dtype).at[pos].set(xt[token])
        block_expert = jnp.minimum(jnp.searchsorted(pad_end, jnp.arange(n_blocks) * MOE_BLOCK, side='right'), N_EXPERTS - 1)

        def expert_block(args):
            xb, e = args
            hid = xb @ w1[e] + b1[e]
            gate = jnp.minimum(hid[:, :D_FF], SWIGLU_LIMIT)
            up = jnp.clip(hid[:, D_FF:], -SWIGLU_LIMIT, SWIGLU_LIMIT)
            act = (up + 1) * gate * jax.nn.sigmoid(SWIGLU_ALPHA * gate)
            return act @ w2[e] + b2[e]

        out_buf = lax.map(expert_block, (buf.reshape(n_blocks, MOE_BLOCK, d), block_expert))
        out = out_buf.reshape(-1, d)[pos] * weight.reshape(-1)[order][:, None].astype(xt.dtype)
        return jax.ops.segment_sum(out, token, num_segments=n_tok)

    return lax.map(per_sample, u)


def setup_inputs(seed: int = 0) -> dict:
    key = jax.random.key(seed)
    ks = jax.random.split(key, 32)
    D = D_MODEL

    def nrm(k, shape, scale):
        return jax.random.normal(k, shape, jnp.float32) * scale

    f_bias = jnp.stack([jnp.zeros((MLSTM_HEADS,), jnp.float32), jnp.linspace(3.0, 6.0, MLSTM_HEADS, dtype=jnp.float32)])
    a0 = jax.random.uniform(ks[20], (N_EVEN, 2, LRU_W), jnp.float32, minval=0.9, maxval=0.999)
    s0 = a0 ** (1.0 / LRU_C)
    return {
        'x': nrm(ks[0], (BATCH, SEQ, D), 1.0),
        'c': nrm(ks[1], (BATCH, D), 1.0),
        'ctx': nrm(ks[2], (BATCH, CTX_LEN, D), 1.0),
        'c_ctx': nrm(ks[3], (D,), 1.0),
        'mod_w': nrm(ks[4], (DEPTH, D, 6 * D), 0.5 * D ** -0.5),
        'mod_b': nrm(ks[5], (DEPTH, 6 * D), 0.02),
        'norm1_g': 1.0 + nrm(ks[6], (DEPTH, D), 0.05),
        'norm2_g': 1.0 + nrm(ks[7], (DEPTH, D), 0.05),
        'final_g': 1.0 + nrm(ks[8], (D,), 0.05),
        'ev_w_in': nrm(ks[9], (N_EVEN, D, EVEN_IN), D ** -0.5),
        'ev_qk_conv_w': nrm(ks[10], (N_EVEN, CONV_W, 2 * MLSTM_W), CONV_W ** -0.5),
        'ev_qk_conv_b': nrm(ks[11], (N_EVEN, 2 * MLSTM_W), 0.02),
        'ev_gate_b': (nrm(ks[12], (N_EVEN, 2, 2, MLSTM_HEADS), 0.1) + f_bias).reshape(N_EVEN, 4 * MLSTM_HEADS),
        'ev_mnorm_g': 1.0 + nrm(ks[13], (N_EVEN, MLSTM_W), 0.05),
        'ev_lru_conv_w': nrm(ks[14], (N_EVEN, CONV_W, LRU_W), CONV_W ** -0.5),
        'ev_lru_conv_b': nrm(ks[15], (N_EVEN, LRU_W), 0.02),
        'ev_lru_wa': nrm(ks[16], (N_EVEN, 2, LRU_BLOCKS, LRU_BW, LRU_BW), LRU_BW ** -0.5),
        'ev_lru_ba': nrm(ks[17], (N_EVEN, 2, LRU_BLOCKS, LRU_BW), 0.02),
        'ev_lru_wx': nrm(ks[18], (N_EVEN, 2, LRU_BLOCKS, LRU_BW, LRU_BW), LRU_BW ** -0.5),
        'ev_lru_bx': nrm(ks[19], (N_EVEN, 2, LRU_BLOCKS, LRU_BW), 0.02),
        'ev_lru_lam': jnp.log(s0) - jnp.log1p(-s0),
        'ev_w_out': nrm(ks[21], (N_EVEN, EVEN_MIX, D), EVEN_MIX ** -0.5),
        'od_w_in': nrm(ks[22], (N_ODD, D, ATT_IN), D ** -0.5),
        'od_q_norm_g': 1.0 + nrm(ks[23], (N_ODD, ATT_DH), 0.05),
        'od_k_norm_g': 1.0 + nrm(ks[24], (N_ODD, ATT_DH), 0.05),
        'od_w_out': nrm(ks[25], (N_ODD, ATT_Q_W, D), ATT_Q_W ** -0.5),
        'moe_w_r': nrm(ks[26], (DEPTH, D, N_EXPERTS), D ** -0.5),
        'moe_b_r': nrm(ks[27], (DEPTH, N_EXPERTS), 0.01),
        'moe_w1': nrm(ks[28], (DEPTH, N_EXPERTS, D, 2 * D_FF), D ** -0.5),
        'moe_b1': nrm(ks[29], (DEPTH, N_EXPERTS, 2 * D_FF), 0.02),
        'moe_w2': nrm(ks[30], (DEPTH, N_EXPERTS, D_FF, D), D_FF ** -0.5),
        'moe_b2': nrm(ks[31], (DEPTH, N_EXPERTS, D), 0.02),
    }


def reference(x, c, ctx, c_ctx, mod_w, mod_b, norm1_g, norm2_g, final_g,
              ev_w_in, ev_qk_conv_w, ev_qk_conv_b, ev_gate_b, ev_mnorm_g, ev_lru_conv_w, ev_lru_conv_b,
              ev_lru_wa, ev_lru_ba, ev_lru_wx, ev_lru_bx, ev_lru_lam, ev_w_out,
              od_w_in, od_q_norm_g, od_k_norm_g, od_w_out,
              moe_w_r, moe_b_r, moe_w1, moe_b1, moe_w2, moe_b2):
    h_ctx = ctx
    for layer in range(DEPTH):
        last = layer == DEPTH - 1
        j = layer // 2
        mod_lat = (jax.nn.silu(c) @ mod_w[layer] + mod_b[layer])[:, None, :]
        mod_ctx = (jax.nn.silu(c_ctx) @ mod_w[layer] + mod_b[layer])[None, None, :]
        sh1, sc1, g1, sh2, sc2, g2 = jnp.split(mod_lat, 6, axis=-1)
        csh1, csc1, cg1, csh2, csc2, cg2 = jnp.split(mod_ctx, 6, axis=-1)
        u_lat = rmsnorm(x, norm1_g[layer]) * (1 + sc1) + sh1
        u_ctx = rmsnorm(h_ctx, norm1_g[layer]) * (1 + csc1) + csh1
        if layer % 2 == 0:
            y_lat, y_ctx = even_mixer(u_lat, u_ctx, ev_w_in[j], ev_qk_conv_w[j], ev_qk_conv_b[j], ev_gate_b[j],
                                      ev_mnorm_g[j], ev_lru_conv_w[j], ev_lru_conv_b[j], ev_lru_wa[j], ev_lru_ba[j],
                                      ev_lru_wx[j], ev_lru_bx[j], ev_lru_lam[j], ev_w_out[j], not last)
        else:
            y_lat, y_ctx = odd_mixer(u_lat, u_ctx, od_w_in[j], od_q_norm_g[j], od_k_norm_g[j], od_w_out[j], not last)
        x = x + g1 * y_lat
        v_lat = rmsnorm(x, norm2_g[layer]) * (1 + sc2) + sh2
        if last:
            x = x + g2 * moe_ffn(v_lat, moe_w_r[layer], moe_b_r[layer], moe_w1[layer], moe_b1[layer], moe_w2[layer], moe_b2[layer])
        else:
            h_ctx = h_ctx + cg1 * y_ctx
            v_ctx = rmsnorm(h_ctx, norm2_g[layer]) * (1 + csc2) + csh2
            n_ctx = h_ctx.shape[1]
            f = moe_ffn(jnp.concatenate([v_ctx, v_lat], axis=1), moe_w_r[layer], moe_b_r[layer],
                        moe_w1[layer], moe_b1[layer], moe_w2[layer], moe_b2[layer])
            h_ctx = h_ctx + cg2 * f[:, :n_ctx]
            x = x + g2 * f[:, n_ctx:]
    return rmsnorm(x, final_g)
```

```python
import functools

import jax
import jax.numpy as jnp
from jax import lax
from jax.experimental import pallas as pl
from jax.experimental.pallas import tpu as pltpu

F32 = jnp.float32
BF16 = jnp.bfloat16
HIGHEST = lax.Precision.HIGHEST

EPS = 1e-6
M_INIT = -1e30
NEG_BIG = -1e30

MLSTM_HEADS = 4
MLSTM_DH = 256
MLSTM_W = MLSTM_HEADS * MLSTM_DH
LRU_W = 1024
LRU_BLOCKS = 16
LRU_BW = LRU_W // LRU_BLOCKS
LRU_C = 8.0
CONV_W = 4
CONV_LEFT = 2
ATT_HEADS = 8
ATT_KV_HEADS = 2
ATT_GROUP = ATT_HEADS // ATT_KV_HEADS
ATT_DH = 128
GRID_W = 64
ROPE_AXIS_DIM = ATT_DH // 2
ROPE_THETA = 10000.0
N_EXPERTS = 32
TOP_K = 4
D_FF = 1024
SWIGLU_ALPHA = 1.702
SWIGLU_LIMIT = 7.0

V7X_LANES = 128
V7X_MXU_DIM = 256
V7X_VMEM_BYTES = 64 * 1024 * 1024
MIB = 1024 * 1024

ROW_TILE = 256
HALO = 16
MOE_TILE = 256
ROUTER_PAD = V7X_LANES
GATE_PAD = V7X_LANES


def _cparams(semantics, vmem_mib):
    assert vmem_mib * MIB < V7X_VMEM_BYTES
    return pltpu.CompilerParams(dimension_semantics=semantics, vmem_limit_bytes=vmem_mib * MIB)


def _dot(a, b):
    return jnp.dot(a, b, preferred_element_type=F32)


def _dot_nt(a, b, precision=None):
    return lax.dot_general(a, b, (((1,), (1,)), ((), ())), precision=precision,
                           preferred_element_type=F32)


def _dot_tn(a, b):
    return lax.dot_general(a, b, (((0,), (0,)), ((), ())), preferred_element_type=F32)


def _sigmoid(x):
    return jax.nn.sigmoid(x)


def _log_sigmoid(x):
    return jnp.minimum(x, 0.0) - jnp.log1p(jnp.exp(-jnp.abs(x)))


def _softplus(x):
    return jnp.maximum(x, 0.0) + jnp.log1p(jnp.exp(-jnp.abs(x)))


def _gelu_tanh(x):
    return 0.5 * x * (1.0 + jnp.tanh(0.7978845608028654 * (x + 0.044715 * x * x * x)))


def _rms(x, g):
    return x * lax.rsqrt(jnp.mean(x * x, axis=-1, keepdims=True) + EPS) * g


def _modnorm(x, g, scale, shift):
    return _rms(x, g) * (1.0 + scale) + shift


def _modvec_kernel(c_ref, w_ref, b_ref, o_ref):
    c = c_ref[...]
    s = c * _sigmoid(c)
    o_ref[...] = jnp.dot(s, w_ref[...], precision=HIGHEST, preferred_element_type=F32) + b_ref[...]


def _modvec(cc, w, b):
    rows, d = cc.shape
    n = w.shape[1]
    tn = 1536
    return pl.pallas_call(
        _modvec_kernel,
        grid=(n // tn,),
        in_specs=[pl.BlockSpec((rows, d), lambda j: (0, 0)),
                  pl.BlockSpec((d, tn), lambda j: (0, j)),
                  pl.BlockSpec((1, tn), lambda j: (0, j))],
        out_specs=pl.BlockSpec((rows, tn), lambda j: (0, j)),
        out_shape=jax.ShapeDtypeStruct((rows, n), F32),
        compiler_params=_cparams(("arbitrary",), 32),
        name="modvec",
    )(cc, w, b.reshape(1, n))


def _mod_table(c, c_ctx, mod_w, mod_b):
    bsz, d = c.shape
    rows = ((bsz + 1 + 7) // 8) * 8
    cc = jnp.zeros((rows, d), F32).at[:bsz].set(c).at[bsz].set(c_ctx)
    mod = _modvec(cc, mod_w, mod_b)
    lat = mod[:bsz].reshape(bsz, 6, d)
    ctx = jnp.broadcast_to(mod[bsz].reshape(1, 6, d), (bsz, 6, d))
    tbl = jnp.stack([ctx, lat], axis=1)
    return jnp.pad(tbl, ((0, 0), (0, 0), (0, 2), (0, 0)))


def _proj_even_kernel(x_ref, mod_ref, g_ref, w_ref, wgt_ref,
                      qk_ref, v_ref, o_ref, xr_ref, yg_ref, gc_ref, gr_ref):
    mod = mod_ref[0, 0]
    u = _modnorm(x_ref[0], g_ref[...], mod[1:2], mod[0:1]).astype(BF16)
    w = MLSTM_W
    qk_ref[0] = _dot(u, w_ref[:, 0:2 * w]).astype(qk_ref.dtype)
    v_ref[0] = _dot(u, w_ref[:, 2 * w:3 * w]).astype(v_ref.dtype)
    o_ref[0] = _dot(u, w_ref[:, 3 * w:4 * w]).astype(o_ref.dtype)
    xr_ref[0] = _dot(u, w_ref[:, 4 * w:4 * w + LRU_W]).astype(xr_ref.dtype)
    yg_ref[0] = _dot(u, w_ref[:, 4 * w + LRU_W:4 * w + 2 * LRU_W]).astype(yg_ref.dtype)
    gc_ref[0] = _dot(u, w_ref[:, 4 * w + 2 * LRU_W:4 * w + 2 * LRU_W + 2 * GATE_PAD])
    gr_ref[0] = _dot_nt(wgt_ref[...], u)


def _proj_even(h, mod, g, w_packed, wg_t, n_ctx):
    bsz, s, d = h.shape
    tm = ROW_TILE
    nt = s // tm
    nct = n_ctx // tm
    ntot = w_packed.shape[1]
    ng = wg_t.shape[0]
    row = lambda b, i: (b, i, 0)
    return pl.pallas_call(
        _proj_even_kernel,
        grid=(bsz, nt),
        in_specs=[pl.BlockSpec((1, tm, d), row),
                  pl.BlockSpec((1, 1, 8, d), lambda b, i: (b, (i >= nct).astype(jnp.int32), 0, 0)),
                  pl.BlockSpec((1, d), lambda b, i: (0, 0)),
                  pl.BlockSpec((d, ntot), lambda b, i: (0, 0)),
                  pl.BlockSpec((ng, d), lambda b, i: (0, 0))],
        out_specs=[pl.BlockSpec((1, tm, 2 * MLSTM_W), row),
                   pl.BlockSpec((1, tm, MLSTM_W), row),
                   pl.BlockSpec((1, tm, MLSTM_W), row),
                   pl.BlockSpec((1, tm, LRU_W), row),
                   pl.BlockSpec((1, tm, LRU_W), row),
                   pl.BlockSpec((1, tm, 2 * GATE_PAD), row),
                   pl.BlockSpec((1, ng, tm), lambda b, i: (b, 0, i))],
        out_shape=[jax.ShapeDtypeStruct((bsz, s, 2 * MLSTM_W), BF16),
                   jax.ShapeDtypeStruct((bsz, s, MLSTM_W), BF16),
                   jax.ShapeDtypeStruct((bsz, s, MLSTM_W), BF16),
                   jax.ShapeDtypeStruct((bsz, s, LRU_W), F32),
                   jax.ShapeDtypeStruct((bsz, s, LRU_W), BF16),
                   jax.ShapeDtypeStruct((bsz, s, 2 * GATE_PAD), F32),
                   jax.ShapeDtypeStruct((bsz, ng, s), F32)],
        compiler_params=_cparams(("arbitrary", "arbitrary"), 48),
        name="proj_even",
    )(h, mod, g.reshape(1, d), w_packed, wg_t)


def _conv_kernel(nct, qk_m, qk_p, qk_n, xr_m, xr_p, xr_n, wqk_ref, bqk_ref, wxr_ref, bxr_ref,
                 q_ref, k_ref, xc_ref, ext_qk, ext_xr):
    i = pl.program_id(1)
    nt = pl.num_programs(1)
    tm = qk_m.shape[1]
    first = jnp.logical_or(i == 0, i == nct)
    last = jnp.logical_or(i == nct - 1, i == nt - 1)
    pm = jnp.where(first, 0.0, 1.0)
    nm = jnp.where(last, 0.0, 1.0)

    def conv(main, prev, nxt, ext, w_ref, b_ref):
        ext[0:HALO] = prev[0].astype(F32) * pm
        ext[HALO:HALO + tm] = main[0].astype(F32)
        ext[HALO + tm:2 * HALO + tm] = nxt[0].astype(F32) * nm
        acc = b_ref[...] + w_ref[0:1, :] * ext[pl.ds(HALO - CONV_LEFT, tm), :]
        for j in range(1, CONV_W):
            acc = acc + w_ref[j:j + 1, :] * ext[pl.ds(HALO - CONV_LEFT + j, tm), :]
        return acc

    y = conv(qk_m, qk_p, qk_n, ext_qk, wqk_ref, bqk_ref)
    y = y * _sigmoid(y)
    q_ref[0] = y[:, :MLSTM_W].astype(q_ref.dtype)
    k_ref[0] = (y[:, MLSTM_W:] * (MLSTM_DH ** -0.5)).astype(k_ref.dtype)
    xc_ref[0] = conv(xr_m, xr_p, xr_n, ext_xr, wxr_ref, bxr_ref)


def _conv(qk_pre, xr, wqk, bqk, wxr, bxr, n_ctx):
    bsz, s, _ = qk_pre.shape
    tm = ROW_TILE
    nt = s // tm
    nct = n_ctx // tm
    hb = tm // HALO
    nhb = s // HALO
    row = lambda b, i: (b, i, 0)
    prev = lambda b, i: (b, jnp.maximum(i * hb - 1, 0), 0)
    nxt = lambda b, i: (b, jnp.minimum((i + 1) * hb, nhb - 1), 0)
    cq = 2 * MLSTM_W
    full = lambda shape: pl.BlockSpec(shape, lambda b, i: (0, 0))
    return pl.pallas_call(
        functools.partial(_conv_kernel, nct),
        grid=(bsz, nt),
        in_specs=[pl.BlockSpec((1, tm, cq), row), pl.BlockSpec((1, HALO, cq), prev),
                  pl.BlockSpec((1, HALO, cq), nxt),
                  pl.BlockSpec((1, tm, LRU_W), row), pl.BlockSpec((1, HALO, LRU_W), prev),
                  pl.BlockSpec((1, HALO, LRU_W), nxt),
                  full((CONV_W, cq)), full((1, cq)), full((CONV_W, LRU_W)), full((1, LRU_W))],
        out_specs=[pl.BlockSpec((1, tm, MLSTM_W), row), pl.BlockSpec((1, tm, MLSTM_W), row),
                   pl.BlockSpec((1, tm, LRU_W), row)],
        out_shape=[jax.ShapeDtypeStruct((bsz, s, MLSTM_W), BF16),
                   jax.ShapeDtypeStruct((bsz, s, MLSTM_W), BF16),
                   jax.ShapeDtypeStruct((bsz, s, LRU_W), F32)],
        scratch_shapes=[pltpu.VMEM((tm + 2 * HALO, cq), F32),
                        pltpu.VMEM((tm + 2 * HALO, LRU_W), F32)],
        compiler_params=_cparams(("arbitrary", "arbitrary"), 48),
        name="dwconv",
    )(qk_pre, qk_pre, qk_pre, xr, xr, xr, wqk, bqk.reshape(1, cq), wxr, bxr.reshape(1, LRU_W))


def _mlstm_kernel(q_ref, k_ref, v_ref, gc_ref, gr_ref, bc_ref, br_ref, h_ref, c_scr, n_scr, m_scr):
    d = pl.program_id(1)
    j = pl.program_id(2)
    lc = q_ref.shape[1]
    nh = MLSTM_HEADS
    dh = MLSTM_DH

    @pl.when(j == 0)
    def _():
        c_scr[...] = jnp.zeros_like(c_scr)
        n_scr[...] = jnp.zeros_like(n_scr)
        m_scr[...] = jnp.full(m_scr.shape, M_INIT, F32)

    row = lax.broadcasted_iota(jnp.int32, (lc, lc), 0)
    col = lax.broadcasted_iota(jnp.int32, (lc, lc), 1)
    lo = jnp.where(d == 1, row, col)
    hi = jnp.where(d == 1, col, row)
    tri = lo <= hi
    trif = tri.astype(F32)

    gc = gc_ref[0] + bc_ref[0]
    gr = gr_ref[0, 0] + br_ref[0]
    lfr = _log_sigmoid(gr)
    bcum_c = jnp.dot(trif, _log_sigmoid(gc), precision=HIGHEST, preferred_element_type=F32)
    bcum_r = _dot_nt(lfr, trif, precision=HIGHEST)

    for h in range(nh):
        sl = slice(h * dh, (h + 1) * dh)
        q = q_ref[0, :, sl]
        k = k_ref[0, :, sl]
        v = v_ref[0, :, sl]
        i_col = gc[:, h:h + 1]
        b_col = bcum_c[:, nh + h:nh + h + 1]
        i_row = gr[h:h + 1, :]
        b_row = bcum_r[nh + h:nh + h + 1, :]
        m_prev = m_scr[h][0:1, 0:1]
        c_mat = c_scr[h]
        n_vec = n_scr[h]

        log_intra = jnp.where(tri, b_col - b_row + i_row, NEG_BIG)
        log_inter = b_col + m_prev
        m_t = jnp.maximum(log_inter, jnp.max(log_intra, axis=1, keepdims=True))
        w_inter = jnp.exp(log_inter - m_t)
        scores = _dot_nt(q, k) * jnp.exp(log_intra - m_t)
        num = w_inter * _dot(q, c_mat.astype(BF16)) + _dot(scores.astype(BF16), v)
        den = (w_inter * jnp.sum(q.astype(F32) * n_vec, axis=1, keepdims=True)
               + jnp.sum(scores, axis=1, keepdims=True))
        hh = num / jnp.maximum(jnp.abs(den), jnp.exp(-m_t))
        h_ref[0, 0, :, sl] = hh.astype(h_ref.dtype)

        total_f = jnp.sum(lfr[nh + h:nh + h + 1, :], axis=1, keepdims=True)
        log_w_row = total_f - b_row + i_row
        m_new = jnp.maximum(total_f + m_prev, jnp.max(log_w_row, axis=1, keepdims=True))
        decay = jnp.exp(total_f + m_prev - m_new)
        w_col = jnp.exp(total_f - b_col + i_col - m_new)
        wv = (w_col * v.astype(F32)).astype(BF16)
        c_scr[h] = decay * c_mat + _dot_tn(k, wv)
        n_scr[h] = decay * n_vec + jnp.sum(w_col * k.astype(F32), axis=0, keepdims=True)
        m_scr[h] = jnp.broadcast_to(m_new, m_scr.shape[1:])


def _chunk_order(n_ctx_chunks, nchunks):
    def order(d, j):
        bwd = jnp.where(j < n_ctx_chunks, n_ctx_chunks - 1 - j, nchunks - 1 - (j - n_ctx_chunks))
        return jnp.where(d == 0, j, bwd)
    return order


def _mlstm(q, k, v, gc, gr, gate_b, n_ctx):
    bsz, s, w = q.shape
    lc = ROW_TILE
    nchunks = s // lc
    order = _chunk_order(n_ctx // lc, nchunks)
    nh = MLSTM_HEADS
    gb = gate_b.reshape(2, 2 * nh)
    bc = jnp.pad(gb, ((0, 0), (0, GATE_PAD - 2 * nh))).reshape(2, 1, GATE_PAD)
    br = gb.reshape(2, 2 * nh, 1)
    gr4 = gr.reshape(bsz, 2, 2 * nh, s)
    row = lambda b, d, j: (b, order(d, j), 0)
    return pl.pallas_call(
        _mlstm_kernel,
        grid=(bsz, 2, nchunks),
        in_specs=[pl.BlockSpec((1, lc, w), row), pl.BlockSpec((1, lc, w), row),
                  pl.BlockSpec((1, lc, w), row),
                  pl.BlockSpec((1, lc, GATE_PAD), lambda b, d, j: (b, order(d, j), d)),
                  pl.BlockSpec((1, 1, 2 * nh, lc), lambda b, d, j: (b, d, 0, order(d, j))),
                  pl.BlockSpec((1, 1, GATE_PAD), lambda b, d, j: (d, 0, 0)),
                  pl.BlockSpec((1, 2 * nh, 1), lambda b, d, j: (d, 0, 0))],
        out_specs=pl.BlockSpec((1, 1, lc, w), lambda b, d, j: (d, b, order(d, j), 0)),
        out_shape=jax.ShapeDtypeStruct((2, bsz, s, w), BF16),
        scratch_shapes=[pltpu.VMEM((nh, MLSTM_DH, MLSTM_DH), F32),
                        pltpu.VMEM((nh, 1, MLSTM_DH), F32),
                        pltpu.VMEM((nh, 8, V7X_LANES), F32)],
        compiler_params=_cparams(("arbitrary", "arbitrary", "arbitrary"), 48),
        name="mlstm",
    )(q, k, v, gc, gr4, bc, br)


def _lru_kernel(reverse, x_ref, wa_ref, wx_ref, ba_ref, bx_ref, lam_ref, h_ref, a_scr, b_scr, carry):
    j = pl.program_id(1)
    t_rows = x_ref.shape[1]
    bw = V7X_MXU_DIM

    @pl.when(j == 0)
    def _():
        carry[...] = jnp.zeros_like(carry)

    x = x_ref[0]
    xb = x.astype(BF16)
    sp = _softplus(-lam_ref[...])
    for jj in range(LRU_W // bw):
        sl = slice(jj * bw, (jj + 1) * bw)
        r = _sigmoid(_dot(xb[:, sl], wa_ref[jj]) + ba_ref[:, sl])
        gi = _sigmoid(_dot(xb[:, sl], wx_ref[jj]) + bx_ref[:, sl])
        a = jnp.exp(-LRU_C * r * sp[:, sl])
        a_scr[:, sl] = a
        b_scr[:, sl] = jnp.sqrt(1.0 - a * a) * gi * x[:, sl]

    def body(t, hc):
        tt = t_rows - 1 - t if reverse else t
        hn = a_scr[pl.ds(tt, 1), :] * hc + b_scr[pl.ds(tt, 1), :]
        b_scr[pl.ds(tt, 1), :] = hn
        return hn

    carry[...] = lax.fori_loop(0, t_rows, body, carry[...], unroll=8)
    h_ref[0] = b_scr[...].astype(h_ref.dtype)


def _lru_blockdiag(w):
    per = V7X_MXU_DIM // LRU_BW
    nt = LRU_BLOCKS // per
    w4 = w.reshape(nt, per, LRU_BW, LRU_BW)
    eye = jnp.eye(per, dtype=w.dtype)
    t = jnp.einsum('tpcd,pq->tpcqd', w4, eye)
    return t.reshape(nt, V7X_MXU_DIM, V7X_MXU_DIM).astype(BF16)


def _lru(xc, wa, wx, ba, bx, lam, n_ctx, reverse):
    bsz, s, w = xc.shape
    tm = ROW_TILE
    nchunks = s // tm
    order = _chunk_order(n_ctx // tm, nchunks)
    d = 1 if reverse else 0
    row = lambda b, j: (b, order(d, j), 0)
    ntile = w // V7X_MXU_DIM
    full3 = pl.BlockSpec((ntile, V7X_MXU_DIM, V7X_MXU_DIM), lambda b, j: (0, 0, 0))
    vec = pl.BlockSpec((1, w), lambda b, j: (0, 0))
    return pl.pallas_call(
        functools.partial(_lru_kernel, reverse),
        grid=(bsz, nchunks),
        in_specs=[pl.BlockSpec((1, tm, w), row), full3, full3, vec, vec, vec],
        out_specs=pl.BlockSpec((1, tm, w), row),
        out_shape=jax.ShapeDtypeStruct((bsz, s, w), BF16),
        scratch_shapes=[pltpu.VMEM((tm, w), F32), pltpu.VMEM((tm, w), F32), pltpu.VMEM((1, w), F32)],
        compiler_params=_cparams(("arbitrary", "arbitrary"), 32),
        name="lru_bwd" if reverse else "lru_fwd",
    )(xc, _lru_blockdiag(wa), _lru_blockdiag(wx), ba.reshape(1, w), bx.reshape(1, w), lam.reshape(1, w))


def _tail(y, x_ref, mod, n2_ref, wr_ref, br_ref, xo_ref, v_ref, lg_ref):
    xn = x_ref[0] + mod[2:3] * y
    xo_ref[0] = xn
    v = _modnorm(xn, n2_ref[...], mod[4:5], mod[3:4])
    v_ref[0] = v.astype(v_ref.dtype)
    lg_ref[0] = jnp.dot(v, wr_ref[...], precision=HIGHEST, preferred_element_type=F32) + br_ref[...]


def _even_out_kernel(hm_ref, hl0_ref, hl1_ref, o_ref, yg_ref, mg_ref, wout_ref,
                     x_ref, mod_ref, n2_ref, wr_ref, br_ref, xo_ref, v_ref, lg_ref):
    hm = hm_ref[0, 0].astype(F32) + hm_ref[1, 0].astype(F32)
    parts = []
    for h in range(MLSTM_HEADS):
        sl = slice(h * MLSTM_DH, (h + 1) * MLSTM_DH)
        parts.append(_rms(hm[:, sl], mg_ref[:, sl]))
    hmn = jnp.concatenate(parts, axis=1) * _sigmoid(o_ref[0].astype(F32))
    hl = (hl0_ref[0].astype(F32) + hl1_ref[0].astype(F32)) * _gelu_tanh(yg_ref[0].astype(F32))
    y = (_dot(hmn.astype(BF16), wout_ref[0:MLSTM_W, :])
         + _dot(hl.astype(BF16), wout_ref[MLSTM_W:MLSTM_W + LRU_W, :]))
    _tail(y, x_ref, mod_ref[0, 0], n2_ref, wr_ref, br_ref, xo_ref, v_ref, lg_ref)


def _even_out(hm, hl0, hl1, o_pre, yg, mnorm_g, w_out, x, mod, n2, w_r, b_r, n_ctx):
    bsz, s, d = x.shape
    tm = ROW_TILE
    nt = s // tm
    nct = n_ctx // tm
    row = lambda b, i: (b, i, 0)
    full = lambda shape: pl.BlockSpec(shape, lambda b, i: (0,) * len(shape))
    return pl.pallas_call(
        _even_out_kernel,
        grid=(bsz, nt),
        in_specs=[pl.BlockSpec((2, 1, tm, MLSTM_W), lambda b, i: (0, b, i, 0)),
                  pl.BlockSpec((1, tm, LRU_W), row), pl.BlockSpec((1, tm, LRU_W), row),
                  pl.BlockSpec((1, tm, MLSTM_W), row), pl.BlockSpec((1, tm, LRU_W), row),
                  full((1, MLSTM_W)), full((MLSTM_W + LRU_W, d)),
                  pl.BlockSpec((1, tm, d), row),
                  pl.BlockSpec((1, 1, 8, d), lambda b, i: (b, (i >= nct).astype(jnp.int32), 0, 0)),
                  full((1, d)), full((d, ROUTER_PAD)), full((1, ROUTER_PAD))],
        out_specs=[pl.BlockSpec((1, tm, d), row), pl.BlockSpec((1, tm, d), row),
                   pl.BlockSpec((1, tm, ROUTER_PAD), row)],
        out_shape=[jax.ShapeDtypeStruct((bsz, s, d), F32),
                   jax.ShapeDtypeStruct((bsz, s, d), BF16),
                   jax.ShapeDtypeStruct((bsz, s, ROUTER_PAD), F32)],
        compiler_params=_cparams(("arbitrary", "arbitrary"), 48),
        name="even_out",
    )(hm, hl0, hl1, o_pre, yg, mnorm_g.reshape(1, MLSTM_W), w_out, x, mod, n2.reshape(1, d), w_r, b_r)


def _odd_out_kernel(a_ref, wout_ref, x_ref, mod_ref, n2_ref, wr_ref, br_ref, xo_ref, v_ref, lg_ref):
    y = _dot(a_ref[0], wout_ref[...])
    _tail(y, x_ref, mod_ref[0, 0], n2_ref, wr_ref, br_ref, xo_ref, v_ref, lg_ref)


def _odd_out(attn, w_out, x, mod, n2, w_r, b_r, n_ctx):
    bsz, seq, d = attn.shape
    tm = ROW_TILE
    nt = seq // tm
    nct = n_ctx // tm
    row = lambda b, i: (b, i, 0)
    full = lambda shape: pl.BlockSpec(shape, lambda b, i: (0,) * len(shape))
    return pl.pallas_call(
        _odd_out_kernel,
        grid=(bsz, nt),
        in_specs=[pl.BlockSpec((1, tm, d), row), full((d, d)),
                  pl.BlockSpec((1, tm, d), lambda b, i: (b, i + nct, 0)),
                  pl.BlockSpec((1, 1, 8, d), lambda b, i: (b, 1, 0, 0)),
                  full((1, d)), full((d, ROUTER_PAD)), full((1, ROUTER_PAD))],
        out_specs=[pl.BlockSpec((1, tm, d), row), pl.BlockSpec((1, tm, d), row),
                   pl.BlockSpec((1, tm, ROUTER_PAD), row)],
        out_shape=[jax.ShapeDtypeStruct((bsz, seq, d), F32),
                   jax.ShapeDtypeStruct((bsz, seq, d), BF16),
                   jax.ShapeDtypeStruct((bsz, seq, ROUTER_PAD), F32)],
        compiler_params=_cparams(("arbitrary", "arbitrary"), 32),
        name="odd_out",
    )(attn, w_out, x, mod, n2.reshape(1, d), w_r, b_r)


def _expert_kernel(te_ref, nu_ref, x_ref, w1_ref, b1_ref, w2_ref, b2_ref, rw_ref, o_ref):
    t = pl.program_id(0)

    @pl.when(t < nu_ref[0])
    def _():
        hid = _dot(x_ref[...], w1_ref[0]) + b1_ref[0]
        gate = jnp.minimum(hid[:, :D_FF], SWIGLU_LIMIT)
        up = jnp.clip(hid[:, D_FF:], -SWIGLU_LIMIT, SWIGLU_LIMIT)
        act = (up + 1.0) * gate * _sigmoid(SWIGLU_ALPHA * gate)
        y = _dot(act.astype(BF16), w2_ref[0]) + b2_ref[0]
        o_ref[...] = (y * rw_ref[...]).astype(o_ref.dtype)

    @pl.when(t >= nu_ref[0])
    def _():
        o_ref[...] = jnp.zeros_like(o_ref)


def _experts(xs, row_w, tile_expert, n_used, w1, b1, w2, b2):
    rows, d = xs.shape
    tm = MOE_TILE
    nt = rows // tm
    ne, _, ff2 = w1.shape
    return pl.pallas_call(
        _expert_kernel,
        grid_spec=pltpu.PrefetchScalarGridSpec(
            num_scalar_prefetch=2,
            grid=(nt,),
            in_specs=[pl.BlockSpec((tm, d), lambda t, te, nu: (t, 0)),
                      pl.BlockSpec((1, d, ff2), lambda t, te, nu: (te[t], 0, 0)),
                      pl.BlockSpec((1, 1, ff2), lambda t, te, nu: (te[t], 0, 0)),
                      pl.BlockSpec((1, ff2 // 2, d), lambda t, te, nu: (te[t], 0, 0)),
                      pl.BlockSpec((1, 1, d), lambda t, te, nu: (te[t], 0, 0)),
                      pl.BlockSpec((tm, 1), lambda t, te, nu: (t, 0))],
            out_specs=pl.BlockSpec((tm, d), lambda t, te, nu: (t, 0)),
        ),
        out_shape=jax.ShapeDtypeStruct((rows, d), BF16),
        compiler_params=_cparams(("arbitrary",), 48),
        name="moe_experts",
    )(tile_expert, n_used, xs, w1, b1.reshape(ne, 1, ff2), w2, b2.reshape(ne, 1, d), row_w)


def _moe(v, logits, w1, b1, w2, b2):
    t, d = v.shape
    tm = MOE_TILE
    nrows = t * TOP_K
    nt = -(-nrows // tm) + N_EXPERTS
    top_logit, top_idx = lax.top_k(logits[:, :N_EXPERTS], TOP_K)
    weight = jax.nn.softmax(top_logit, axis=-1)
    expert = top_idx.reshape(-1).astype(jnp.int32)
    order = jnp.argsort(expert).astype(jnp.int32)
    e_sorted = expert[order]
    sizes = jnp.sum(expert[:, None] == jnp.arange(N_EXPERTS, dtype=jnp.int32)[None, :], axis=0,
                    dtype=jnp.int32)
    start = jnp.cumsum(sizes) - sizes
    padded = (sizes + tm - 1) // tm * tm
    pad_end = jnp.cumsum(padded)
    pos_sorted = pad_end[e_sorted] - padded[e_sorted] + jnp.arange(nrows, dtype=jnp.int32) - start[e_sorted]
    row_token = jnp.zeros((nt * tm,), jnp.int32).at[pos_sorted].set(order // TOP_K)
    row_w = jnp.zeros((nt * tm,), F32).at[pos_sorted].set(weight.reshape(-1)[order])
    pair_pos = jnp.zeros((nrows,), jnp.int32).at[order].set(pos_sorted)
    tile_expert = jnp.minimum(
        jnp.searchsorted(pad_end, jnp.arange(nt, dtype=jnp.int32) * tm, side='right'),
        N_EXPERTS - 1).astype(jnp.int32)
    n_used = (pad_end[-1] // tm).reshape(1).astype(jnp.int32)
    xs = jnp.take(v, row_token, axis=0)
    ys = _experts(xs, row_w.reshape(-1, 1), tile_expert, n_used, w1, b1, w2, b2)
    pp = pair_pos.reshape(t, TOP_K)
    return [jnp.take(ys, pp[:, kk], axis=0) for kk in range(TOP_K)]


def _rope(t, cos, sin, lane_lo):
    swapped = jnp.where(lane_lo, pltpu.roll(t, ATT_DH - ROPE_AXIS_DIM // 2, 1),
                        pltpu.roll(t, ROPE_AXIS_DIM // 2, 1))
    return t * cos + swapped * sin


def _proj_odd_kernel(x_ref, f0, f1, f2, f3, mod0_ref, mod_ref, g_ref, w_ref, qg_ref, kg_ref,
                     cos_ref, sin_ref, h_ref, q_ref, k_ref, v_ref):
    f = (f0[0].astype(F32) + f1[0].astype(F32)) + (f2[0].astype(F32) + f3[0].astype(F32))
    hcur = x_ref[0] + mod0_ref[0, 0][5:6] * f
    h_ref[0] = hcur
    mod = mod_ref[0, 0]
    u = _modnorm(hcur, g_ref[...], mod[1:2], mod[0:1]).astype(BF16)
    z = _dot(u, w_ref[...])
    cos = cos_ref[...]
    sin = sin_ref[...]
    lane = lax.broadcasted_iota(jnp.int32, cos.shape, 1)
    lane_lo = (lane % ROPE_AXIS_DIM) < (ROPE_AXIS_DIM // 2)
    qw = ATT_HEADS * ATT_DH
    kw = ATT_KV_HEADS * ATT_DH
    for hh in range(ATT_HEADS):
        sl = slice(hh * ATT_DH, (hh + 1) * ATT_DH)
        t = _rope(_rms(z[:, sl], qg_ref[...]), cos, sin, lane_lo)
        q_ref[0, :, sl] = (t * (ATT_DH ** -0.5)).astype(q_ref.dtype)
    for hh in range(ATT_KV_HEADS):
        sl = slice(hh * ATT_DH, (hh + 1) * ATT_DH)
        t = _rope(_rms(z[:, qw + hh * ATT_DH:qw + (hh + 1) * ATT_DH], kg_ref[...]), cos, sin, lane_lo)
        k_ref[0, :, sl] = t.astype(k_ref.dtype)
    v_ref[0] = z[:, qw + kw:qw + 2 * kw].astype(v_ref.dtype)


def _proj_odd(x, fparts, mod0, mod, g, w, qg, kg, cos_tab, sin_tab, n_ctx):
    bsz, s, d = x.shape
    tm = ROW_TILE
    nt = s // tm
    nct = n_ctx // tm
    n = w.shape[1]
    qw = ATT_HEADS * ATT_DH
    kw = ATT_KV_HEADS * ATT_DH
    row = lambda b, i: (b, i, 0)
    seg = lambda b, i: (b, (i >= nct).astype(jnp.int32), 0, 0)
    full = lambda shape: pl.BlockSpec(shape, lambda b, i: (0,) * len(shape))
    tab = pl.BlockSpec((tm, ATT_DH), lambda b, i: (i, 0))
    return pl.pallas_call(
        _proj_odd_kernel,
        grid=(bsz, nt),
        in_specs=[pl.BlockSpec((1, tm, d), row)] + [pl.BlockSpec((1, tm, d), row)] * TOP_K
                 + [pl.BlockSpec((1, 1, 8, d), seg), pl.BlockSpec((1, 1, 8, d), seg),
                    full((1, d)), full((d, n)), full((1, ATT_DH)), full((1, ATT_DH)), tab, tab],
        out_specs=[pl.BlockSpec((1, tm, d), row), pl.BlockSpec((1, tm, qw), row),
                   pl.BlockSpec((1, tm, kw), row), pl.BlockSpec((1, tm, kw), row)],
        out_shape=[jax.ShapeDtypeStruct((bsz, s, d), F32),
                   jax.ShapeDtypeStruct((bsz, s, qw), BF16),
                   jax.ShapeDtypeStruct((bsz, s, kw), BF16),
                   jax.ShapeDtypeStruct((bsz, s, kw), BF16)],
        compiler_params=_cparams(("arbitrary", "arbitrary"), 48),
        name="proj_odd",
    )(x, *fparts, mod0, mod, g.reshape(1, d), w, qg.reshape(1, ATT_DH), kg.reshape(1, ATT_DH),
      cos_tab, sin_tab)


def _rope_tables(n_ctx, seq):
    rows = seq // GRID_W
    pos_r = jnp.repeat(jnp.arange(rows), GRID_W).astype(F32)
    pos_c = jnp.tile(jnp.arange(GRID_W), rows).astype(F32)
    inv_freq = ROPE_THETA ** (-jnp.arange(0, ROPE_AXIS_DIM, 2, dtype=F32) / ROPE_AXIS_DIM)
    ar = pos_r[:, None] * inv_freq
    ac = pos_c[:, None] * inv_freq
    cos = jnp.concatenate([jnp.cos(ar), jnp.cos(ar), jnp.cos(ac), jnp.cos(ac)], axis=-1)
    sin = jnp.concatenate([-jnp.sin(ar), jnp.sin(ar), -jnp.sin(ac), jnp.sin(ac)], axis=-1)
    cos = jnp.concatenate([jnp.ones((n_ctx, ATT_DH), F32), cos], axis=0)
    sin = jnp.concatenate([jnp.zeros((n_ctx, ATT_DH), F32), sin], axis=0)
    return cos, sin


def _attn_kernel(q_ref, k_ref, v_ref, o_ref):
    k = k_ref[0]
    v = v_ref[0]
    for g in range(ATT_GROUP):
        sl = slice(g * ATT_DH, (g + 1) * ATT_DH)
        s = _dot_nt(q_ref[0, :, sl], k)
        p = jnp.exp(s - jnp.max(s, axis=1, keepdims=True))
        l = jnp.sum(p, axis=1, keepdims=True)
        o_ref[0, :, sl] = (_dot(p.astype(BF16), v) / l).astype(o_ref.dtype)


def _attention(q, k, v, n_ctx):
    bsz, s, qw = q.shape
    seq = s - n_ctx
    tq = ROW_TILE
    nct = n_ctx // tq
    gw = ATT_GROUP * ATT_DH
    return pl.pallas_call(
        _attn_kernel,
        grid=(bsz, ATT_KV_HEADS, seq // tq),
        in_specs=[pl.BlockSpec((1, tq, gw), lambda b, h, i: (b, i + nct, h)),
                  pl.BlockSpec((1, s, ATT_DH), lambda b, h, i: (b, 0, h)),
                  pl.BlockSpec((1, s, ATT_DH), lambda b, h, i: (b, 0, h))],
        out_specs=pl.BlockSpec((1, tq, gw), lambda b, h, i: (b, i, h)),
        out_shape=jax.ShapeDtypeStruct((bsz, seq, qw), BF16),
        compiler_params=_cparams(("arbitrary", "arbitrary", "arbitrary"), 48),
        name="attention",
    )(q, k, v)


def _final_kernel(x_ref, f0, f1, f2, f3, mod_ref, g_ref, o_ref):
    f = (f0[0].astype(F32) + f1[0].astype(F32)) + (f2[0].astype(F32) + f3[0].astype(F32))
    o_ref[0] = _rms(x_ref[0] + mod_ref[0, 0][5:6] * f, g_ref[...])


def _final(x, fparts, mod, g):
    bsz, seq, d = x.shape
    tm = ROW_TILE
    row = lambda b, i: (b, i, 0)
    return pl.pallas_call(
        _final_kernel,
        grid=(bsz, seq // tm),
        in_specs=[pl.BlockSpec((1, tm, d), row)] * (1 + TOP_K)
                 + [pl.BlockSpec((1, 1, 8, d), lambda b, i: (b, 1, 0, 0)),
                    pl.BlockSpec((1, d), lambda b, i: (0, 0))],
        out_specs=pl.BlockSpec((1, tm, d), row),
        out_shape=jax.ShapeDtypeStruct((bsz, seq, d), F32),
        compiler_params=_cparams(("arbitrary", "arbitrary"), 32),
        name="final_norm",
    )(x, *fparts, mod, g.reshape(1, d))


def _pack_even_w_in(w_in):
    w4 = 4 * MLSTM_W
    ng = 4 * MLSTM_HEADS
    wg = w_in[:, w4:w4 + ng]
    half = ng // 2
    pad = jnp.zeros((w_in.shape[0], GATE_PAD - half), w_in.dtype)
    packed = jnp.concatenate([w_in[:, :w4], w_in[:, w4 + ng:], wg[:, :half], pad, wg[:, half:], pad], axis=1)
    return packed.astype(BF16), wg.T.astype(BF16)


def _pad_router(w_r, b_r):
    d, ne = w_r.shape
    return (jnp.pad(w_r, ((0, 0), (0, ROUTER_PAD - ne))),
            jnp.pad(b_r, (0, ROUTER_PAD - ne)).reshape(1, ROUTER_PAD))


def kernel(x, c, ctx, c_ctx, mod_w, mod_b, norm1_g, norm2_g, final_g, ev_w_in, ev_qk_conv_w, ev_qk_conv_b, ev_gate_b, ev_mnorm_g, ev_lru_conv_w, ev_lru_conv_b, ev_lru_wa, ev_lru_ba, ev_lru_wx, ev_lru_bx, ev_lru_lam, ev_w_out, od_w_in, od_q_norm_g, od_k_norm_g, od_w_out, moe_w_r, moe_b_r, moe_w1, moe_b1, moe_w2, moe_b2):
    bsz, seq, d = x.shape
    n_ctx = ctx.shape[1]
    s = n_ctx + seq
    assert n_ctx % ROW_TILE == 0 and seq % ROW_TILE == 0 and seq % GRID_W == 0
    h = jnp.concatenate([ctx, x], axis=1)

    mod0 = _mod_table(c, c_ctx, mod_w[0], mod_b[0])
    w_packed, wg_t = _pack_even_w_in(ev_w_in[0])
    qk_pre, v, o_pre, xr, yg, gc, gr = _proj_even(h, mod0, norm1_g[0], w_packed, wg_t, n_ctx)
    q, k, xc = _conv(qk_pre, xr, ev_qk_conv_w[0], ev_qk_conv_b[0], ev_lru_conv_w[0], ev_lru_conv_b[0], n_ctx)
    hm = _mlstm(q, k, v, gc, gr, ev_gate_b[0], n_ctx)
    hl = [_lru(xc, ev_lru_wa[0, dd], ev_lru_wx[0, dd], ev_lru_ba[0, dd], ev_lru_bx[0, dd],
               ev_lru_lam[0, dd], n_ctx, dd == 1) for dd in range(2)]
    w_r0, b_r0 = _pad_router(moe_w_r[0], moe_b_r[0])
    x_mid, v0, lg0 = _even_out(hm, hl[0], hl[1], o_pre, yg, ev_mnorm_g[0], ev_w_out[0].astype(BF16),
                               h, mod0, norm2_g[0], w_r0, b_r0, n_ctx)
    f0 = _moe(v0.reshape(bsz * s, d), lg0.reshape(bsz * s, ROUTER_PAD),
              moe_w1[0].astype(BF16), moe_b1[0], moe_w2[0].astype(BF16), moe_b2[0])
    f0 = [p.reshape(bsz, s, d) for p in f0]

    mod1 = _mod_table(c, c_ctx, mod_w[1], mod_b[1])
    cos_tab, sin_tab = _rope_tables(n_ctx, seq)
    h1, q1, k1, v1 = _proj_odd(x_mid, f0, mod0, mod1, norm1_g[1], od_w_in[0].astype(BF16),
                               od_q_norm_g[0], od_k_norm_g[0], cos_tab, sin_tab, n_ctx)
    attn = _attention(q1, k1, v1, n_ctx)
    w_r1, b_r1 = _pad_router(moe_w_r[1], moe_b_r[1])
    x2, v2, lg2 = _odd_out(attn, od_w_out[0].astype(BF16), h1, mod1, norm2_g[1], w_r1, b_r1, n_ctx)
    f1 = _moe(v2.reshape(bsz * seq, d), lg2.reshape(bsz * seq, ROUTER_PAD),
              moe_w1[1].astype(BF16), moe_b1[1], moe_w2[1].astype(BF16), moe_b2[1])
    f1 = [p.reshape(bsz, seq, d) for p in f1]
    return _final(x2, f1, mod1, final_g)
```

```python
import functools

import jax
import jax.numpy as jnp
from jax import lax
from jax.experimental import pallas as pl
from jax.experimental.pallas import tpu as pltpu

F32 = jnp.float32
BF16 = jnp.bfloat16
HIGHEST = lax.Precision.HIGHEST

EPS = 1e-6
M_INIT = -1e30
NEG_BIG = -1e30

MLSTM_HEADS = 4
MLSTM_DH = 256
MLSTM_W = MLSTM_HEADS * MLSTM_DH
LRU_W = 1024
LRU_BLOCKS = 16
LRU_BW = LRU_W // LRU_BLOCKS
LRU_C = 8.0
CONV_W = 4
CONV_LEFT = 2
ATT_HEADS = 8
ATT_KV_HEADS = 2
ATT_GROUP = ATT_HEADS // ATT_KV_HEADS
ATT_DH = 128
GRID_W = 64
ROPE_AXIS_DIM = ATT_DH // 2
ROPE_THETA = 10000.0
N_EXPERTS = 32
TOP_K = 4
D_FF = 1024
SWIGLU_ALPHA = 1.702
SWIGLU_LIMIT = 7.0

V7X_LANES = 128
V7X_MXU_DIM = 256
V7X_VMEM_BYTES = 64 * 1024 * 1024
MIB = 1024 * 1024

ROW_TILE = 256
HALO = 16
MOE_TILE = 256
ROUTER_PAD = V7X_LANES
GATE_PAD = V7X_LANES


def _cparams(semantics, vmem_mib):
    assert vmem_mib * MIB < V7X_VMEM_BYTES
    return pltpu.CompilerParams(dimension_semantics=semantics, vmem_limit_bytes=vmem_mib * MIB)


def _dot(a, b):
    return jnp.dot(a, b, preferred_element_type=F32)


def _dot_nt(a, b, precision=None):
    return lax.dot_general(a, b, (((1,), (1,)), ((), ())), precision=precision,
                           preferred_element_type=F32)


def _dot_tn(a, b):
    return lax.dot_general(a, b, (((0,), (0,)), ((), ())), preferred_element_type=F32)


def _sigmoid(x):
    return jax.nn.sigmoid(x)


def _log_sigmoid(x):
    return jnp.minimum(x, 0.0) - jnp.log1p(jnp.exp(-jnp.abs(x)))


def _softplus(x):
    return jnp.maximum(x, 0.0) + jnp.log1p(jnp.exp(-jnp.abs(x)))


def _gelu_tanh(x):
    return 0.5 * x * (1.0 + jnp.tanh(0.7978845608028654 * (x + 0.044715 * x * x * x)))


def _rms(x, g):
    return x * lax.rsqrt(jnp.mean(x * x, axis=-1, keepdims=True) + EPS) * g


def _modnorm(x, g, scale, shift):
    return _rms(x, g) * (1.0 + scale) + shift


def _modvec_kernel(c_ref, w_ref, b_ref, o_ref):
    c = c_ref[...]
    s = c * _sigmoid(c)
    o_ref[...] = jnp.dot(s, w_ref[...], precision=HIGHEST, preferred_element_type=F32) + b_ref[...]


def _modvec(cc, w, b):
    rows, d = cc.shape
    n = w.shape[1]
    tn = 1536
    return pl.pallas_call(
        _modvec_kernel,
        grid=(n // tn,),
        in_specs=[pl.BlockSpec((rows, d), lambda j: (0, 0)),
                  pl.BlockSpec((d, tn), lambda j: (0, j)),
                  pl.BlockSpec((1, tn), lambda j: (0, j))],
        out_specs=pl.BlockSpec((rows, tn), lambda j: (0, j)),
        out_shape=jax.ShapeDtypeStruct((rows, n), F32),
        compiler_params=_cparams(("arbitrary",), 32),
        name="modvec",
    )(cc, w, b.reshape(1, n))


def _mod_table(c, c_ctx, mod_w, mod_b):
    bsz, d = c.shape
    rows = ((bsz + 1 + 7) // 8) * 8
    cc = jnp.zeros((rows, d), F32).at[:bsz].set(c).at[bsz].set(c_ctx)
    mod = _modvec(cc, mod_w, mod_b)
    lat = mod[:bsz].reshape(bsz, 6, d)
    ctx = jnp.broadcast_to(mod[bsz].reshape(1, 6, d), (bsz, 6, d))
    tbl = jnp.stack([ctx, lat], axis=1)
    return jnp.pad(tbl, ((0, 0), (0, 0), (0, 2), (0, 0)))


def _proj_even_kernel(x_ref, mod_ref, g_ref, w_ref, wgt_ref,
                      qk_ref, v_ref, o_ref, xr_ref, yg_ref, gc_ref, gr_ref):
    mod = mod_ref[0, 0]
    u = _modnorm(x_ref[0], g_ref[...], mod[1:2], mod[0:1]).astype(BF16)
    w = MLSTM_W
    qk_ref[0] = _dot(u, w_ref[:, 0:2 * w]).astype(qk_ref.dtype)
    v_ref[0] = _dot(u, w_ref[:, 2 * w:3 * w]).astype(v_ref.dtype)
    o_ref[0] = _dot(u, w_ref[:, 3 * w:4 * w]).astype(o_ref.dtype)
    xr_ref[0] = _dot(u, w_ref[:, 4 * w:4 * w + LRU_W]).astype(xr_ref.dtype)
    yg_ref[0] = _dot(u, w_ref[:, 4 * w + LRU_W:4 * w + 2 * LRU_W]).astype(yg_ref.dtype)
    gc_ref[0] = _dot(u, w_ref[:, 4 * w + 2 * LRU_W:4 * w + 2 * LRU_W + 2 * GATE_PAD])
    gr_ref[0] = _dot_nt(wgt_ref[...], u)


def _proj_even(h, mod, g, w_packed, wg_t, n_ctx):
    bsz, s, d = h.shape
    tm = ROW_TILE
    nt = s // tm
    nct = n_ctx // tm
    ntot = w_packed.shape[1]
    ng = wg_t.shape[0]
    row = lambda b, i: (b, i, 0)
    return pl.pallas_call(
        _proj_even_kernel,
        grid=(bsz, nt),
        in_specs=[pl.BlockSpec((1, tm, d), row),
                  pl.BlockSpec((1, 1, 8, d), lambda b, i: (b, (i >= nct).astype(jnp.int32), 0, 0)),
                  pl.BlockSpec((1, d), lambda b, i: (0, 0)),
                  pl.BlockSpec((d, ntot), lambda b, i: (0, 0)),
                  pl.BlockSpec((ng, d), lambda b, i: (0, 0))],
        out_specs=[pl.BlockSpec((1, tm, 2 * MLSTM_W), row),
                   pl.BlockSpec((1, tm, MLSTM_W), row),
                   pl.BlockSpec((1, tm, MLSTM_W), row),
                   pl.BlockSpec((1, tm, LRU_W), row),
                   pl.BlockSpec((1, tm, LRU_W), row),
                   pl.BlockSpec((1, tm, 2 * GATE_PAD), row),
                   pl.BlockSpec((1, ng, tm), lambda b, i: (b, 0, i))],
        out_shape=[jax.ShapeDtypeStruct((bsz, s, 2 * MLSTM_W), BF16),
                   jax.ShapeDtypeStruct((bsz, s, MLSTM_W), BF16),
                   jax.ShapeDtypeStruct((bsz, s, MLSTM_W), BF16),
                   jax.ShapeDtypeStruct((bsz, s, LRU_W), F32),
                   jax.ShapeDtypeStruct((bsz, s, LRU_W), BF16),
                   jax.ShapeDtypeStruct((bsz, s, 2 * GATE_PAD), F32),
                   jax.ShapeDtypeStruct((bsz, ng, s), F32)],
        compiler_params=_cparams(("arbitrary", "arbitrary"), 48),
        name="proj_even",
    )(h, mod, g.reshape(1, d), w_packed, wg_t)


def _conv_kernel(nct, qk_m, qk_p, qk_n, xr_m, xr_p, xr_n, wqk_ref, bqk_ref, wxr_ref, bxr_ref,
                 q_ref, k_ref, xc_ref, ext_qk, ext_xr):
    i = pl.program_id(1)
    nt = pl.num_programs(1)
    tm = qk_m.shape[1]
    first = jnp.logical_or(i == 0, i == nct)
    last = jnp.logical_or(i == nct - 1, i == nt - 1)
    pm = jnp.where(first, 0.0, 1.0)
    nm = jnp.where(last, 0.0, 1.0)

    def conv(main, prev, nxt, ext, w_ref, b_ref):
        ext[0:HALO] = prev[0].astype(F32) * pm
        ext[HALO:HALO + tm] = main[0].astype(F32)
        ext[HALO + tm:2 * HALO + tm] = nxt[0].astype(F32) * nm
        acc = b_ref[...] + w_ref[0:1, :] * ext[pl.ds(HALO - CONV_LEFT, tm), :]
        for j in range(1, CONV_W):
            acc = acc + w_ref[j:j + 1, :] * ext[pl.ds(HALO - CONV_LEFT + j, tm), :]
        return acc

    y = conv(qk_m, qk_p, qk_n, ext_qk, wqk_ref, bqk_ref)
    y = y * _sigmoid(y)
    q_ref[0] = y[:, :MLSTM_W].astype(q_ref.dtype)
    k_ref[0] = (y[:, MLSTM_W:] * (MLSTM_DH ** -0.5)).astype(k_ref.dtype)
    xc_ref[0] = conv(xr_m, xr_p, xr_n, ext_xr, wxr_ref, bxr_ref)


def _conv(qk_pre, xr, wqk, bqk, wxr, bxr, n_ctx):
    bsz, s, _ = qk_pre.shape
    tm = ROW_TILE
    nt = s // tm
    nct = n_ctx // tm
    hb = tm // HALO
    nhb = s // HALO
    row = lambda b, i: (b, i, 0)
    prev = lambda b, i: (b, jnp.maximum(i * hb - 1, 0), 0)
    nxt = lambda b, i: (b, jnp.minimum((i + 1) * hb, nhb - 1), 0)
    cq = 2 * MLSTM_W
    full = lambda shape: pl.BlockSpec(shape, lambda b, i: (0, 0))
    return pl.pallas_call(
        functools.partial(_conv_kernel, nct),
        grid=(bsz, nt),
        in_specs=[pl.BlockSpec((1, tm, cq), row), pl.BlockSpec((1, HALO, cq), prev),
                  pl.BlockSpec((1, HALO, cq), nxt),
                  pl.BlockSpec((1, tm, LRU_W), row), pl.BlockSpec((1, HALO, LRU_W), prev),
                  pl.BlockSpec((1, HALO, LRU_W), nxt),
                  full((CONV_W, cq)), full((1, cq)), full((CONV_W, LRU_W)), full((1, LRU_W))],
        out_specs=[pl.BlockSpec((1, tm, MLSTM_W), row), pl.BlockSpec((1, tm, MLSTM_W), row),
                   pl.BlockSpec((1, tm, LRU_W), row)],
        out_shape=[jax.ShapeDtypeStruct((bsz, s, MLSTM_W), BF16),
                   jax.ShapeDtypeStruct((bsz, s, MLSTM_W), BF16),
                   jax.ShapeDtypeStruct((bsz, s, LRU_W), F32)],
        scratch_shapes=[pltpu.VMEM((tm + 2 * HALO, cq), F32),
                        pltpu.VMEM((tm + 2 * HALO, LRU_W), F32)],
        compiler_params=_cparams(("arbitrary", "arbitrary"), 48),
        name="dwconv",
    )(qk_pre, qk_pre, qk_pre, xr, xr, xr, wqk, bqk.reshape(1, cq), wxr, bxr.reshape(1, LRU_W))


def _mlstm_kernel(q_ref, k_ref, v_ref, gc_ref, gr_ref, bc_ref, br_ref, h_ref, c_scr, n_scr, m_scr):
    d = pl.program_id(1)
    j = pl.program_id(2)
    lc = q_ref.shape[1]
    nh = MLSTM_HEADS
    dh = MLSTM_DH

    @pl.when(j == 0)
    def _():
        c_scr[...] = jnp.zeros_like(c_scr)
        n_scr[...] = jnp.zeros_like(n_scr)
        m_scr[...] = jnp.full(m_scr.shape, M_INIT, F32)

    row = lax.broadcasted_iota(jnp.int32, (lc, lc), 0)
    col = lax.broadcasted_iota(jnp.int32, (lc, lc), 1)
    lo = jnp.where(d == 1, row, col)
    hi = jnp.where(d == 1, col, row)
    tri = lo <= hi
    trif = tri.astype(F32)

    gc = gc_ref[0] + bc_ref[0]
    gr = gr_ref[0, 0] + br_ref[0]
    lfr = _log_sigmoid(gr)
    bcum_c = jnp.dot(trif, _log_sigmoid(gc), precision=HIGHEST, preferred_element_type=F32)
    bcum_r = _dot_nt(lfr, trif, precision=HIGHEST)

    for h in range(nh):
        sl = slice(h * dh, (h + 1) * dh)
        q = q_ref[0, :, sl]
        k = k_ref[0, :, sl]
        v = v_ref[0, :, sl]
        i_col = gc[:, h:h + 1]
        b_col = bcum_c[:, nh + h:nh + h + 1]
        i_row = gr[h:h + 1, :]
        b_row = bcum_r[nh + h:nh + h + 1, :]
        m_prev = m_scr[h][0:1, 0:1]
        c_mat = c_scr[h]
        n_vec = n_scr[h]

        log_intra = jnp.where(tri, b_col - b_row + i_row, NEG_BIG)
        log_inter = b_col + m_prev
        m_t = jnp.maximum(log_inter, jnp.max(log_intra, axis=1, keepdims=True))
        w_inter = jnp.exp(log_inter - m_t)
        scores = _dot_nt(q, k) * jnp.exp(log_intra - m_t)
        num = w_inter * _dot(q, c_mat.astype(BF16)) + _dot(scores.astype(BF16), v)
        den = (w_inter * jnp.sum(q.astype(F32) * n_vec, axis=1, keepdims=True)
               + jnp.sum(scores, axis=1, keepdims=True))
        hh = num / jnp.maximum(jnp.abs(den), jnp.exp(-m_t))
        h_ref[0, 0, :, sl] = hh.astype(h_ref.dtype)

        total_f = jnp.sum(lfr[nh + h:nh + h + 1, :], axis=1, keepdims=True)
        log_w_row = total_f - b_row + i_row
        m_new = jnp.maximum(total_f + m_prev, jnp.max(log_w_row, axis=1, keepdims=True))
        decay = jnp.exp(total_f + m_prev - m_new)
        w_col = jnp.exp(total_f - b_col + i_col - m_new)
        wv = (w_col * v.astype(F32)).astype(BF16)
        c_scr[h] = decay * c_mat + _dot_tn(k, wv)
        n_scr[h] = decay * n_vec + jnp.sum(w_col * k.astype(F32), axis=0, keepdims=True)
        m_scr[h] = jnp.broadcast_to(m_new, m_scr.shape[1:])


def _chunk_order(n_ctx_chunks, nchunks):
    def order(d, j):
        bwd = jnp.where(j < n_ctx_chunks, n_ctx_chunks - 1 - j, nchunks - 1 - (j - n_ctx_chunks))
        return jnp.where(d == 0, j, bwd)
    return order


def _mlstm(q, k, v, gc, gr, gate_b, n_ctx):
    bsz, s, w = q.shape
    lc = ROW_TILE
    nchunks = s // lc
    order = _chunk_order(n_ctx // lc, nchunks)
    nh = MLSTM_HEADS
    gb = gate_b.reshape(2, 2 * nh)
    bc = jnp.pad(gb, ((0, 0), (0, GATE_PAD - 2 * nh))).reshape(2, 1, GATE_PAD)
    br = gb.reshape(2, 2 * nh, 1)
    gr4 = gr.reshape(bsz, 2, 2 * nh, s)
    row = lambda b, d, j: (b, order(d, j), 0)
    return pl.pallas_call(
        _mlstm_kernel,
        grid=(bsz, 2, nchunks),
        in_specs=[pl.BlockSpec((1, lc, w), row), pl.BlockSpec((1, lc, w), row),
                  pl.BlockSpec((1, lc, w), row),
                  pl.BlockSpec((1, lc, GATE_PAD), lambda b, d, j: (b, order(d, j), d)),
                  pl.BlockSpec((1, 1, 2 * nh, lc), lambda b, d, j: (b, d, 0, order(d, j))),
                  pl.BlockSpec((1, 1, GATE_PAD), lambda b, d, j: (d, 0, 0)),
                  pl.BlockSpec((1, 2 * nh, 1), lambda b, d, j: (d, 0, 0))],
        out_specs=pl.BlockSpec((1, 1, lc, w), lambda b, d, j: (d, b, order(d, j), 0)),
        out_shape=jax.ShapeDtypeStruct((2, bsz, s, w), BF16),
        scratch_shapes=[pltpu.VMEM((nh, MLSTM_DH, MLSTM_DH), F32),
                        pltpu.VMEM((nh, 1, MLSTM_DH), F32),
                        pltpu.VMEM((nh, 8, V7X_LANES), F32)],
        compiler_params=_cparams(("arbitrary", "arbitrary", "arbitrary"), 48),
        name="mlstm",
    )(q, k, v, gc, gr4, bc, br)


def _lru_kernel(reverse, x_ref, wa_ref, wx_ref, ba_ref, bx_ref, lam_ref, h_ref, a_scr, b_scr, carry):
    j = pl.program_id(1)
    t_rows = x_ref.shape[1]
    bw = V7X_MXU_DIM

    @pl.when(j == 0)
    def _():
        carry[...] = jnp.zeros_like(carry)

    x = x_ref[0]
    xb = x.astype(BF16)
    sp = _softplus(-lam_ref[...])
    for jj in range(LRU_W // bw):
        sl = slice(jj * bw, (jj + 1) * bw)
        r = _sigmoid(_dot(xb[:, sl], wa_ref[jj]) + ba_ref[:, sl])
        gi = _sigmoid(_dot(xb[:, sl], wx_ref[jj]) + bx_ref[:, sl])
        a = jnp.exp(-LRU_C * r * sp[:, sl])
        a_scr[:, sl] = a
        b_scr[:, sl] = jnp.sqrt(1.0 - a * a) * gi * x[:, sl]

    def body(t, hc):
        tt = t_rows - 1 - t if reverse else t
        hn = a_scr[pl.ds(tt, 1), :] * hc + b_scr[pl.ds(tt, 1), :]
        b_scr[pl.ds(tt, 1), :] = hn
        return hn

    carry[...] = lax.fori_loop(0, t_rows, body, carry[...], unroll=8)
    h_ref[0] = b_scr[...].astype(h_ref.dtype)


def _lru_blockdiag(w):
    per = V7X_MXU_DIM // LRU_BW
    nt = LRU_BLOCKS // per
    w4 = w.reshape(nt, per, LRU_BW, LRU_BW)
    eye = jnp.eye(per, dtype=w.dtype)
    t = jnp.einsum('tpcd,pq->tpcqd', w4, eye)
    return t.reshape(nt, V7X_MXU_DIM, V7X_MXU_DIM).astype(BF16)


def _lru(xc, wa, wx, ba, bx, lam, n_ctx, reverse):
    bsz, s, w = xc.shape
    tm = ROW_TILE
    nchunks = s // tm
    order = _chunk_order(n_ctx // tm, nchunks)
    d = 1 if reverse else 0
    row = lambda b, j: (b, order(d, j), 0)
    ntile = w // V7X_MXU_DIM
    full3 = pl.BlockSpec((ntile, V7X_MXU_DIM, V7X_MXU_DIM), lambda b, j: (0, 0, 0))
    vec = pl.BlockSpec((1, w), lambda b, j: (0, 0))
    return pl.pallas_call(
        functools.partial(_lru_kernel, reverse),
        grid=(bsz, nchunks),
        in_specs=[pl.BlockSpec((1, tm, w), row), full3, full3, vec, vec, vec],
        out_specs=pl.BlockSpec((1, tm, w), row),
        out_shape=jax.ShapeDtypeStruct((bsz, s, w), BF16),
        scratch_shapes=[pltpu.VMEM((tm, w), F32), pltpu.VMEM((tm, w), F32), pltpu.VMEM((1, w), F32)],
        compiler_params=_cparams(("arbitrary", "arbitrary"), 32),
        name="lru_bwd" if reverse else "lru_fwd",
    )(xc, _lru_blockdiag(wa), _lru_blockdiag(wx), ba.reshape(1, w), bx.reshape(1, w), lam.reshape(1, w))


def _route(logits, first, ridx_ref, rw_ref, cnt_ref):
    tm = logits.shape[0]
    lane = lax.broadcasted_iota(jnp.int32, logits.shape, 1)
    lg = jnp.where(lane < N_EXPERTS, logits, -jnp.inf)
    tops, hots = [], []
    for _ in range(TOP_K):
        m = jnp.max(lg, axis=1, keepdims=True)
        idx = jnp.min(jnp.where(lg == m, lane, ROUTER_PAD), axis=1, keepdims=True)
        hot = lane == idx
        lg = jnp.where(hot, -jnp.inf, lg)
        tops.append((m, idx))
        hots.append(hot)
    es = [jnp.exp(m - tops[0][0]) for m, _ in tops]
    denom = es[0]
    for e in es[1:]:
        denom = denom + e

    @pl.when(first)
    def _():
        cnt_ref[...] = jnp.zeros_like(cnt_ref)

    chosen = hots[0]
    for hot in hots[1:]:
        chosen = jnp.logical_or(chosen, hot)
    chosen_f = jnp.where(chosen, 1.0, 0.0)
    row = lax.broadcasted_iota(jnp.int32, (tm, tm), 0)
    col = lax.broadcasted_iota(jnp.int32, (tm, tm), 1)
    before = jnp.where(col < row, 1.0, 0.0).astype(BF16)
    ranks = _dot(before, chosen_f.astype(BF16)) + cnt_ref[...]
    cnt_ref[...] = cnt_ref[...] + jnp.sum(chosen_f, axis=0, keepdims=True)
    ridx = jnp.zeros(logits.shape, jnp.int32)
    rw = jnp.zeros(logits.shape, F32)
    for kk in range(TOP_K):
        rank = jnp.sum(jnp.where(hots[kk], ranks, 0.0), axis=1, keepdims=True).astype(jnp.int32)
        ridx = jnp.where(lane == kk, tops[kk][1], ridx)
        ridx = jnp.where(lane == TOP_K + kk, rank, ridx)
        rw = jnp.where(lane == kk, es[kk] / denom, rw)
    ridx_ref[0] = ridx
    rw_ref[0] = rw


def _tail(y, x_ref, mod, n2_ref, wr_ref, br_ref, xo_ref, v_ref, ridx_ref, rw_ref, cnt_ref):
    xn = x_ref[0] + mod[2:3] * y
    xo_ref[0] = xn
    v = _modnorm(xn, n2_ref[...], mod[4:5], mod[3:4])
    v_ref[0] = v.astype(v_ref.dtype)
    logits = jnp.dot(v, wr_ref[...], precision=HIGHEST, preferred_element_type=F32) + br_ref[...]
    first = jnp.logical_and(pl.program_id(0) == 0, pl.program_id(1) == 0)
    _route(logits, first, ridx_ref, rw_ref, cnt_ref)


def _even_out_kernel(hm_ref, hl0_ref, hl1_ref, o_ref, yg_ref, mg_ref, wout_ref,
                     x_ref, mod_ref, n2_ref, wr_ref, br_ref, xo_ref, v_ref, ridx_ref, rw_ref, cnt_ref):
    hm = hm_ref[0, 0].astype(F32) + hm_ref[1, 0].astype(F32)
    parts = []
    for h in range(MLSTM_HEADS):
        sl = slice(h * MLSTM_DH, (h + 1) * MLSTM_DH)
        parts.append(_rms(hm[:, sl], mg_ref[:, sl]))
    hmn = jnp.concatenate(parts, axis=1) * _sigmoid(o_ref[0].astype(F32))
    hl = (hl0_ref[0].astype(F32) + hl1_ref[0].astype(F32)) * _gelu_tanh(yg_ref[0].astype(F32))
    y = (_dot(hmn.astype(BF16), wout_ref[0:MLSTM_W, :])
         + _dot(hl.astype(BF16), wout_ref[MLSTM_W:MLSTM_W + LRU_W, :]))
    _tail(y, x_ref, mod_ref[0, 0], n2_ref, wr_ref, br_ref, xo_ref, v_ref, ridx_ref, rw_ref, cnt_ref)


def _even_out(hm, hl0, hl1, o_pre, yg, mnorm_g, w_out, x, mod, n2, w_r, b_r, n_ctx):
    bsz, s, d = x.shape
    tm = ROW_TILE
    nt = s // tm
    nct = n_ctx // tm
    row = lambda b, i: (b, i, 0)
    full = lambda shape: pl.BlockSpec(shape, lambda b, i: (0,) * len(shape))
    return pl.pallas_call(
        _even_out_kernel,
        grid=(bsz, nt),
        in_specs=[pl.BlockSpec((2, 1, tm, MLSTM_W), lambda b, i: (0, b, i, 0)),
                  pl.BlockSpec((1, tm, LRU_W), row), pl.BlockSpec((1, tm, LRU_W), row),
                  pl.BlockSpec((1, tm, MLSTM_W), row), pl.BlockSpec((1, tm, LRU_W), row),
                  full((1, MLSTM_W)), full((MLSTM_W + LRU_W, d)),
                  pl.BlockSpec((1, tm, d), row),
                  pl.BlockSpec((1, 1, 8, d), lambda b, i: (b, (i >= nct).astype(jnp.int32), 0, 0)),
                  full((1, d)), full((d, ROUTER_PAD)), full((1, ROUTER_PAD))],
        out_specs=[pl.BlockSpec((1, tm, d), row), pl.BlockSpec((1, tm, d), row),
                   pl.BlockSpec((1, tm, ROUTER_PAD), row), pl.BlockSpec((1, tm, ROUTER_PAD), row),
                   full((1, ROUTER_PAD))],
        out_shape=[jax.ShapeDtypeStruct((bsz, s, d), F32),
                   jax.ShapeDtypeStruct((bsz, s, d), BF16),
                   jax.ShapeDtypeStruct((bsz, s, ROUTER_PAD), jnp.int32),
                   jax.ShapeDtypeStruct((bsz, s, ROUTER_PAD), F32),
                   jax.ShapeDtypeStruct((1, ROUTER_PAD), F32)],
        compiler_params=_cparams(("arbitrary", "arbitrary"), 48),
        name="even_out",
    )(hm, hl0, hl1, o_pre, yg, mnorm_g.reshape(1, MLSTM_W), w_out, x, mod, n2.reshape(1, d), w_r, b_r)


def _odd_out_kernel(a_ref, wout_ref, x_ref, mod_ref, n2_ref, wr_ref, br_ref, xo_ref, v_ref, ridx_ref, rw_ref, cnt_ref):
    y = _dot(a_ref[0], wout_ref[...])
    _tail(y, x_ref, mod_ref[0, 0], n2_ref, wr_ref, br_ref, xo_ref, v_ref, ridx_ref, rw_ref, cnt_ref)


def _odd_out(attn, w_out, x, mod, n2, w_r, b_r, n_ctx):
    bsz, seq, d = attn.shape
    tm = ROW_TILE
    nt = seq // tm
    nct = n_ctx // tm
    row = lambda b, i: (b, i, 0)
    full = lambda shape: pl.BlockSpec(shape, lambda b, i: (0,) * len(shape))
    return pl.pallas_call(
        _odd_out_kernel,
        grid=(bsz, nt),
        in_specs=[pl.BlockSpec((1, tm, d), row), full((d, d)),
                  pl.BlockSpec((1, tm, d), lambda b, i: (b, i + nct, 0)),
                  pl.BlockSpec((1, 1, 8, d), lambda b, i: (b, 1, 0, 0)),
                  full((1, d)), full((d, ROUTER_PAD)), full((1, ROUTER_PAD))],
        out_specs=[pl.BlockSpec((1, tm, d), row), pl.BlockSpec((1, tm, d), row),
                   pl.BlockSpec((1, tm, ROUTER_PAD), row), pl.BlockSpec((1, tm, ROUTER_PAD), row),
                   full((1, ROUTER_PAD))],
        out_shape=[jax.ShapeDtypeStruct((bsz, seq, d), F32),
                   jax.ShapeDtypeStruct((bsz, seq, d), BF16),
                   jax.ShapeDtypeStruct((bsz, seq, ROUTER_PAD), jnp.int32),
                   jax.ShapeDtypeStruct((bsz, seq, ROUTER_PAD), F32),
                   jax.ShapeDtypeStruct((1, ROUTER_PAD), F32)],
        compiler_params=_cparams(("arbitrary", "arbitrary"), 32),
        name="odd_out",
    )(attn, w_out, x, mod, n2.reshape(1, d), w_r, b_r)


def _expert_kernel(te_ref, nu_ref, x_ref, w1_ref, b1_ref, w2_ref, b2_ref, rw_ref, o_ref, w1b, w2b):
    t = pl.program_id(0)
    used = t < nu_ref[0]
    new_expert = jnp.logical_or(t == 0, te_ref[t] != te_ref[jnp.maximum(t - 1, 0)])

    @pl.when(jnp.logical_and(used, new_expert))
    def _():
        w1b[...] = w1_ref[0].astype(BF16)
        w2b[...] = w2_ref[0].astype(BF16)

    @pl.when(used)
    def _():
        hid = _dot(x_ref[...], w1b[...]) + b1_ref[0]
        gate = jnp.minimum(hid[:, :D_FF], SWIGLU_LIMIT)
        up = jnp.clip(hid[:, D_FF:], -SWIGLU_LIMIT, SWIGLU_LIMIT)
        act = (up + 1.0) * gate * _sigmoid(SWIGLU_ALPHA * gate)
        y = _dot(act.astype(BF16), w2b[...]) + b2_ref[0]
        o_ref[...] = (y * rw_ref[...]).astype(o_ref.dtype)

    @pl.when(jnp.logical_not(used))
    def _():
        o_ref[...] = jnp.zeros_like(o_ref)


def _experts(xs, row_w, tile_expert, n_used, w1, b1, w2, b2):
    rows, d = xs.shape
    tm = MOE_TILE
    nt = rows // tm
    ne, _, ff2 = w1.shape
    return pl.pallas_call(
        _expert_kernel,
        grid_spec=pltpu.PrefetchScalarGridSpec(
            num_scalar_prefetch=2,
            grid=(nt,),
            in_specs=[pl.BlockSpec((tm, d), lambda t, te, nu: (t, 0)),
                      pl.BlockSpec((1, d, ff2), lambda t, te, nu: (te[t], 0, 0)),
                      pl.BlockSpec((1, 1, ff2), lambda t, te, nu: (te[t], 0, 0)),
                      pl.BlockSpec((1, ff2 // 2, d), lambda t, te, nu: (te[t], 0, 0)),
                      pl.BlockSpec((1, 1, d), lambda t, te, nu: (te[t], 0, 0)),
                      pl.BlockSpec((tm, 1), lambda t, te, nu: (t, 0))],
            out_specs=pl.BlockSpec((tm, d), lambda t, te, nu: (t, 0)),
            scratch_shapes=[pltpu.VMEM((d, ff2), BF16), pltpu.VMEM((ff2 // 2, d), BF16)],
        ),
        out_shape=jax.ShapeDtypeStruct((rows, d), BF16),
        compiler_params=_cparams(("arbitrary",), 56),
        name="moe_experts",
    )(tile_expert, n_used, xs, w1, b1.reshape(ne, 1, ff2), w2, b2.reshape(ne, 1, d), row_w)


def _moe(v, ridx, rw, counts, w1, b1, w2, b2):
    t, d = v.shape
    tm = MOE_TILE
    nrows = t * TOP_K
    nt = -(-nrows // tm) + N_EXPERTS
    idx = ridx[:, :TOP_K]
    rank = ridx[:, TOP_K:2 * TOP_K]
    weight = rw[:, :TOP_K]
    sizes = counts[0, :N_EXPERTS].astype(jnp.int32)
    start = jnp.cumsum(sizes) - sizes
    padded = (sizes + tm - 1) // tm * tm
    pad_end = jnp.cumsum(padded)
    pad_start = pad_end - padded
    pair_pos = pad_start[idx] + rank
    bits = max(1, (nrows - 1).bit_length())
    assert N_EXPERTS << bits < 2 ** 31
    key = (idx.reshape(-1) << bits) + jnp.arange(nrows, dtype=jnp.int32)
    order = jnp.sort(key) & ((1 << bits) - 1)
    tile_expert = jnp.minimum(
        jnp.searchsorted(pad_end, jnp.arange(nt, dtype=jnp.int32) * tm, side='right'),
        N_EXPERTS - 1).astype(jnp.int32)
    n_used = (pad_end[-1] // tm).reshape(1).astype(jnp.int32)
    r_in = (jnp.arange(nt * tm, dtype=jnp.int32).reshape(nt, tm) - pad_start[tile_expert][:, None])
    valid = r_in < sizes[tile_expert][:, None]
    src = jnp.where(valid, start[tile_expert][:, None] + r_in, 0).reshape(-1)
    pair = jnp.take(order, src)
    row_token = pair // TOP_K
    row_w = jnp.where(valid.reshape(-1), jnp.take(weight.reshape(-1), pair), 0.0)
    xs = jnp.take(v, row_token, axis=0)
    ys = _experts(xs, row_w.reshape(-1, 1), tile_expert, n_used, w1, b1, w2, b2)
    return [jnp.take(ys, pair_pos[:, kk], axis=0) for kk in range(TOP_K)]


def _rope(t, cos, sin, lane_lo):
    swapped = jnp.where(lane_lo, pltpu.roll(t, ATT_DH - ROPE_AXIS_DIM // 2, 1),
                        pltpu.roll(t, ROPE_AXIS_DIM // 2, 1))
    return t * cos + swapped * sin


def _proj_odd_kernel(x_ref, f0, f1, f2, f3, mod0_ref, mod_ref, g_ref, w_ref, qg_ref, kg_ref,
                     cos_ref, sin_ref, h_ref, q_ref, k_ref, v_ref):
    f = (f0[0].astype(F32) + f1[0].astype(F32)) + (f2[0].astype(F32) + f3[0].astype(F32))
    hcur = x_ref[0] + mod0_ref[0, 0][5:6] * f
    h_ref[0] = hcur
    mod = mod_ref[0, 0]
    u = _modnorm(hcur, g_ref[...], mod[1:2], mod[0:1]).astype(BF16)
    z = _dot(u, w_ref[...])
    cos = cos_ref[...]
    sin = sin_ref[...]
    lane = lax.broadcasted_iota(jnp.int32, cos.shape, 1)
    lane_lo = (lane % ROPE_AXIS_DIM) < (ROPE_AXIS_DIM // 2)
    qw = ATT_HEADS * ATT_DH
    kw = ATT_KV_HEADS * ATT_DH
    for hh in range(ATT_HEADS):
        sl = slice(hh * ATT_DH, (hh + 1) * ATT_DH)
        t = _rope(_rms(z[:, sl], qg_ref[...]), cos, sin, lane_lo)
        q_ref[0, :, sl] = (t * (ATT_DH ** -0.5)).astype(q_ref.dtype)
    for hh in range(ATT_KV_HEADS):
        sl = slice(hh * ATT_DH, (hh + 1) * ATT_DH)
        t = _rope(_rms(z[:, qw + hh * ATT_DH:qw + (hh + 1) * ATT_DH], kg_ref[...]), cos, sin, lane_lo)
        k_ref[0, :, sl] = t.astype(k_ref.dtype)
    v_ref[0] = z[:, qw + kw:qw + 2 * kw].astype(v_ref.dtype)


def _proj_odd(x, fparts, mod0, mod, g, w, qg, kg, cos_tab, sin_tab, n_ctx):
    bsz, s, d = x.shape
    tm = ROW_TILE
    nt = s // tm
    nct = n_ctx // tm
    n = w.shape[1]
    qw = ATT_HEADS * ATT_DH
    kw = ATT_KV_HEADS * ATT_DH
    row = lambda b, i: (b, i, 0)
    seg = lambda b, i: (b, (i >= nct).astype(jnp.int32), 0, 0)
    full = lambda shape: pl.BlockSpec(shape, lambda b, i: (0,) * len(shape))
    tab = pl.BlockSpec((tm, ATT_DH), lambda b, i: (i, 0))
    return pl.pallas_call(
        _proj_odd_kernel,
        grid=(bsz, nt),
        in_specs=[pl.BlockSpec((1, tm, d), row)] + [pl.BlockSpec((1, tm, d), row)] * TOP_K
                 + [pl.BlockSpec((1, 1, 8, d), seg), pl.BlockSpec((1, 1, 8, d), seg),
                    full((1, d)), full((d, n)), full((1, ATT_DH)), full((1, ATT_DH)), tab, tab],
        out_specs=[pl.BlockSpec((1, tm, d), row), pl.BlockSpec((1, tm, qw), row),
                   pl.BlockSpec((1, tm, kw), row), pl.BlockSpec((1, tm, kw), row)],
        out_shape=[jax.ShapeDtypeStruct((bsz, s, d), F32),
                   jax.ShapeDtypeStruct((bsz, s, qw), BF16),
                   jax.ShapeDtypeStruct((bsz, s, kw), BF16),
                   jax.ShapeDtypeStruct((bsz, s, kw), BF16)],
        compiler_params=_cparams(("arbitrary", "arbitrary"), 48),
        name="proj_odd",
    )(x, *fparts, mod0, mod, g.reshape(1, d), w, qg.reshape(1, ATT_DH), kg.reshape(1, ATT_DH),
      cos_tab, sin_tab)


def _rope_tables(n_ctx, seq):
    rows = seq // GRID_W
    pos_r = jnp.repeat(jnp.arange(rows), GRID_W).astype(F32)
    pos_c = jnp.tile(jnp.arange(GRID_W), rows).astype(F32)
    inv_freq = ROPE_THETA ** (-jnp.arange(0, ROPE_AXIS_DIM, 2, dtype=F32) / ROPE_AXIS_DIM)
    ar = pos_r[:, None] * inv_freq
    ac = pos_c[:, None] * inv_freq
    cos = jnp.concatenate([jnp.cos(ar), jnp.cos(ar), jnp.cos(ac), jnp.cos(ac)], axis=-1)
    sin = jnp.concatenate([-jnp.sin(ar), jnp.sin(ar), -jnp.sin(ac), jnp.sin(ac)], axis=-1)
    cos = jnp.concatenate([jnp.ones((n_ctx, ATT_DH), F32), cos], axis=0)
    sin = jnp.concatenate([jnp.zeros((n_ctx, ATT_DH), F32), sin], axis=0)
    return cos, sin


def _attn_kernel(q_ref, k_ref, v_ref, o_ref):
    k = k_ref[0]
    v = v_ref[0]
    for g in range(ATT_GROUP):
        sl = slice(g * ATT_DH, (g + 1) * ATT_DH)
        s = _dot_nt(q_ref[0, :, sl], k)
        p = jnp.exp(s - jnp.max(s, axis=1, keepdims=True))
        l = jnp.sum(p, axis=1, keepdims=True)
        o_ref[0, :, sl] = (_dot(p.astype(BF16), v) / l).astype(o_ref.dtype)


def _attention(q, k, v, n_ctx):
    bsz, s, qw = q.shape
    seq = s - n_ctx
    tq = ROW_TILE
    nct = n_ctx // tq
    gw = ATT_GROUP * ATT_DH
    return pl.pallas_call(
        _attn_kernel,
        grid=(bsz, ATT_KV_HEADS, seq // tq),
        in_specs=[pl.BlockSpec((1, tq, gw), lambda b, h, i: (b, i + nct, h)),
                  pl.BlockSpec((1, s, ATT_DH), lambda b, h, i: (b, 0, h)),
                  pl.BlockSpec((1, s, ATT_DH), lambda b, h, i: (b, 0, h))],
        out_specs=pl.BlockSpec((1, tq, gw), lambda b, h, i: (b, i, h)),
        out_shape=jax.ShapeDtypeStruct((bsz, seq, qw), BF16),
        compiler_params=_cparams(("arbitrary", "arbitrary", "arbitrary"), 48),
        name="attention",
    )(q, k, v)


def _final_kernel(x_ref, f0, f1, f2, f3, mod_ref, g_ref, o_ref):
    f = (f0[0].astype(F32) + f1[0].astype(F32)) + (f2[0].astype(F32) + f3[0].astype(F32))
    o_ref[0] = _rms(x_ref[0] + mod_ref[0, 0][5:6] * f, g_ref[...])


def _final(x, fparts, mod, g):
    bsz, seq, d = x.shape
    tm = ROW_TILE
    row = lambda b, i: (b, i, 0)
    return pl.pallas_call(
        _final_kernel,
        grid=(bsz, seq // tm),
        in_specs=[pl.BlockSpec((1, tm, d), row)] * (1 + TOP_K)
                 + [pl.BlockSpec((1, 1, 8, d), lambda b, i: (b, 1, 0, 0)),
                    pl.BlockSpec((1, d), lambda b, i: (0, 0))],
        out_specs=pl.BlockSpec((1, tm, d), row),
        out_shape=jax.ShapeDtypeStruct((bsz, seq, d), F32),
        compiler_params=_cparams(("arbitrary", "arbitrary"), 32),
        name="final_norm",
    )(x, *fparts, mod, g.reshape(1, d))


def _pack_even_w_in(w_in):
    w4 = 4 * MLSTM_W
    ng = 4 * MLSTM_HEADS
    wg = w_in[:, w4:w4 + ng]
    half = ng // 2
    pad = jnp.zeros((w_in.shape[0], GATE_PAD - half), w_in.dtype)
    packed = jnp.concatenate([w_in[:, :w4], w_in[:, w4 + ng:], wg[:, :half], pad, wg[:, half:], pad], axis=1)
    return packed.astype(BF16), wg.T.astype(BF16)


def _pad_router(w_r, b_r):
    d, ne = w_r.shape
    return (jnp.pad(w_r, ((0, 0), (0, ROUTER_PAD - ne))),
            jnp.pad(b_r, (0, ROUTER_PAD - ne)).reshape(1, ROUTER_PAD))


def kernel(x, c, ctx, c_ctx, mod_w, mod_b, norm1_g, norm2_g, final_g, ev_w_in, ev_qk_conv_w, ev_qk_conv_b, ev_gate_b, ev_mnorm_g, ev_lru_conv_w, ev_lru_conv_b, ev_lru_wa, ev_lru_ba, ev_lru_wx, ev_lru_bx, ev_lru_lam, ev_w_out, od_w_in, od_q_norm_g, od_k_norm_g, od_w_out, moe_w_r, moe_b_r, moe_w1, moe_b1, moe_w2, moe_b2):
    bsz, seq, d = x.shape
    n_ctx = ctx.shape[1]
    s = n_ctx + seq
    assert n_ctx % ROW_TILE == 0 and seq % ROW_TILE == 0 and seq % GRID_W == 0
    h = jnp.concatenate([ctx, x], axis=1)

    mod0 = _mod_table(c, c_ctx, mod_w[0], mod_b[0])
    w_packed, wg_t = _pack_even_w_in(ev_w_in[0])
    qk_pre, v, o_pre, xr, yg, gc, gr = _proj_even(h, mod0, norm1_g[0], w_packed, wg_t, n_ctx)
    q, k, xc = _conv(qk_pre, xr, ev_qk_conv_w[0], ev_qk_conv_b[0], ev_lru_conv_w[0], ev_lru_conv_b[0], n_ctx)
    hm = _mlstm(q, k, v, gc, gr, ev_gate_b[0], n_ctx)
    hl = [_lru(xc, ev_lru_wa[0, dd], ev_lru_wx[0, dd], ev_lru_ba[0, dd], ev_lru_bx[0, dd],
               ev_lru_lam[0, dd], n_ctx, dd == 1) for dd in range(2)]
    w_r0, b_r0 = _pad_router(moe_w_r[0], moe_b_r[0])
    x_mid, v0, ri0, rw0, cnt0 = _even_out(hm, hl[0], hl[1], o_pre, yg, ev_mnorm_g[0], ev_w_out[0].astype(BF16),
                                          h, mod0, norm2_g[0], w_r0, b_r0, n_ctx)
    f0 = _moe(v0.reshape(bsz * s, d), ri0.reshape(bsz * s, ROUTER_PAD), rw0.reshape(bsz * s, ROUTER_PAD), cnt0,
              moe_w1[0], moe_b1[0], moe_w2[0], moe_b2[0])
    f0 = [p.reshape(bsz, s, d) for p in f0]

    mod1 = _mod_table(c, c_ctx, mod_w[1], mod_b[1])
    cos_tab, sin_tab = _rope_tables(n_ctx, seq)
    h1, q1, k1, v1 = _proj_odd(x_mid, f0, mod0, mod1, norm1_g[1], od_w_in[0].astype(BF16),
                               od_q_norm_g[0], od_k_norm_g[0], cos_tab, sin_tab, n_ctx)
    attn = _attention(q1, k1, v1, n_ctx)
    w_r1, b_r1 = _pad_router(moe_w_r[1], moe_b_r[1])
    x2, v2, ri2, rw2, cnt2 = _odd_out(attn, od_w_out[0].astype(BF16), h1, mod1, norm2_g[1], w_r1, b_r1, n_ctx)
    f1 = _moe(v2.reshape(bsz * seq, d), ri2.reshape(bsz * seq, ROUTER_PAD), rw2.reshape(bsz * seq, ROUTER_PAD), cnt2,
              moe_w1[1], moe_b1[1], moe_w2[1], moe_b2[1])
    f1 = [p.reshape(bsz, seq, d) for p in f1]
    return _final(x2, f1, mod1, final_g)
```

```python
import functools

import jax
import jax.numpy as jnp
from jax import lax
from jax.experimental import pallas as pl
from jax.experimental.pallas import tpu as pltpu

F32 = jnp.float32
BF16 = jnp.bfloat16
HIGHEST = lax.Precision.HIGHEST

EPS = 1e-6
M_INIT = -1e30
NEG_BIG = -1e30

MLSTM_HEADS = 4
MLSTM_DH = 256
MLSTM_W = MLSTM_HEADS * MLSTM_DH
LRU_W = 1024
LRU_BLOCKS = 16
LRU_BW = LRU_W // LRU_BLOCKS
LRU_C = 8.0
CONV_W = 4
CONV_LEFT = 2
ATT_HEADS = 8
ATT_KV_HEADS = 2
ATT_GROUP = ATT_HEADS // ATT_KV_HEADS
ATT_DH = 128
GRID_W = 64
ROPE_AXIS_DIM = ATT_DH // 2
ROPE_THETA = 10000.0
N_EXPERTS = 32
TOP_K = 4
D_FF = 1024
SWIGLU_ALPHA = 1.702
SWIGLU_LIMIT = 7.0
LOG2_E = 1.4426950408889634

V7X_LANES = 128
V7X_MXU_DIM = 256
V7X_VMEM_BYTES = 64 * 1024 * 1024
MIB = 1024 * 1024

ROW_TILE = 256
HALO = 16
MOE_TILE = 256
ROUTE_ROWS = 16
ATT_Q_TILE = 256
GATE_PAD = V7X_LANES


def _cparams(semantics, vmem_mib):
    assert vmem_mib * MIB < V7X_VMEM_BYTES
    return pltpu.CompilerParams(dimension_semantics=semantics, vmem_limit_bytes=vmem_mib * MIB)


def _dot(a, b):
    return jnp.dot(a, b, preferred_element_type=F32)


def _dot_nt(a, b, precision=None):
    return lax.dot_general(a, b, (((1,), (1,)), ((), ())), precision=precision,
                           preferred_element_type=F32)


def _dot_tn(a, b):
    return lax.dot_general(a, b, (((0,), (0,)), ((), ())), preferred_element_type=F32)


def _sigmoid(x):
    return jax.nn.sigmoid(x)


def _log_sigmoid(x):
    return jnp.minimum(x, 0.0) - jnp.log1p(jnp.exp(-jnp.abs(x)))


def _softplus(x):
    return jnp.maximum(x, 0.0) + jnp.log1p(jnp.exp(-jnp.abs(x)))


def _gelu_tanh(x):
    return 0.5 * x * (1.0 + jnp.tanh(0.7978845608028654 * (x + 0.044715 * x * x * x)))


def _rms(x, g):
    return x * lax.rsqrt(jnp.mean(x * x, axis=-1, keepdims=True) + EPS) * g


def _modnorm(x, g, scale, shift):
    return _rms(x, g) * (1.0 + scale) + shift


def _modvec_kernel(c_ref, w_ref, b_ref, o_ref):
    c = c_ref[...]
    s = c * _sigmoid(c)
    o_ref[...] = jnp.dot(s, w_ref[...], precision=HIGHEST, preferred_element_type=F32) + b_ref[...]


def _modvec(cc, w, b):
    rows, d = cc.shape
    n = w.shape[1]
    tn = 1536
    return pl.pallas_call(
        _modvec_kernel,
        grid=(n // tn,),
        in_specs=[pl.BlockSpec((rows, d), lambda j: (0, 0)),
                  pl.BlockSpec((d, tn), lambda j: (0, j)),
                  pl.BlockSpec((1, tn), lambda j: (0, j))],
        out_specs=pl.BlockSpec((rows, tn), lambda j: (0, j)),
        out_shape=jax.ShapeDtypeStruct((rows, n), F32),
        compiler_params=_cparams(("arbitrary",), 32),
        name="modvec",
    )(cc, w, b.reshape(1, n))


def _mod_table(c, c_ctx, mod_w, mod_b):
    bsz, d = c.shape
    rows = ((bsz + 1 + 7) // 8) * 8
    cc = jnp.zeros((rows, d), F32).at[:bsz].set(c).at[bsz].set(c_ctx)
    mod = _modvec(cc, mod_w, mod_b)
    lat = mod[:bsz].reshape(bsz, 6, d)
    ctx = jnp.broadcast_to(mod[bsz].reshape(1, 6, d), (bsz, 6, d))
    tbl = jnp.stack([ctx, lat], axis=1)
    return jnp.pad(tbl, ((0, 0), (0, 0), (0, 2), (0, 0)))


def _proj_even_kernel(x_ref, mod_ref, g_ref, w_ref, wgt_ref,
                      qk_ref, v_ref, o_ref, xr_ref, yg_ref, gc_ref, gr_ref):
    mod = mod_ref[0, 0]
    u = _modnorm(x_ref[0], g_ref[...], mod[1:2], mod[0:1]).astype(BF16)
    w = MLSTM_W
    qk_ref[0] = _dot(u, w_ref[:, 0:2 * w]).astype(qk_ref.dtype)
    v_ref[0] = _dot(u, w_ref[:, 2 * w:3 * w]).astype(v_ref.dtype)
    o_ref[0] = _dot(u, w_ref[:, 3 * w:4 * w]).astype(o_ref.dtype)
    xr_ref[0] = _dot(u, w_ref[:, 4 * w:4 * w + LRU_W]).astype(xr_ref.dtype)
    yg_ref[0] = _dot(u, w_ref[:, 4 * w + LRU_W:4 * w + 2 * LRU_W]).astype(yg_ref.dtype)
    gc_ref[0] = _dot(u, w_ref[:, 4 * w + 2 * LRU_W:4 * w + 2 * LRU_W + 2 * GATE_PAD])
    gr_ref[0] = _dot_nt(wgt_ref[...], u)


def _proj_even(h, mod, g, w_packed, wg_t, n_ctx):
    bsz, s, d = h.shape
    tm = ROW_TILE
    nt = s // tm
    nct = n_ctx // tm
    ntot = w_packed.shape[1]
    ng = wg_t.shape[0]
    row = lambda b, i: (b, i, 0)
    return pl.pallas_call(
        _proj_even_kernel,
        grid=(bsz, nt),
        in_specs=[pl.BlockSpec((1, tm, d), row),
                  pl.BlockSpec((1, 1, 8, d), lambda b, i: (b, (i >= nct).astype(jnp.int32), 0, 0)),
                  pl.BlockSpec((1, d), lambda b, i: (0, 0)),
                  pl.BlockSpec((d, ntot), lambda b, i: (0, 0)),
                  pl.BlockSpec((ng, d), lambda b, i: (0, 0))],
        out_specs=[pl.BlockSpec((1, tm, 2 * MLSTM_W), row),
                   pl.BlockSpec((1, tm, MLSTM_W), row),
                   pl.BlockSpec((1, tm, MLSTM_W), row),
                   pl.BlockSpec((1, tm, LRU_W), row),
                   pl.BlockSpec((1, tm, LRU_W), row),
                   pl.BlockSpec((1, tm, 2 * GATE_PAD), row),
                   pl.BlockSpec((1, ng, tm), lambda b, i: (b, 0, i))],
        out_shape=[jax.ShapeDtypeStruct((bsz, s, 2 * MLSTM_W), BF16),
                   jax.ShapeDtypeStruct((bsz, s, MLSTM_W), BF16),
                   jax.ShapeDtypeStruct((bsz, s, MLSTM_W), BF16),
                   jax.ShapeDtypeStruct((bsz, s, LRU_W), F32),
                   jax.ShapeDtypeStruct((bsz, s, LRU_W), BF16),
                   jax.ShapeDtypeStruct((bsz, s, 2 * GATE_PAD), F32),
                   jax.ShapeDtypeStruct((bsz, ng, s), F32)],
        compiler_params=_cparams(("arbitrary", "arbitrary"), 48),
        name="proj_even",
    )(h, mod, g.reshape(1, d), w_packed, wg_t)


def _conv_kernel(nct, qk_m, qk_p, qk_n, xr_m, xr_p, xr_n, wqk_ref, bqk_ref, wxr_ref, bxr_ref,
                 q_ref, k_ref, xc_ref, ext_qk, ext_xr):
    i = pl.program_id(1)
    nt = pl.num_programs(1)
    tm = qk_m.shape[1]
    first = jnp.logical_or(i == 0, i == nct)
    last = jnp.logical_or(i == nct - 1, i == nt - 1)
    pm = jnp.where(first, 0.0, 1.0)
    nm = jnp.where(last, 0.0, 1.0)

    def conv(main, prev, nxt, ext, w_ref, b_ref):
        ext[0:HALO] = prev[0].astype(F32) * pm
        ext[HALO:HALO + tm] = main[0].astype(F32)
        ext[HALO + tm:2 * HALO + tm] = nxt[0].astype(F32) * nm
        acc = b_ref[...] + w_ref[0:1, :] * ext[pl.ds(HALO - CONV_LEFT, tm), :]
        for j in range(1, CONV_W):
            acc = acc + w_ref[j:j + 1, :] * ext[pl.ds(HALO - CONV_LEFT + j, tm), :]
        return acc

    y = conv(qk_m, qk_p, qk_n, ext_qk, wqk_ref, bqk_ref)
    y = y * _sigmoid(y)
    q_ref[0] = y[:, :MLSTM_W].astype(q_ref.dtype)
    k_ref[0] = (y[:, MLSTM_W:] * (MLSTM_DH ** -0.5)).astype(k_ref.dtype)
    xc_ref[0] = conv(xr_m, xr_p, xr_n, ext_xr, wxr_ref, bxr_ref)


def _conv(qk_pre, xr, wqk, bqk, wxr, bxr, n_ctx):
    bsz, s, _ = qk_pre.shape
    tm = ROW_TILE
    nt = s // tm
    nct = n_ctx // tm
    hb = tm // HALO
    nhb = s // HALO
    row = lambda b, i: (b, i, 0)
    prev = lambda b, i: (b, jnp.maximum(i * hb - 1, 0), 0)
    nxt = lambda b, i: (b, jnp.minimum((i + 1) * hb, nhb - 1), 0)
    cq = 2 * MLSTM_W
    full = lambda shape: pl.BlockSpec(shape, lambda b, i: (0, 0))
    return pl.pallas_call(
        functools.partial(_conv_kernel, nct),
        grid=(bsz, nt),
        in_specs=[pl.BlockSpec((1, tm, cq), row), pl.BlockSpec((1, HALO, cq), prev),
                  pl.BlockSpec((1, HALO, cq), nxt),
                  pl.BlockSpec((1, tm, LRU_W), row), pl.BlockSpec((1, HALO, LRU_W), prev),
                  pl.BlockSpec((1, HALO, LRU_W), nxt),
                  full((CONV_W, cq)), full((1, cq)), full((CONV_W, LRU_W)), full((1, LRU_W))],
        out_specs=[pl.BlockSpec((1, tm, MLSTM_W), row), pl.BlockSpec((1, tm, MLSTM_W), row),
                   pl.BlockSpec((1, tm, LRU_W), row)],
        out_shape=[jax.ShapeDtypeStruct((bsz, s, MLSTM_W), BF16),
                   jax.ShapeDtypeStruct((bsz, s, MLSTM_W), BF16),
                   jax.ShapeDtypeStruct((bsz, s, LRU_W), F32)],
        scratch_shapes=[pltpu.VMEM((tm + 2 * HALO, cq), F32),
                        pltpu.VMEM((tm + 2 * HALO, LRU_W), F32)],
        compiler_params=_cparams(("arbitrary", "arbitrary"), 48),
        name="dwconv",
    )(qk_pre, qk_pre, qk_pre, xr, xr, xr, wqk, bqk.reshape(1, cq), wxr, bxr.reshape(1, LRU_W))


def _mlstm_kernel(q_ref, k_ref, v_ref, gc_ref, gr_ref, bc_ref, br_ref, h_ref, c_scr, n_scr, m_scr):
    d = pl.program_id(1)
    j = pl.program_id(2)
    lc = q_ref.shape[1]
    nh = MLSTM_HEADS
    dh = MLSTM_DH

    @pl.when(j == 0)
    def _():
        c_scr[...] = jnp.zeros_like(c_scr)
        n_scr[...] = jnp.zeros_like(n_scr)
        m_scr[...] = jnp.full(m_scr.shape, M_INIT, F32)

    row = lax.broadcasted_iota(jnp.int32, (lc, lc), 0)
    col = lax.broadcasted_iota(jnp.int32, (lc, lc), 1)
    lo = jnp.where(d == 1, row, col)
    hi = jnp.where(d == 1, col, row)
    tri = lo <= hi
    trif = tri.astype(F32)

    gc = gc_ref[0] + bc_ref[0]
    gr = gr_ref[0, 0] + br_ref[0]
    lfr = _log_sigmoid(gr)
    bcum_c = jnp.dot(trif, _log_sigmoid(gc), precision=HIGHEST, preferred_element_type=F32)
    bcum_r = _dot_nt(lfr, trif, precision=HIGHEST)

    for h in range(nh):
        sl = slice(h * dh, (h + 1) * dh)
        q = q_ref[0, :, sl]
        k = k_ref[0, :, sl]
        v = v_ref[0, :, sl]
        i_col = gc[:, h:h + 1]
        b_col = bcum_c[:, nh + h:nh + h + 1]
        i_row = gr[h:h + 1, :]
        b_row = bcum_r[nh + h:nh + h + 1, :]
        m_prev = m_scr[h][0:1, 0:1]
        c_mat = c_scr[h]
        n_vec = n_scr[h]

        log_intra = jnp.where(tri, b_col - b_row + i_row, NEG_BIG)
        log_inter = b_col + m_prev
        m_t = jnp.maximum(log_inter, jnp.max(log_intra, axis=1, keepdims=True))
        w_inter = jnp.exp(log_inter - m_t)
        scores = _dot_nt(q, k) * jnp.exp(log_intra - m_t)
        num = w_inter * _dot(q, c_mat.astype(BF16)) + _dot(scores.astype(BF16), v)
        den = (w_inter * jnp.sum(q.astype(F32) * n_vec, axis=1, keepdims=True)
               + jnp.sum(scores, axis=1, keepdims=True))
        hh = num / jnp.maximum(jnp.abs(den), jnp.exp(-m_t))
        h_ref[0, 0, :, sl] = hh.astype(h_ref.dtype)

        total_f = jnp.sum(lfr[nh + h:nh + h + 1, :], axis=1, keepdims=True)
        log_w_row = total_f - b_row + i_row
        m_new = jnp.maximum(total_f + m_prev, jnp.max(log_w_row, axis=1, keepdims=True))
        decay = jnp.exp(total_f + m_prev - m_new)
        w_col = jnp.exp(total_f - b_col + i_col - m_new)
        wv = (w_col * v.astype(F32)).astype(BF16)
        c_scr[h] = decay * c_mat + _dot_tn(k, wv)
        n_scr[h] = decay * n_vec + jnp.sum(w_col * k.astype(F32), axis=0, keepdims=True)
        m_scr[h] = jnp.broadcast_to(m_new, m_scr.shape[1:])


def _chunk_order(n_ctx_chunks, nchunks):
    def order(d, j):
        bwd = jnp.where(j < n_ctx_chunks, n_ctx_chunks - 1 - j, nchunks - 1 - (j - n_ctx_chunks))
        return jnp.where(d == 0, j, bwd)
    return order


def _mlstm(q, k, v, gc, gr, gate_b, n_ctx):
    bsz, s, w = q.shape
    lc = ROW_TILE
    nchunks = s // lc
    order = _chunk_order(n_ctx // lc, nchunks)
    nh = MLSTM_HEADS
    gb = gate_b.reshape(2, 2 * nh)
    bc = jnp.pad(gb, ((0, 0), (0, GATE_PAD - 2 * nh))).reshape(2, 1, GATE_PAD)
    br = gb.reshape(2, 2 * nh, 1)
    gr4 = gr.reshape(bsz, 2, 2 * nh, s)
    row = lambda b, d, j: (b, order(d, j), 0)
    return pl.pallas_call(
        _mlstm_kernel,
        grid=(bsz, 2, nchunks),
        in_specs=[pl.BlockSpec((1, lc, w), row), pl.BlockSpec((1, lc, w), row),
                  pl.BlockSpec((1, lc, w), row),
                  pl.BlockSpec((1, lc, GATE_PAD), lambda b, d, j: (b, order(d, j), d)),
                  pl.BlockSpec((1, 1, 2 * nh, lc), lambda b, d, j: (b, d, 0, order(d, j))),
                  pl.BlockSpec((1, 1, GATE_PAD), lambda b, d, j: (d, 0, 0)),
                  pl.BlockSpec((1, 2 * nh, 1), lambda b, d, j: (d, 0, 0))],
        out_specs=pl.BlockSpec((1, 1, lc, w), lambda b, d, j: (d, b, order(d, j), 0)),
        out_shape=jax.ShapeDtypeStruct((2, bsz, s, w), BF16),
        scratch_shapes=[pltpu.VMEM((nh, MLSTM_DH, MLSTM_DH), F32),
                        pltpu.VMEM((nh, 1, MLSTM_DH), F32),
                        pltpu.VMEM((nh, 8, V7X_LANES), F32)],
        compiler_params=_cparams(("arbitrary", "arbitrary", "arbitrary"), 48),
        name="mlstm",
    )(q, k, v, gc, gr4, bc, br)


def _lru_kernel(reverse, x_ref, wa_ref, wx_ref, ba_ref, bx_ref, lam_ref, h_ref, a_scr, b_scr, carry):
    j = pl.program_id(1)
    t_rows = x_ref.shape[1]
    bw = V7X_MXU_DIM

    @pl.when(j == 0)
    def _():
        carry[...] = jnp.zeros_like(carry)

    x = x_ref[0]
    xb = x.astype(BF16)
    sp = _softplus(-lam_ref[...])
    for jj in range(LRU_W // bw):
        sl = slice(jj * bw, (jj + 1) * bw)
        r = _sigmoid(_dot(xb[:, sl], wa_ref[jj]) + ba_ref[:, sl])
        gi = _sigmoid(_dot(xb[:, sl], wx_ref[jj]) + bx_ref[:, sl])
        a = jnp.exp(-LRU_C * r * sp[:, sl])
        a_scr[:, sl] = a
        b_scr[:, sl] = jnp.sqrt(1.0 - a * a) * gi * x[:, sl]

    def body(t, hc):
        tt = t_rows - 1 - t if reverse else t
        hn = a_scr[pl.ds(tt, 1), :] * hc + b_scr[pl.ds(tt, 1), :]
        b_scr[pl.ds(tt, 1), :] = hn
        return hn

    carry[...] = lax.fori_loop(0, t_rows, body, carry[...], unroll=8)
    h_ref[0] = b_scr[...].astype(h_ref.dtype)


def _lru_blockdiag(w):
    per = V7X_MXU_DIM // LRU_BW
    nt = LRU_BLOCKS // per
    w4 = w.reshape(nt, per, LRU_BW, LRU_BW)
    eye = jnp.eye(per, dtype=w.dtype)
    t = jnp.einsum('tpcd,pq->tpcqd', w4, eye)
    return t.reshape(nt, V7X_MXU_DIM, V7X_MXU_DIM).astype(BF16)


def _lru(xc, wa, wx, ba, bx, lam, n_ctx, reverse):
    bsz, s, w = xc.shape
    tm = ROW_TILE
    nchunks = s // tm
    order = _chunk_order(n_ctx // tm, nchunks)
    d = 1 if reverse else 0
    row = lambda b, j: (b, order(d, j), 0)
    ntile = w // V7X_MXU_DIM
    full3 = pl.BlockSpec((ntile, V7X_MXU_DIM, V7X_MXU_DIM), lambda b, j: (0, 0, 0))
    vec = pl.BlockSpec((1, w), lambda b, j: (0, 0))
    return pl.pallas_call(
        functools.partial(_lru_kernel, reverse),
        grid=(bsz, nchunks),
        in_specs=[pl.BlockSpec((1, tm, w), row), full3, full3, vec, vec, vec],
        out_specs=pl.BlockSpec((1, tm, w), row),
        out_shape=jax.ShapeDtypeStruct((bsz, s, w), BF16),
        scratch_shapes=[pltpu.VMEM((tm, w), F32), pltpu.VMEM((tm, w), F32), pltpu.VMEM((1, w), F32)],
        compiler_params=_cparams(("arbitrary", "arbitrary"), 32),
        name="lru_bwd" if reverse else "lru_fwd",
    )(xc, _lru_blockdiag(wa), _lru_blockdiag(wx), ba.reshape(1, w), bx.reshape(1, w), lam.reshape(1, w))


def _route(lt, first, route_ref, cnt_ref):
    ne, tm = lt.shape
    erow = lax.broadcasted_iota(jnp.int32, (ne, tm), 0).astype(F32)
    lg = lt
    tops, hots = [], []
    for _ in range(TOP_K):
        m = jnp.max(lg, axis=0, keepdims=True)
        idx = jnp.min(jnp.where(lg == m, erow, float(ne)), axis=0, keepdims=True)
        hot = erow == idx
        lg = jnp.where(hot, -jnp.inf, lg)
        tops.append((m, idx))
        hots.append(hot)
    es = [jnp.exp(m - tops[0][0]) for m, _ in tops]
    denom = es[0]
    for e in es[1:]:
        denom = denom + e

    @pl.when(first)
    def _():
        cnt_ref[...] = jnp.zeros_like(cnt_ref)

    chosen = hots[0]
    for hot in hots[1:]:
        chosen = jnp.logical_or(chosen, hot)
    chosen_f = jnp.where(chosen, 1.0, 0.0)
    row = lax.broadcasted_iota(jnp.int32, (tm, tm), 0)
    col = lax.broadcasted_iota(jnp.int32, (tm, tm), 1)
    before = jnp.where(row < col, 1.0, 0.0).astype(BF16)
    ranks = _dot(chosen_f.astype(BF16), before) + cnt_ref[:, 0:1]
    cnt_ref[...] = cnt_ref[...] + jnp.sum(chosen_f, axis=1, keepdims=True)
    srow = lax.broadcasted_iota(jnp.int32, route_ref.shape, 0)
    out = jnp.zeros(route_ref.shape, F32)
    for kk in range(TOP_K):
        rank = jnp.sum(jnp.where(hots[kk], ranks, 0.0), axis=0, keepdims=True)
        out = jnp.where(srow == kk, tops[kk][1], out)
        out = jnp.where(srow == TOP_K + kk, rank, out)
        out = jnp.where(srow == 2 * TOP_K + kk, es[kk] / denom, out)
    route_ref[...] = out


def _tail(y, x_ref, mod, n2_ref, wrt_ref, br_ref, xo_ref, v_ref, route_ref, cnt_ref):
    xn = x_ref[0] + mod[2:3] * y
    xo_ref[0] = xn
    v = _modnorm(xn, n2_ref[...], mod[4:5], mod[3:4])
    v_ref[0] = v.astype(v_ref.dtype)
    logits_t = _dot_nt(wrt_ref[...], v, precision=HIGHEST) + br_ref[...]
    first = jnp.logical_and(pl.program_id(0) == 0, pl.program_id(1) == 0)
    _route(logits_t, first, route_ref, cnt_ref)


def _tail_specs(bsz, nt, tm, d, rows_per_sample):
    full = lambda shape: pl.BlockSpec(shape, lambda b, i: (0,) * len(shape))
    row = lambda b, i: (b, i, 0)
    in_specs = [full((N_EXPERTS, d)), full((N_EXPERTS, 1))]
    out_specs = [pl.BlockSpec((1, tm, d), row), pl.BlockSpec((1, tm, d), row),
                 pl.BlockSpec((ROUTE_ROWS, tm), lambda b, i: (0, b * nt + i)),
                 full((N_EXPERTS, V7X_LANES))]
    out_shape = [jax.ShapeDtypeStruct((bsz, rows_per_sample, d), F32),
                 jax.ShapeDtypeStruct((bsz, rows_per_sample, d), BF16),
                 jax.ShapeDtypeStruct((ROUTE_ROWS, bsz * rows_per_sample), F32),
                 jax.ShapeDtypeStruct((N_EXPERTS, V7X_LANES), F32)]
    return in_specs, out_specs, out_shape


def _even_out_kernel(hm_ref, hl0_ref, hl1_ref, o_ref, yg_ref, mg_ref, wout_ref,
                     x_ref, mod_ref, n2_ref, wr_ref, br_ref, xo_ref, v_ref, route_ref, cnt_ref):
    hm = hm_ref[0, 0].astype(F32) + hm_ref[1, 0].astype(F32)
    parts = []
    for h in range(MLSTM_HEADS):
        sl = slice(h * MLSTM_DH, (h + 1) * MLSTM_DH)
        parts.append(_rms(hm[:, sl], mg_ref[:, sl]))
    hmn = jnp.concatenate(parts, axis=1) * _sigmoid(o_ref[0].astype(F32))
    hl = (hl0_ref[0].astype(F32) + hl1_ref[0].astype(F32)) * _gelu_tanh(yg_ref[0].astype(F32))
    y = (_dot(hmn.astype(BF16), wout_ref[0:MLSTM_W, :])
         + _dot(hl.astype(BF16), wout_ref[MLSTM_W:MLSTM_W + LRU_W, :]))
    _tail(y, x_ref, mod_ref[0, 0], n2_ref, wr_ref, br_ref, xo_ref, v_ref, route_ref, cnt_ref)


def _even_out(hm, hl0, hl1, o_pre, yg, mnorm_g, w_out, x, mod, n2, w_r, b_r, n_ctx):
    bsz, s, d = x.shape
    tm = ROW_TILE
    nt = s // tm
    nct = n_ctx // tm
    row = lambda b, i: (b, i, 0)
    full = lambda shape: pl.BlockSpec(shape, lambda b, i: (0,) * len(shape))
    tail_in, out_specs, out_shape = _tail_specs(bsz, nt, tm, d, s)
    return pl.pallas_call(
        _even_out_kernel,
        grid=(bsz, nt),
        in_specs=[pl.BlockSpec((2, 1, tm, MLSTM_W), lambda b, i: (0, b, i, 0)),
                  pl.BlockSpec((1, tm, LRU_W), row), pl.BlockSpec((1, tm, LRU_W), row),
                  pl.BlockSpec((1, tm, MLSTM_W), row), pl.BlockSpec((1, tm, LRU_W), row),
                  full((1, MLSTM_W)), full((MLSTM_W + LRU_W, d)),
                  pl.BlockSpec((1, tm, d), row),
                  pl.BlockSpec((1, 1, 8, d), lambda b, i: (b, (i >= nct).astype(jnp.int32), 0, 0)),
                  full((1, d))] + tail_in,
        out_specs=out_specs,
        out_shape=out_shape,
        compiler_params=_cparams(("arbitrary", "arbitrary"), 48),
        name="even_out",
    )(hm, hl0, hl1, o_pre, yg, mnorm_g.reshape(1, MLSTM_W), w_out, x, mod, n2.reshape(1, d),
      w_r.T, b_r.reshape(N_EXPERTS, 1))


def _odd_out_kernel(a_ref, wout_ref, x_ref, mod_ref, n2_ref, wr_ref, br_ref, xo_ref, v_ref, route_ref, cnt_ref):
    y = _dot(a_ref[0], wout_ref[...])
    _tail(y, x_ref, mod_ref[0, 0], n2_ref, wr_ref, br_ref, xo_ref, v_ref, route_ref, cnt_ref)


def _odd_out(attn, w_out, x, mod, n2, w_r, b_r, n_ctx):
    bsz, seq, d = attn.shape
    tm = ROW_TILE
    nt = seq // tm
    nct = n_ctx // tm
    row = lambda b, i: (b, i, 0)
    full = lambda shape: pl.BlockSpec(shape, lambda b, i: (0,) * len(shape))
    tail_in, out_specs, out_shape = _tail_specs(bsz, nt, tm, d, seq)
    return pl.pallas_call(
        _odd_out_kernel,
        grid=(bsz, nt),
        in_specs=[pl.BlockSpec((1, tm, d), row), full((d, d)),
                  pl.BlockSpec((1, tm, d), lambda b, i: (b, i + nct, 0)),
                  pl.BlockSpec((1, 1, 8, d), lambda b, i: (b, 1, 0, 0)),
                  full((1, d))] + tail_in,
        out_specs=out_specs,
        out_shape=out_shape,
        compiler_params=_cparams(("arbitrary", "arbitrary"), 32),
        name="odd_out",
    )(attn, w_out, x, mod, n2.reshape(1, d), w_r.T, b_r.reshape(N_EXPERTS, 1))


def _expert_kernel(te_ref, nu_ref, x_ref, w1_ref, b1_ref, w2_ref, b2_ref, rw_ref, o_ref, w1b, w2b):
    t = pl.program_id(0)
    used = t < nu_ref[0]
    new_expert = jnp.logical_or(t == 0, te_ref[t] != te_ref[jnp.maximum(t - 1, 0)])

    @pl.when(jnp.logical_and(used, new_expert))
    def _():
        w1b[...] = w1_ref[0, 0].astype(BF16)
        w2b[...] = w2_ref[0, 0].astype(BF16)

    @pl.when(used)
    def _():
        tm = x_ref.shape[0]
        hid = _dot(x_ref[...], w1b[...]) + b1_ref[0, 0]
        gate = jnp.minimum(hid[:, :D_FF], SWIGLU_LIMIT)
        up = jnp.clip(hid[:, D_FF:], -SWIGLU_LIMIT, SWIGLU_LIMIT)
        act = (up + 1.0) * gate * _sigmoid(SWIGLU_ALPHA * gate)
        y = _dot(act.astype(BF16), w2b[...]) + b2_ref[0, 0]
        w_col = jnp.transpose(jnp.broadcast_to(rw_ref[0], (V7X_LANES, tm)))[:, 0:1]
        o_ref[...] = (y * w_col).astype(o_ref.dtype)

    @pl.when(jnp.logical_not(used))
    def _():
        o_ref[...] = jnp.zeros_like(o_ref)


def _experts(xs, row_w, tile_expert, n_used, layer, w1, b1, w2, b2):
    rows, d = xs.shape
    tm = MOE_TILE
    nt = rows // tm
    nl, ne, _, ff2 = w1.shape
    return pl.pallas_call(
        _expert_kernel,
        grid_spec=pltpu.PrefetchScalarGridSpec(
            num_scalar_prefetch=2,
            grid=(nt,),
            in_specs=[pl.BlockSpec((tm, d), lambda t, te, nu: (t, 0)),
                      pl.BlockSpec((1, 1, d, ff2), lambda t, te, nu: (layer, te[t], 0, 0)),
                      pl.BlockSpec((1, 1, 1, ff2), lambda t, te, nu: (layer, te[t], 0, 0)),
                      pl.BlockSpec((1, 1, ff2 // 2, d), lambda t, te, nu: (layer, te[t], 0, 0)),
                      pl.BlockSpec((1, 1, 1, d), lambda t, te, nu: (layer, te[t], 0, 0)),
                      pl.BlockSpec((1, 1, tm), lambda t, te, nu: (t, 0, 0))],
            out_specs=pl.BlockSpec((tm, d), lambda t, te, nu: (t, 0)),
            scratch_shapes=[pltpu.VMEM((d, ff2), BF16), pltpu.VMEM((ff2 // 2, d), BF16)],
        ),
        out_shape=jax.ShapeDtypeStruct((rows, d), BF16),
        compiler_params=_cparams(("arbitrary",), 56),
        name="moe_experts",
    )(tile_expert, n_used, xs, w1, b1.reshape(nl, ne, 1, ff2), w2, b2.reshape(nl, ne, 1, d), row_w)


def _moe(v, route, counts, layer, w1, b1, w2, b2):
    t, d = v.shape
    tm = MOE_TILE
    nrows = t * TOP_K
    nt = -(-nrows // tm) + N_EXPERTS
    idx = route[0:TOP_K].astype(jnp.int32)
    rank = route[TOP_K:2 * TOP_K].astype(jnp.int32)
    weight = route[2 * TOP_K:3 * TOP_K]
    sizes = counts[:, 0].astype(jnp.int32)
    start = jnp.cumsum(sizes) - sizes
    padded = (sizes + tm - 1) // tm * tm
    pad_end = jnp.cumsum(padded)
    pad_start = pad_end - padded
    pair_pos = rank
    for e in range(N_EXPERTS):
        pair_pos = pair_pos + jnp.where(idx == e, pad_start[e], 0)
    bits = max(1, (nrows - 1).bit_length())
    assert N_EXPERTS << bits < 2 ** 31
    pair_id = (jnp.arange(t, dtype=jnp.int32)[None, :] * TOP_K + jnp.arange(TOP_K, dtype=jnp.int32)[:, None])
    order = jnp.sort(((idx << bits) + pair_id).reshape(-1)) & ((1 << bits) - 1)
    tile_row0 = jnp.arange(nt, dtype=jnp.int32) * tm
    tile_expert = jnp.minimum(jnp.sum(pad_end[None, :] <= tile_row0[:, None], axis=1, dtype=jnp.int32),
                              N_EXPERTS - 1)
    n_used = (pad_end[-1] // tm).reshape(1).astype(jnp.int32)
    onehot_te = tile_expert[:, None] == jnp.arange(N_EXPERTS, dtype=jnp.int32)[None, :]
    pick = lambda tbl: jnp.sum(jnp.where(onehot_te, tbl[None, :], 0), axis=1)
    r_in = tile_row0[:, None] + jnp.arange(tm, dtype=jnp.int32)[None, :] - pick(pad_start)[:, None]
    valid = r_in < pick(sizes)[:, None]
    src = jnp.where(valid, pick(start)[:, None] + r_in, 0).reshape(-1)
    pair = order.at[src].get(mode='promise_in_bounds')
    row_token = pair // TOP_K
    flat_w = weight.reshape(-1).at[(pair % TOP_K) * t + row_token].get(mode='promise_in_bounds')
    row_w = jnp.where(valid, flat_w.reshape(nt, tm), 0.0)
    xs = v.at[row_token].get(mode='promise_in_bounds')
    ys = _experts(xs, row_w.reshape(nt, 1, tm), tile_expert, n_used, layer, w1, b1, w2, b2)
    return [ys.at[pair_pos[kk]].get(mode='promise_in_bounds') for kk in range(TOP_K)]


def _rope(t, cos, sin, lane_lo):
    swapped = jnp.where(lane_lo, pltpu.roll(t, ATT_DH - ROPE_AXIS_DIM // 2, 1),
                        pltpu.roll(t, ROPE_AXIS_DIM // 2, 1))
    return t * cos + swapped * sin


def _proj_odd_kernel(x_ref, f0, f1, f2, f3, mod0_ref, mod_ref, g_ref, w_ref, qg_ref, kg_ref,
                     cos_ref, sin_ref, h_ref, q_ref, k_ref, v_ref):
    f = (f0[0].astype(F32) + f1[0].astype(F32)) + (f2[0].astype(F32) + f3[0].astype(F32))
    hcur = x_ref[0] + mod0_ref[0, 0][5:6] * f
    h_ref[0] = hcur
    mod = mod_ref[0, 0]
    u = _modnorm(hcur, g_ref[...], mod[1:2], mod[0:1]).astype(BF16)
    z = _dot(u, w_ref[...])
    cos = cos_ref[...]
    sin = sin_ref[...]
    lane = lax.broadcasted_iota(jnp.int32, cos.shape, 1)
    lane_lo = (lane % ROPE_AXIS_DIM) < (ROPE_AXIS_DIM // 2)
    qw = ATT_HEADS * ATT_DH
    kw = ATT_KV_HEADS * ATT_DH
    for hh in range(ATT_HEADS):
        sl = slice(hh * ATT_DH, (hh + 1) * ATT_DH)
        t = _rope(_rms(z[:, sl], qg_ref[...]), cos, sin, lane_lo)
        q_ref[0, :, sl] = (t * (ATT_DH ** -0.5 * LOG2_E)).astype(q_ref.dtype)
    for hh in range(ATT_KV_HEADS):
        sl = slice(hh * ATT_DH, (hh + 1) * ATT_DH)
        t = _rope(_rms(z[:, qw + hh * ATT_DH:qw + (hh + 1) * ATT_DH], kg_ref[...]), cos, sin, lane_lo)
        k_ref[0, :, sl] = t.astype(k_ref.dtype)
    v_ref[0] = z[:, qw + kw:qw + 2 * kw].astype(v_ref.dtype)


def _proj_odd(x, fparts, mod0, mod, g, w, qg, kg, cos_tab, sin_tab, n_ctx):
    bsz, s, d = x.shape
    tm = ROW_TILE
    nt = s // tm
    nct = n_ctx // tm
    n = w.shape[1]
    qw = ATT_HEADS * ATT_DH
    kw = ATT_KV_HEADS * ATT_DH
    row = lambda b, i: (b, i, 0)
    seg = lambda b, i: (b, (i >= nct).astype(jnp.int32), 0, 0)
    full = lambda shape: pl.BlockSpec(shape, lambda b, i: (0,) * len(shape))
    tab = pl.BlockSpec((tm, ATT_DH), lambda b, i: (i, 0))
    return pl.pallas_call(
        _proj_odd_kernel,
        grid=(bsz, nt),
        in_specs=[pl.BlockSpec((1, tm, d), row)] + [pl.BlockSpec((1, tm, d), row)] * TOP_K
                 + [pl.BlockSpec((1, 1, 8, d), seg), pl.BlockSpec((1, 1, 8, d), seg),
                    full((1, d)), full((d, n)), full((1, ATT_DH)), full((1, ATT_DH)), tab, tab],
        out_specs=[pl.BlockSpec((1, tm, d), row), pl.BlockSpec((1, tm, qw), row),
                   pl.BlockSpec((1, tm, kw), row), pl.BlockSpec((1, tm, kw), row)],
        out_shape=[jax.ShapeDtypeStruct((bsz, s, d), F32),
                   jax.ShapeDtypeStruct((bsz, s, qw), BF16),
                   jax.ShapeDtypeStruct((bsz, s, kw), BF16),
                   jax.ShapeDtypeStruct((bsz, s, kw), BF16)],
        compiler_params=_cparams(("arbitrary", "arbitrary"), 48),
        name="proj_odd",
    )(x, *fparts, mod0, mod, g.reshape(1, d), w, qg.reshape(1, ATT_DH), kg.reshape(1, ATT_DH),
      cos_tab, sin_tab)


def _rope_tables(n_ctx, seq):
    rows = seq // GRID_W
    pos_r = jnp.repeat(jnp.arange(rows), GRID_W).astype(F32)
    pos_c = jnp.tile(jnp.arange(GRID_W), rows).astype(F32)
    inv_freq = ROPE_THETA ** (-jnp.arange(0, ROPE_AXIS_DIM, 2, dtype=F32) / ROPE_AXIS_DIM)
    ar = pos_r[:, None] * inv_freq
    ac = pos_c[:, None] * inv_freq
    cos = jnp.concatenate([jnp.cos(ar), jnp.cos(ar), jnp.cos(ac), jnp.cos(ac)], axis=-1)
    sin = jnp.concatenate([-jnp.sin(ar), jnp.sin(ar), -jnp.sin(ac), jnp.sin(ac)], axis=-1)
    cos = jnp.concatenate([jnp.ones((n_ctx, ATT_DH), F32), cos], axis=0)
    sin = jnp.concatenate([jnp.zeros((n_ctx, ATT_DH), F32), sin], axis=0)
    return cos, sin


def _attn_kernel(q_ref, k_ref, v_ref, o_ref):
    k = k_ref[0]
    v = v_ref[0]
    for g in range(ATT_GROUP):
        sl = slice(g * ATT_DH, (g + 1) * ATT_DH)
        s = _dot_nt(q_ref[0, :, sl], k)
        p = jnp.exp2(s - jnp.max(s, axis=1, keepdims=True))
        l = jnp.sum(p, axis=1, keepdims=True)
        o_ref[0, :, sl] = (_dot(p.astype(BF16), v) / l).astype(o_ref.dtype)


def _attention(q, k, v, n_ctx):
    bsz, s, qw = q.shape
    seq = s - n_ctx
    tq = ATT_Q_TILE
    nct = n_ctx // tq
    gw = ATT_GROUP * ATT_DH
    return pl.pallas_call(
        _attn_kernel,
        grid=(bsz, ATT_KV_HEADS, seq // tq),
        in_specs=[pl.BlockSpec((1, tq, gw), lambda b, h, i: (b, i + nct, h)),
                  pl.BlockSpec((1, s, ATT_DH), lambda b, h, i: (b, 0, h)),
                  pl.BlockSpec((1, s, ATT_DH), lambda b, h, i: (b, 0, h))],
        out_specs=pl.BlockSpec((1, tq, gw), lambda b, h, i: (b, i, h)),
        out_shape=jax.ShapeDtypeStruct((bsz, seq, qw), BF16),
        compiler_params=_cparams(("arbitrary", "arbitrary", "arbitrary"), 48),
        name="attention",
    )(q, k, v)


def _final_kernel(x_ref, f0, f1, f2, f3, mod_ref, g_ref, o_ref):
    f = (f0[0].astype(F32) + f1[0].astype(F32)) + (f2[0].astype(F32) + f3[0].astype(F32))
    o_ref[0] = _rms(x_ref[0] + mod_ref[0, 0][5:6] * f, g_ref[...])


def _final(x, fparts, mod, g):
    bsz, seq, d = x.shape
    tm = ROW_TILE
    row = lambda b, i: (b, i, 0)
    return pl.pallas_call(
        _final_kernel,
        grid=(bsz, seq // tm),
        in_specs=[pl.BlockSpec((1, tm, d), row)] * (1 + TOP_K)
                 + [pl.BlockSpec((1, 1, 8, d), lambda b, i: (b, 1, 0, 0)),
                    pl.BlockSpec((1, d), lambda b, i: (0, 0))],
        out_specs=pl.BlockSpec((1, tm, d), row),
        out_shape=jax.ShapeDtypeStruct((bsz, seq, d), F32),
        compiler_params=_cparams(("arbitrary", "arbitrary"), 32),
        name="final_norm",
    )(x, *fparts, mod, g.reshape(1, d))


def _pack_even_w_in(w_in):
    w4 = 4 * MLSTM_W
    ng = 4 * MLSTM_HEADS
    wg = w_in[:, w4:w4 + ng]
    half = ng // 2
    pad = jnp.zeros((w_in.shape[0], GATE_PAD - half), w_in.dtype)
    packed = jnp.concatenate([w_in[:, :w4], w_in[:, w4 + ng:], wg[:, :half], pad, wg[:, half:], pad], axis=1)
    return packed.astype(BF16), wg.T.astype(BF16)


def kernel(x, c, ctx, c_ctx, mod_w, mod_b, norm1_g, norm2_g, final_g, ev_w_in, ev_qk_conv_w, ev_qk_conv_b, ev_gate_b, ev_mnorm_g, ev_lru_conv_w, ev_lru_conv_b, ev_lru_wa, ev_lru_ba, ev_lru_wx, ev_lru_bx, ev_lru_lam, ev_w_out, od_w_in, od_q_norm_g, od_k_norm_g, od_w_out, moe_w_r, moe_b_r, moe_w1, moe_b1, moe_w2, moe_b2):
    bsz, seq, d = x.shape
    n_ctx = ctx.shape[1]
    s = n_ctx + seq
    assert n_ctx % ROW_TILE == 0 and seq % ROW_TILE == 0 and seq % GRID_W == 0
    h = jnp.concatenate([ctx, x], axis=1)

    mod0 = _mod_table(c, c_ctx, mod_w[0], mod_b[0])
    w_packed, wg_t = _pack_even_w_in(ev_w_in[0])
    qk_pre, v, o_pre, xr, yg, gc, gr = _proj_even(h, mod0, norm1_g[0], w_packed, wg_t, n_ctx)
    q, k, xc = _conv(qk_pre, xr, ev_qk_conv_w[0], ev_qk_conv_b[0], ev_lru_conv_w[0], ev_lru_conv_b[0], n_ctx)
    hm = _mlstm(q, k, v, gc, gr, ev_gate_b[0], n_ctx)
    hl = [_lru(xc, ev_lru_wa[0, dd], ev_lru_wx[0, dd], ev_lru_ba[0, dd], ev_lru_bx[0, dd],
               ev_lru_lam[0, dd], n_ctx, dd == 1) for dd in range(2)]
    x_mid, v0, route0, cnt0 = _even_out(hm, hl[0], hl[1], o_pre, yg, ev_mnorm_g[0], ev_w_out[0].astype(BF16),
                                        h, mod0, norm2_g[0], moe_w_r[0], moe_b_r[0], n_ctx)
    f0 = _moe(v0.reshape(bsz * s, d), route0, cnt0, 0, moe_w1, moe_b1, moe_w2, moe_b2)
    f0 = [p.reshape(bsz, s, d) for p in f0]

    mod1 = _mod_table(c, c_ctx, mod_w[1], mod_b[1])
    cos_tab, sin_tab = _rope_tables(n_ctx, seq)
    h1, q1, k1, v1 = _proj_odd(x_mid, f0, mod0, mod1, norm1_g[1], od_w_in[0].astype(BF16),
                               od_q_norm_g[0], od_k_norm_g[0], cos_tab, sin_tab, n_ctx)
    attn = _attention(q1, k1, v1, n_ctx)
    x2, v2, route2, cnt2 = _odd_out(attn, od_w_out[0].astype(BF16), h1, mod1, norm2_g[1],
                                    moe_w_r[1], moe_b_r[1], n_ctx)
    f1 = _moe(v2.reshape(bsz * seq, d), route2, cnt2, 1, moe_w1, moe_b1, moe_w2, moe_b2)
    f1 = [p.reshape(bsz, seq, d) for p in f1]
    return _final(x2, f1, mod1, final_g)
```

```python
import functools

import jax
import jax.numpy as jnp
from jax import lax
from jax.experimental import pallas as pl
from jax.experimental.pallas import tpu as pltpu

F32 = jnp.float32
BF16 = jnp.bfloat16
HIGHEST = lax.Precision.HIGHEST

EPS = 1e-6
M_INIT = -1e30
NEG_BIG = -1e30

MLSTM_HEADS = 4
MLSTM_DH = 256
MLSTM_W = MLSTM_HEADS * MLSTM_DH
LRU_W = 1024
LRU_BLOCKS = 16
LRU_BW = LRU_W // LRU_BLOCKS
LRU_C = 8.0
CONV_W = 4
CONV_LEFT = 2
ATT_HEADS = 8
ATT_KV_HEADS = 2
ATT_GROUP = ATT_HEADS // ATT_KV_HEADS
ATT_DH = 128
GRID_W = 64
ROPE_AXIS_DIM = ATT_DH // 2
ROPE_THETA = 10000.0
N_EXPERTS = 32
TOP_K = 4
D_FF = 1024
SWIGLU_ALPHA = 1.702
SWIGLU_LIMIT = 7.0
LOG2_E = 1.4426950408889634

V7X_LANES = 128
V7X_MXU_DIM = 256
V7X_VMEM_BYTES = 64 * 1024 * 1024
MIB = 1024 * 1024

ROW_TILE = 256
HALO = 16
MOE_TILE = 256
MOE_CHUNKS = 8
ROUTE_ROWS = 16
ATT_Q_TILE = 256
GATE_PAD = V7X_LANES


def _cparams(semantics, vmem_mib):
    assert vmem_mib * MIB < V7X_VMEM_BYTES
    return pltpu.CompilerParams(dimension_semantics=semantics, vmem_limit_bytes=vmem_mib * MIB)


def _dot(a, b):
    return jnp.dot(a, b, preferred_element_type=F32)


def _dot_nt(a, b, precision=None):
    return lax.dot_general(a, b, (((1,), (1,)), ((), ())), precision=precision,
                           preferred_element_type=F32)


def _dot_tn(a, b):
    return lax.dot_general(a, b, (((0,), (0,)), ((), ())), preferred_element_type=F32)


def _sigmoid(x):
    return jax.nn.sigmoid(x)


def _log_sigmoid(x):
    return jnp.minimum(x, 0.0) - jnp.log1p(jnp.exp(-jnp.abs(x)))


def _softplus(x):
    return jnp.maximum(x, 0.0) + jnp.log1p(jnp.exp(-jnp.abs(x)))


def _gelu_tanh(x):
    return 0.5 * x * (1.0 + jnp.tanh(0.7978845608028654 * (x + 0.044715 * x * x * x)))


def _rms(x, g):
    return x * lax.rsqrt(jnp.mean(x * x, axis=-1, keepdims=True) + EPS) * g


def _modnorm(x, g, scale, shift):
    return _rms(x, g) * (1.0 + scale) + shift


def _modvec_kernel(c_ref, w_ref, b_ref, o_ref):
    c = c_ref[...]
    s = c * _sigmoid(c)
    o_ref[...] = jnp.dot(s, w_ref[...], precision=HIGHEST, preferred_element_type=F32) + b_ref[...]


def _modvec(cc, w, b):
    rows, d = cc.shape
    n = w.shape[1]
    tn = 1536
    return pl.pallas_call(
        _modvec_kernel,
        grid=(n // tn,),
        in_specs=[pl.BlockSpec((rows, d), lambda j: (0, 0)),
                  pl.BlockSpec((d, tn), lambda j: (0, j)),
                  pl.BlockSpec((1, tn), lambda j: (0, j))],
        out_specs=pl.BlockSpec((rows, tn), lambda j: (0, j)),
        out_shape=jax.ShapeDtypeStruct((rows, n), F32),
        compiler_params=_cparams(("arbitrary",), 32),
        name="modvec",
    )(cc, w, b.reshape(1, n))


def _mod_table(c, c_ctx, mod_w, mod_b):
    bsz, d = c.shape
    rows = ((bsz + 1 + 7) // 8) * 8
    cc = jnp.zeros((rows, d), F32).at[:bsz].set(c).at[bsz].set(c_ctx)
    mod = _modvec(cc, mod_w, mod_b)
    lat = mod[:bsz].reshape(bsz, 6, d)
    ctx = jnp.broadcast_to(mod[bsz].reshape(1, 6, d), (bsz, 6, d))
    tbl = jnp.stack([ctx, lat], axis=1)
    return jnp.pad(tbl, ((0, 0), (0, 0), (0, 2), (0, 0)))


def _proj_even_kernel(nct, x_ref, xp_ref, xn_ref, mod_ref, g_ref, w_ref, wgt_ref,
                      wqk_ref, bqk_ref, wxr_ref, bxr_ref,
                      q_ref, k_ref, v_ref, o_ref, xc_ref, yg_ref, gc_ref, gr_ref, ext_qk, ext_xr):
    i = pl.program_id(1)
    nt = pl.num_programs(1)
    tm = x_ref.shape[1]
    w = MLSTM_W
    first = jnp.logical_or(i == 0, i == nct)
    last = jnp.logical_or(i == nct - 1, i == nt - 1)
    mod = mod_ref[0, 0]
    xe = jnp.concatenate([xp_ref[0], x_ref[0], xn_ref[0]], axis=0)
    ue = _modnorm(xe, g_ref[...], mod[1:2], mod[0:1]).astype(BF16)
    rowi = lax.broadcasted_iota(jnp.int32, (tm + 2 * HALO, 1), 0)
    keep = jnp.where(rowi < HALO, jnp.where(first, 0.0, 1.0),
                     jnp.where(rowi >= HALO + tm, jnp.where(last, 0.0, 1.0), 1.0))

    def conv(z, ext, w_ref, b_ref):
        ext[...] = z * keep
        acc = b_ref[...] + w_ref[0:1, :] * ext[pl.ds(HALO - CONV_LEFT, tm), :]
        for j in range(1, CONV_W):
            acc = acc + w_ref[j:j + 1, :] * ext[pl.ds(HALO - CONV_LEFT + j, tm), :]
        return acc

    y = conv(_dot(ue, w_ref[:, 0:2 * w]), ext_qk, wqk_ref, bqk_ref)
    y = y * _sigmoid(y)
    q_ref[0] = y[:, :w].astype(q_ref.dtype)
    k_ref[0] = (y[:, w:] * (MLSTM_DH ** -0.5)).astype(k_ref.dtype)
    xc_ref[0] = conv(_dot(ue, w_ref[:, 4 * w:4 * w + LRU_W]), ext_xr, wxr_ref, bxr_ref)
    u = ue[HALO:HALO + tm]
    v_ref[0] = _dot(u, w_ref[:, 2 * w:3 * w]).astype(v_ref.dtype)
    o_ref[0] = _dot(u, w_ref[:, 3 * w:4 * w]).astype(o_ref.dtype)
    yg_ref[0] = _dot(u, w_ref[:, 4 * w + LRU_W:4 * w + 2 * LRU_W]).astype(yg_ref.dtype)
    gc_ref[0] = _dot(u, w_ref[:, 4 * w + 2 * LRU_W:4 * w + 2 * LRU_W + 2 * GATE_PAD])
    gr_ref[0] = _dot_nt(wgt_ref[...], u)


def _proj_even(h, mod, g, w_packed, wg_t, wqk, bqk, wxr, bxr, n_ctx):
    bsz, s, d = h.shape
    tm = ROW_TILE
    nt = s // tm
    nct = n_ctx // tm
    ntot = w_packed.shape[1]
    ng = wg_t.shape[0]
    hb = tm // HALO
    nhb = s // HALO
    cq = 2 * MLSTM_W
    row = lambda b, i: (b, i, 0)
    prev = lambda b, i: (b, jnp.maximum(i * hb - 1, 0), 0)
    nxt = lambda b, i: (b, jnp.minimum((i + 1) * hb, nhb - 1), 0)
    full = lambda shape: pl.BlockSpec(shape, lambda b, i: (0, 0))
    return pl.pallas_call(
        functools.partial(_proj_even_kernel, nct),
        grid=(bsz, nt),
        in_specs=[pl.BlockSpec((1, tm, d), row), pl.BlockSpec((1, HALO, d), prev),
                  pl.BlockSpec((1, HALO, d), nxt),
                  pl.BlockSpec((1, 1, 8, d), lambda b, i: (b, (i >= nct).astype(jnp.int32), 0, 0)),
                  full((1, d)), full((d, ntot)), full((ng, d)),
                  full((CONV_W, cq)), full((1, cq)), full((CONV_W, LRU_W)), full((1, LRU_W))],
        out_specs=[pl.BlockSpec((1, tm, MLSTM_W), row),
                   pl.BlockSpec((1, tm, MLSTM_W), row),
                   pl.BlockSpec((1, tm, MLSTM_W), row),
                   pl.BlockSpec((1, tm, MLSTM_W), row),
                   pl.BlockSpec((1, tm, LRU_W), row),
                   pl.BlockSpec((1, tm, LRU_W), row),
                   pl.BlockSpec((1, tm, 2 * GATE_PAD), row),
                   pl.BlockSpec((1, ng, tm), lambda b, i: (b, 0, i))],
        out_shape=[jax.ShapeDtypeStruct((bsz, s, MLSTM_W), BF16),
                   jax.ShapeDtypeStruct((bsz, s, MLSTM_W), BF16),
                   jax.ShapeDtypeStruct((bsz, s, MLSTM_W), BF16),
                   jax.ShapeDtypeStruct((bsz, s, MLSTM_W), BF16),
                   jax.ShapeDtypeStruct((bsz, s, LRU_W), F32),
                   jax.ShapeDtypeStruct((bsz, s, LRU_W), BF16),
                   jax.ShapeDtypeStruct((bsz, s, 2 * GATE_PAD), F32),
                   jax.ShapeDtypeStruct((bsz, ng, s), F32)],
        scratch_shapes=[pltpu.VMEM((tm + 2 * HALO, cq), F32),
                        pltpu.VMEM((tm + 2 * HALO, LRU_W), F32)],
        compiler_params=_cparams(("arbitrary", "arbitrary"), 56),
        name="proj_even",
    )(h, h, h, mod, g.reshape(1, d), w_packed, wg_t, wqk, bqk.reshape(1, cq), wxr, bxr.reshape(1, LRU_W))


def _mlstm_kernel(q_ref, k_ref, v_ref, gc_ref, gr_ref, bc_ref, br_ref, h_ref, c_scr, n_scr, m_scr):
    d = pl.program_id(1)
    j = pl.program_id(2)
    lc = q_ref.shape[1]
    nh = MLSTM_HEADS
    dh = MLSTM_DH

    @pl.when(j == 0)
    def _():
        c_scr[...] = jnp.zeros_like(c_scr)
        n_scr[...] = jnp.zeros_like(n_scr)
        m_scr[...] = jnp.full(m_scr.shape, M_INIT, F32)

    row = lax.broadcasted_iota(jnp.int32, (lc, lc), 0)
    col = lax.broadcasted_iota(jnp.int32, (lc, lc), 1)
    lo = jnp.where(d == 1, row, col)
    hi = jnp.where(d == 1, col, row)
    tri = lo <= hi
    trif = tri.astype(F32)

    gc = gc_ref[0] + bc_ref[0]
    gr = gr_ref[0, 0] + br_ref[0]
    lfr = _log_sigmoid(gr)
    bcum_c = jnp.dot(trif, _log_sigmoid(gc), precision=HIGHEST, preferred_element_type=F32)
    bcum_r = _dot_nt(lfr, trif, precision=HIGHEST)

    for h in range(nh):
        sl = slice(h * dh, (h + 1) * dh)
        q = q_ref[0, :, sl]
        k = k_ref[0, :, sl]
        v = v_ref[0, :, sl]
        i_col = gc[:, h:h + 1]
        b_col = bcum_c[:, nh + h:nh + h + 1]
        i_row = gr[h:h + 1, :]
        b_row = bcum_r[nh + h:nh + h + 1, :]
        m_prev = m_scr[h][0:1, 0:1]
        c_mat = c_scr[h]
        n_vec = n_scr[h]

        log_intra = jnp.where(tri, b_col - b_row + i_row, NEG_BIG)
        log_inter = b_col + m_prev
        m_t = jnp.maximum(log_inter, jnp.max(log_intra, axis=1, keepdims=True))
        w_inter = jnp.exp(log_inter - m_t)
        scores = _dot_nt(q, k) * jnp.exp(log_intra - m_t)
        num = w_inter * _dot(q, c_mat.astype(BF16)) + _dot(scores.astype(BF16), v)
        den = (w_inter * jnp.sum(q.astype(F32) * n_vec, axis=1, keepdims=True)
               + jnp.sum(scores, axis=1, keepdims=True))
        hh = num / jnp.maximum(jnp.abs(den), jnp.exp(-m_t))
        h_ref[0, 0, :, sl] = hh.astype(h_ref.dtype)

        total_f = jnp.sum(lfr[nh + h:nh + h + 1, :], axis=1, keepdims=True)
        log_w_row = total_f - b_row + i_row
        m_new = jnp.maximum(total_f + m_prev, jnp.max(log_w_row, axis=1, keepdims=True))
        decay = jnp.exp(total_f + m_prev - m_new)
        w_col = jnp.exp(total_f - b_col + i_col - m_new)
        wv = (w_col * v.astype(F32)).astype(BF16)
        c_scr[h] = decay * c_mat + _dot_tn(k, wv)
        n_scr[h] = decay * n_vec + jnp.sum(w_col * k.astype(F32), axis=0, keepdims=True)
        m_scr[h] = jnp.broadcast_to(m_new, m_scr.shape[1:])


def _chunk_order(n_ctx_chunks, nchunks):
    def order(d, j):
        bwd = jnp.where(j < n_ctx_chunks, n_ctx_chunks - 1 - j, nchunks - 1 - (j - n_ctx_chunks))
        return jnp.where(d == 0, j, bwd)
    return order


def _mlstm(q, k, v, gc, gr, gate_b, n_ctx):
    bsz, s, w = q.shape
    lc = ROW_TILE
    nchunks = s // lc
    order = _chunk_order(n_ctx // lc, nchunks)
    nh = MLSTM_HEADS
    gb = gate_b.reshape(2, 2 * nh)
    bc = jnp.pad(gb, ((0, 0), (0, GATE_PAD - 2 * nh))).reshape(2, 1, GATE_PAD)
    br = gb.reshape(2, 2 * nh, 1)
    gr4 = gr.reshape(bsz, 2, 2 * nh, s)
    row = lambda b, d, j: (b, order(d, j), 0)
    return pl.pallas_call(
        _mlstm_kernel,
        grid=(bsz, 2, nchunks),
        in_specs=[pl.BlockSpec((1, lc, w), row), pl.BlockSpec((1, lc, w), row),
                  pl.BlockSpec((1, lc, w), row),
                  pl.BlockSpec((1, lc, GATE_PAD), lambda b, d, j: (b, order(d, j), d)),
                  pl.BlockSpec((1, 1, 2 * nh, lc), lambda b, d, j: (b, d, 0, order(d, j))),
                  pl.BlockSpec((1, 1, GATE_PAD), lambda b, d, j: (d, 0, 0)),
                  pl.BlockSpec((1, 2 * nh, 1), lambda b, d, j: (d, 0, 0))],
        out_specs=pl.BlockSpec((1, 1, lc, w), lambda b, d, j: (d, b, order(d, j), 0)),
        out_shape=jax.ShapeDtypeStruct((2, bsz, s, w), BF16),
        scratch_shapes=[pltpu.VMEM((nh, MLSTM_DH, MLSTM_DH), F32),
                        pltpu.VMEM((nh, 1, MLSTM_DH), F32),
                        pltpu.VMEM((nh, 8, V7X_LANES), F32)],
        compiler_params=_cparams(("arbitrary", "arbitrary", "arbitrary"), 48),
        name="mlstm",
    )(q, k, v, gc, gr4, bc, br)


def _lru_kernel(reverse, x_ref, wa_ref, wx_ref, ba_ref, bx_ref, lam_ref, h_ref, a_scr, b_scr, carry):
    j = pl.program_id(1)
    t_rows = x_ref.shape[1]
    bw = V7X_MXU_DIM

    @pl.when(j == 0)
    def _():
        carry[...] = jnp.zeros_like(carry)

    x = x_ref[0]
    xb = x.astype(BF16)
    sp = _softplus(-lam_ref[...])
    for jj in range(LRU_W // bw):
        sl = slice(jj * bw, (jj + 1) * bw)
        r = _sigmoid(_dot(xb[:, sl], wa_ref[jj]) + ba_ref[:, sl])
        gi = _sigmoid(_dot(xb[:, sl], wx_ref[jj]) + bx_ref[:, sl])
        a = jnp.exp(-LRU_C * r * sp[:, sl])
        a_scr[:, sl] = a
        b_scr[:, sl] = jnp.sqrt(1.0 - a * a) * gi * x[:, sl]

    def body(t, hc):
        tt = t_rows - 1 - t if reverse else t
        hn = a_scr[pl.ds(tt, 1), :] * hc + b_scr[pl.ds(tt, 1), :]
        b_scr[pl.ds(tt, 1), :] = hn
        return hn

    carry[...] = lax.fori_loop(0, t_rows, body, carry[...], unroll=8)
    h_ref[0] = b_scr[...].astype(h_ref.dtype)


def _lru_blockdiag(w):
    per = V7X_MXU_DIM // LRU_BW
    nt = LRU_BLOCKS // per
    w4 = w.reshape(nt, per, LRU_BW, LRU_BW)
    eye = jnp.eye(per, dtype=w.dtype)
    t = jnp.einsum('tpcd,pq->tpcqd', w4, eye)
    return t.reshape(nt, V7X_MXU_DIM, V7X_MXU_DIM).astype(BF16)


def _lru(xc, wa, wx, ba, bx, lam, n_ctx, reverse):
    bsz, s, w = xc.shape
    tm = ROW_TILE
    nchunks = s // tm
    order = _chunk_order(n_ctx // tm, nchunks)
    d = 1 if reverse else 0
    row = lambda b, j: (b, order(d, j), 0)
    ntile = w // V7X_MXU_DIM
    full3 = pl.BlockSpec((ntile, V7X_MXU_DIM, V7X_MXU_DIM), lambda b, j: (0, 0, 0))
    vec = pl.BlockSpec((1, w), lambda b, j: (0, 0))
    return pl.pallas_call(
        functools.partial(_lru_kernel, reverse),
        grid=(bsz, nchunks),
        in_specs=[pl.BlockSpec((1, tm, w), row), full3, full3, vec, vec, vec],
        out_specs=pl.BlockSpec((1, tm, w), row),
        out_shape=jax.ShapeDtypeStruct((bsz, s, w), BF16),
        scratch_shapes=[pltpu.VMEM((tm, w), F32), pltpu.VMEM((tm, w), F32), pltpu.VMEM((1, w), F32)],
        compiler_params=_cparams(("arbitrary", "arbitrary"), 32),
        name="lru_bwd" if reverse else "lru_fwd",
    )(xc, _lru_blockdiag(wa), _lru_blockdiag(wx), ba.reshape(1, w), bx.reshape(1, w), lam.reshape(1, w))


def _route(lt, first, route_ref, cnt_ref):
    ne, tm = lt.shape
    erow = lax.broadcasted_iota(jnp.int32, (ne, tm), 0).astype(F32)
    lg = lt
    tops, hots = [], []
    for _ in range(TOP_K):
        m = jnp.max(lg, axis=0, keepdims=True)
        idx = jnp.min(jnp.where(lg == m, erow, float(ne)), axis=0, keepdims=True)
        hot = erow == idx
        lg = jnp.where(hot, -jnp.inf, lg)
        tops.append((m, idx))
        hots.append(hot)
    es = [jnp.exp(m - tops[0][0]) for m, _ in tops]
    denom = es[0]
    for e in es[1:]:
        denom = denom + e

    @pl.when(first)
    def _():
        cnt_ref[...] = jnp.zeros_like(cnt_ref)

    chosen = hots[0]
    for hot in hots[1:]:
        chosen = jnp.logical_or(chosen, hot)
    chosen_f = jnp.where(chosen, 1.0, 0.0)
    row = lax.broadcasted_iota(jnp.int32, (tm, tm), 0)
    col = lax.broadcasted_iota(jnp.int32, (tm, tm), 1)
    before = jnp.where(row < col, 1.0, 0.0).astype(BF16)
    ranks = _dot(chosen_f.astype(BF16), before) + cnt_ref[:, 0:1]
    cnt_ref[...] = cnt_ref[...] + jnp.sum(chosen_f, axis=1, keepdims=True)
    srow = lax.broadcasted_iota(jnp.int32, route_ref.shape, 0)
    out = jnp.zeros(route_ref.shape, F32)
    for kk in range(TOP_K):
        rank = jnp.sum(jnp.where(hots[kk], ranks, 0.0), axis=0, keepdims=True)
        out = jnp.where(srow == kk, tops[kk][1], out)
        out = jnp.where(srow == TOP_K + kk, rank, out)
        out = jnp.where(srow == 2 * TOP_K + kk, es[kk] / denom, out)
    route_ref[...] = out


def _tail(y, x_ref, mod, n2_ref, wrt_ref, br_ref, xo_ref, v_ref, route_ref, cnt_ref):
    xn = x_ref[0] + mod[2:3] * y
    xo_ref[0] = xn
    v = _modnorm(xn, n2_ref[...], mod[4:5], mod[3:4])
    v_ref[0] = v.astype(v_ref.dtype)
    logits_t = _dot_nt(wrt_ref[...], v, precision=HIGHEST) + br_ref[...]
    first = jnp.logical_and(pl.program_id(0) == 0, pl.program_id(1) == 0)
    _route(logits_t, first, route_ref, cnt_ref)


def _tail_specs(bsz, nt, tm, d, rows_per_sample):
    full = lambda shape: pl.BlockSpec(shape, lambda b, i: (0,) * len(shape))
    row = lambda b, i: (b, i, 0)
    in_specs = [full((N_EXPERTS, d)), full((N_EXPERTS, 1))]
    out_specs = [pl.BlockSpec((1, tm, d), row), pl.BlockSpec((1, tm, d), row),
                 pl.BlockSpec((ROUTE_ROWS, tm), lambda b, i: (0, b * nt + i)),
                 full((N_EXPERTS, V7X_LANES))]
    out_shape = [jax.ShapeDtypeStruct((bsz, rows_per_sample, d), F32),
                 jax.ShapeDtypeStruct((bsz, rows_per_sample, d), BF16),
                 jax.ShapeDtypeStruct((ROUTE_ROWS, bsz * rows_per_sample), F32),
                 jax.ShapeDtypeStruct((N_EXPERTS, V7X_LANES), F32)]
    return in_specs, out_specs, out_shape


def _even_out_kernel(hm_ref, hl0_ref, hl1_ref, o_ref, yg_ref, mg_ref, wout_ref,
                     x_ref, mod_ref, n2_ref, wr_ref, br_ref, xo_ref, v_ref, route_ref, cnt_ref):
    hm = hm_ref[0, 0].astype(F32) + hm_ref[1, 0].astype(F32)
    parts = []
    for h in range(MLSTM_HEADS):
        sl = slice(h * MLSTM_DH, (h + 1) * MLSTM_DH)
        parts.append(_rms(hm[:, sl], mg_ref[:, sl]))
    hmn = jnp.concatenate(parts, axis=1) * _sigmoid(o_ref[0].astype(F32))
    hl = (hl0_ref[0].astype(F32) + hl1_ref[0].astype(F32)) * _gelu_tanh(yg_ref[0].astype(F32))
    y = (_dot(hmn.astype(BF16), wout_ref[0:MLSTM_W, :])
         + _dot(hl.astype(BF16), wout_ref[MLSTM_W:MLSTM_W + LRU_W, :]))
    _tail(y, x_ref, mod_ref[0, 0], n2_ref, wr_ref, br_ref, xo_ref, v_ref, route_ref, cnt_ref)


def _even_out(hm, hl0, hl1, o_pre, yg, mnorm_g, w_out, x, mod, n2, w_r, b_r, n_ctx):
    bsz, s, d = x.shape
    tm = ROW_TILE
    nt = s // tm
    nct = n_ctx // tm
    row = lambda b, i: (b, i, 0)
    full = lambda shape: pl.BlockSpec(shape, lambda b, i: (0,) * len(shape))
    tail_in, out_specs, out_shape = _tail_specs(bsz, nt, tm, d, s)
    return pl.pallas_call(
        _even_out_kernel,
        grid=(bsz, nt),
        in_specs=[pl.BlockSpec((2, 1, tm, MLSTM_W), lambda b, i: (0, b, i, 0)),
                  pl.BlockSpec((1, tm, LRU_W), row), pl.BlockSpec((1, tm, LRU_W), row),
                  pl.BlockSpec((1, tm, MLSTM_W), row), pl.BlockSpec((1, tm, LRU_W), row),
                  full((1, MLSTM_W)), full((MLSTM_W + LRU_W, d)),
                  pl.BlockSpec((1, tm, d), row),
                  pl.BlockSpec((1, 1, 8, d), lambda b, i: (b, (i >= nct).astype(jnp.int32), 0, 0)),
                  full((1, d))] + tail_in,
        out_specs=out_specs,
        out_shape=out_shape,
        compiler_params=_cparams(("arbitrary", "arbitrary"), 48),
        name="even_out",
    )(hm, hl0, hl1, o_pre, yg, mnorm_g.reshape(1, MLSTM_W), w_out, x, mod, n2.reshape(1, d),
      w_r.T, b_r.reshape(N_EXPERTS, 1))


def _odd_out_kernel(a_ref, wout_ref, x_ref, mod_ref, n2_ref, wr_ref, br_ref, xo_ref, v_ref, route_ref, cnt_ref):
    y = _dot(a_ref[0], wout_ref[...])
    _tail(y, x_ref, mod_ref[0, 0], n2_ref, wr_ref, br_ref, xo_ref, v_ref, route_ref, cnt_ref)


def _odd_out(attn, w_out, x, mod, n2, w_r, b_r, n_ctx):
    bsz, seq, d = attn.shape
    tm = ROW_TILE
    nt = seq // tm
    nct = n_ctx // tm
    row = lambda b, i: (b, i, 0)
    full = lambda shape: pl.BlockSpec(shape, lambda b, i: (0,) * len(shape))
    tail_in, out_specs, out_shape = _tail_specs(bsz, nt, tm, d, seq)
    return pl.pallas_call(
        _odd_out_kernel,
        grid=(bsz, nt),
        in_specs=[pl.BlockSpec((1, tm, d), row), full((d, d)),
                  pl.BlockSpec((1, tm, d), lambda b, i: (b, i + nct, 0)),
                  pl.BlockSpec((1, 1, 8, d), lambda b, i: (b, 1, 0, 0)),
                  full((1, d))] + tail_in,
        out_specs=out_specs,
        out_shape=out_shape,
        compiler_params=_cparams(("arbitrary", "arbitrary"), 32),
        name="odd_out",
    )(attn, w_out, x, mod, n2.reshape(1, d), w_r.T, b_r.reshape(N_EXPERTS, 1))


def _expert_kernel(te_ref, nu_ref, x_ref, w1_ref, b1_ref, w2_ref, b2_ref, rw_ref, *rest):
    o_ref, w1b, w2b = rest[-3:]
    t = pl.program_id(0)
    tm = x_ref.shape[0]
    used = t < nu_ref[0]
    new_expert = jnp.logical_or(t == 0, te_ref[t] != te_ref[jnp.maximum(t - 1, 0)])

    @pl.when(jnp.logical_and(used, new_expert))
    def _():
        w1b[...] = w1_ref[0, 0].astype(BF16)
        w2b[...] = w2_ref[0, 0].astype(BF16)

    @pl.when(used)
    def _():
        hid = _dot(x_ref[...], w1b[...]) + b1_ref[0, 0]
        gate = jnp.minimum(hid[:, :D_FF], SWIGLU_LIMIT)
        up = jnp.clip(hid[:, D_FF:], -SWIGLU_LIMIT, SWIGLU_LIMIT)
        act = (up + 1.0) * gate * _sigmoid(SWIGLU_ALPHA * gate)
        y = _dot(act.astype(BF16), w2b[...]) + b2_ref[0, 0]
        w_col = jnp.transpose(jnp.broadcast_to(rw_ref[0], (V7X_LANES, tm)))[:, 0:1]
        o_ref[...] = (y * w_col).astype(o_ref.dtype)

    @pl.when(jnp.logical_not(used))
    def _():
        o_ref[...] = jnp.zeros_like(o_ref)


def _experts(xs, row_w, tile_expert, n_used, layer, w1, b1, w2, b2, chunk, ys_buf):
    rows, d = xs.shape
    nt, _, tm = row_w.shape
    nl, ne, _, ff2 = w1.shape
    in_specs = [pl.BlockSpec((tm, d), lambda t, te, nu: (t, 0)),
                pl.BlockSpec((1, 1, d, ff2), lambda t, te, nu: (layer, te[t], 0, 0)),
                pl.BlockSpec((1, 1, 1, ff2), lambda t, te, nu: (layer, te[t], 0, 0)),
                pl.BlockSpec((1, 1, ff2 // 2, d), lambda t, te, nu: (layer, te[t], 0, 0)),
                pl.BlockSpec((1, 1, 1, d), lambda t, te, nu: (layer, te[t], 0, 0)),
                pl.BlockSpec((1, 1, tm), lambda t, te, nu: (t, 0, 0))]
    args = [tile_expert, n_used, xs, w1, b1.reshape(nl, ne, 1, ff2), w2, b2.reshape(nl, ne, 1, d), row_w]
    aliases = {}
    if ys_buf is not None:
        in_specs.append(pl.BlockSpec(memory_space=pl.ANY))
        aliases = {len(args): 0}
        args.append(ys_buf)
    return pl.pallas_call(
        _expert_kernel,
        grid_spec=pltpu.PrefetchScalarGridSpec(
            num_scalar_prefetch=2,
            grid=(nt,),
            in_specs=in_specs,
            out_specs=pl.BlockSpec((tm, d), lambda t, te, nu: (chunk * nt + t, 0)),
            scratch_shapes=[pltpu.VMEM((d, ff2), BF16), pltpu.VMEM((ff2 // 2, d), BF16)],
        ),
        out_shape=jax.ShapeDtypeStruct((MOE_CHUNKS * rows, d), BF16),
        input_output_aliases=aliases,
        compiler_params=_cparams(("arbitrary",), 56),
        name="moe_experts",
    )(*args)


def _moe(v, route, counts, layer, w1, b1, w2, b2):
    t, d = v.shape
    tm = MOE_TILE
    nrows = t * TOP_K
    ntc = -(-(-(-nrows // tm) + N_EXPERTS) // MOE_CHUNKS)
    nt = ntc * MOE_CHUNKS
    idx = route[0:TOP_K].astype(jnp.int32)
    rank = route[TOP_K:2 * TOP_K].astype(jnp.int32)
    weight = route[2 * TOP_K:3 * TOP_K]
    sizes = counts[:, 0].astype(jnp.int32)
    start = jnp.cumsum(sizes) - sizes
    padded = (sizes + tm - 1) // tm * tm
    pad_end = jnp.cumsum(padded)
    pad_start = pad_end - padded
    pair_pos = rank
    for e in range(N_EXPERTS):
        pair_pos = pair_pos + jnp.where(idx == e, pad_start[e], 0)
    bits = max(1, (nrows - 1).bit_length())
    assert N_EXPERTS << bits < 2 ** 31
    pair_id = (jnp.arange(t, dtype=jnp.int32)[None, :] * TOP_K + jnp.arange(TOP_K, dtype=jnp.int32)[:, None])
    order = jnp.sort(((idx << bits) + pair_id).reshape(-1)) & ((1 << bits) - 1)
    tile_row0 = jnp.arange(nt, dtype=jnp.int32) * tm
    tile_expert = jnp.minimum(jnp.sum(pad_end[None, :] <= tile_row0[:, None], axis=1, dtype=jnp.int32),
                              N_EXPERTS - 1)
    n_used = (pad_end[-1] // tm).reshape(1).astype(jnp.int32)
    onehot_te = tile_expert[:, None] == jnp.arange(N_EXPERTS, dtype=jnp.int32)[None, :]
    pick = lambda tbl: jnp.sum(jnp.where(onehot_te, tbl[None, :], 0), axis=1)
    r_in = tile_row0[:, None] + jnp.arange(tm, dtype=jnp.int32)[None, :] - pick(pad_start)[:, None]
    valid = r_in < pick(sizes)[:, None]
    src = jnp.where(valid, pick(start)[:, None] + r_in, 0).reshape(-1)
    pair = order.at[src].get(mode='promise_in_bounds')
    row_token = pair // TOP_K
    flat_w = weight.reshape(-1).at[(pair % TOP_K) * t + row_token].get(mode='promise_in_bounds')
    row_w = jnp.where(valid, flat_w.reshape(nt, tm), 0.0)
    row_token = row_token.reshape(MOE_CHUNKS, ntc * tm)
    row_w = row_w.reshape(MOE_CHUNKS, ntc, 1, tm)
    tile_expert = tile_expert.reshape(MOE_CHUNKS, ntc)
    ys = None
    for c in range(MOE_CHUNKS):
        xs = v.at[row_token[c]].get(mode='promise_in_bounds')
        ys = _experts(xs, row_w[c], tile_expert[c], jnp.clip(n_used - c * ntc, 0, ntc), layer,
                      w1, b1, w2, b2, c, ys)
    return [ys.at[pair_pos[kk]].get(mode='promise_in_bounds') for kk in range(TOP_K)]


def _rope(t, cos, sin, lane_lo):
    swapped = jnp.where(lane_lo, pltpu.roll(t, ATT_DH - ROPE_AXIS_DIM // 2, 1),
                        pltpu.roll(t, ROPE_AXIS_DIM // 2, 1))
    return t * cos + swapped * sin


def _proj_odd_kernel(x_ref, f0, f1, f2, f3, mod0_ref, mod_ref, g_ref, w_ref, qg_ref, kg_ref,
                     cos_ref, sin_ref, h_ref, q_ref, k_ref, v_ref):
    f = (f0[0].astype(F32) + f1[0].astype(F32)) + (f2[0].astype(F32) + f3[0].astype(F32))
    hcur = x_ref[0] + mod0_ref[0, 0][5:6] * f
    h_ref[0] = hcur
    mod = mod_ref[0, 0]
    u = _modnorm(hcur, g_ref[...], mod[1:2], mod[0:1]).astype(BF16)
    z = _dot(u, w_ref[...])
    cos = cos_ref[...]
    sin = sin_ref[...]
    lane = lax.broadcasted_iota(jnp.int32, cos.shape, 1)
    lane_lo = (lane % ROPE_AXIS_DIM) < (ROPE_AXIS_DIM // 2)
    qw = ATT_HEADS * ATT_DH
    kw = ATT_KV_HEADS * ATT_DH
    for hh in range(ATT_HEADS):
        sl = slice(hh * ATT_DH, (hh + 1) * ATT_DH)
        t = _rope(_rms(z[:, sl], qg_ref[...]), cos, sin, lane_lo)
        q_ref[0, :, sl] = (t * (ATT_DH ** -0.5 * LOG2_E)).astype(q_ref.dtype)
    for hh in range(ATT_KV_HEADS):
        sl = slice(hh * ATT_DH, (hh + 1) * ATT_DH)
        t = _rope(_rms(z[:, qw + hh * ATT_DH:qw + (hh + 1) * ATT_DH], kg_ref[...]), cos, sin, lane_lo)
        k_ref[0, :, sl] = t.astype(k_ref.dtype)
    v_ref[0] = z[:, qw + kw:qw + 2 * kw].astype(v_ref.dtype)


def _proj_odd(x, fparts, mod0, mod, g, w, qg, kg, cos_tab, sin_tab, n_ctx):
    bsz, s, d = x.shape
    tm = ROW_TILE
    nt = s // tm
    nct = n_ctx // tm
    n = w.shape[1]
    qw = ATT_HEADS * ATT_DH
    kw = ATT_KV_HEADS * ATT_DH
    row = lambda b, i: (b, i, 0)
    seg = lambda b, i: (b, (i >= nct).astype(jnp.int32), 0, 0)
    full = lambda shape: pl.BlockSpec(shape, lambda b, i: (0,) * len(shape))
    tab = pl.BlockSpec((tm, ATT_DH), lambda b, i: (i, 0))
    return pl.pallas_call(
        _proj_odd_kernel,
        grid=(bsz, nt),
        in_specs=[pl.BlockSpec((1, tm, d), row)] + [pl.BlockSpec((1, tm, d), row)] * TOP_K
                 + [pl.BlockSpec((1, 1, 8, d), seg), pl.BlockSpec((1, 1, 8, d), seg),
                    full((1, d)), full((d, n)), full((1, ATT_DH)), full((1, ATT_DH)), tab, tab],
        out_specs=[pl.BlockSpec((1, tm, d), row), pl.BlockSpec((1, tm, qw), row),
                   pl.BlockSpec((1, tm, kw), row), pl.BlockSpec((1, tm, kw), row)],
        out_shape=[jax.ShapeDtypeStruct((bsz, s, d), F32),
                   jax.ShapeDtypeStruct((bsz, s, qw), BF16),
                   jax.ShapeDtypeStruct((bsz, s, kw), BF16),
                   jax.ShapeDtypeStruct((bsz, s, kw), BF16)],
        compiler_params=_cparams(("arbitrary", "arbitrary"), 48),
        name="proj_odd",
    )(x, *fparts, mod0, mod, g.reshape(1, d), w, qg.reshape(1, ATT_DH), kg.reshape(1, ATT_DH),
      cos_tab, sin_tab)


def _rope_tables(n_ctx, seq):
    rows = seq // GRID_W
    pos_r = jnp.repeat(jnp.arange(rows), GRID_W).astype(F32)
    pos_c = jnp.tile(jnp.arange(GRID_W), rows).astype(F32)
    inv_freq = ROPE_THETA ** (-jnp.arange(0, ROPE_AXIS_DIM, 2, dtype=F32) / ROPE_AXIS_DIM)
    ar = pos_r[:, None] * inv_freq
    ac = pos_c[:, None] * inv_freq
    cos = jnp.concatenate([jnp.cos(ar), jnp.cos(ar), jnp.cos(ac), jnp.cos(ac)], axis=-1)
    sin = jnp.concatenate([-jnp.sin(ar), jnp.sin(ar), -jnp.sin(ac), jnp.sin(ac)], axis=-1)
    cos = jnp.concatenate([jnp.ones((n_ctx, ATT_DH), F32), cos], axis=0)
    sin = jnp.concatenate([jnp.zeros((n_ctx, ATT_DH), F32), sin], axis=0)
    return cos, sin


def _attn_kernel(q_ref, k_ref, v_ref, o_ref):
    k = k_ref[0]
    v = v_ref[0]
    for g in range(ATT_GROUP):
        sl = slice(g * ATT_DH, (g + 1) * ATT_DH)
        s = _dot_nt(q_ref[0, :, sl], k)
        p = jnp.exp2(s - jnp.max(s, axis=1, keepdims=True))
        l = jnp.sum(p, axis=1, keepdims=True)
        o_ref[0, :, sl] = (_dot(p.astype(BF16), v) / l).astype(o_ref.dtype)


def _attention(q, k, v, n_ctx):
    bsz, s, qw = q.shape
    seq = s - n_ctx
    tq = ATT_Q_TILE
    nct = n_ctx // tq
    gw = ATT_GROUP * ATT_DH
    return pl.pallas_call(
        _attn_kernel,
        grid=(bsz, ATT_KV_HEADS, seq // tq),
        in_specs=[pl.BlockSpec((1, tq, gw), lambda b, h, i: (b, i + nct, h)),
                  pl.BlockSpec((1, s, ATT_DH), lambda b, h, i: (b, 0, h)),
                  pl.BlockSpec((1, s, ATT_DH), lambda b, h, i: (b, 0, h))],
        out_specs=pl.BlockSpec((1, tq, gw), lambda b, h, i: (b, i, h)),
        out_shape=jax.ShapeDtypeStruct((bsz, seq, qw), BF16),
        compiler_params=_cparams(("arbitrary", "arbitrary", "arbitrary"), 48),
        name="attention",
    )(q, k, v)


def _final_kernel(x_ref, f0, f1, f2, f3, mod_ref, g_ref, o_ref):
    f = (f0[0].astype(F32) + f1[0].astype(F32)) + (f2[0].astype(F32) + f3[0].astype(F32))
    o_ref[0] = _rms(x_ref[0] + mod_ref[0, 0][5:6] * f, g_ref[...])


def _final(x, fparts, mod, g):
    bsz, seq, d = x.shape
    tm = ROW_TILE
    row = lambda b, i: (b, i, 0)
    return pl.pallas_call(
        _final_kernel,
        grid=(bsz, seq // tm),
        in_specs=[pl.BlockSpec((1, tm, d), row)] * (1 + TOP_K)
                 + [pl.BlockSpec((1, 1, 8, d), lambda b, i: (b, 1, 0, 0)),
                    pl.BlockSpec((1, d), lambda b, i: (0, 0))],
        out_specs=pl.BlockSpec((1, tm, d), row),
        out_shape=jax.ShapeDtypeStruct((bsz, seq, d), F32),
        compiler_params=_cparams(("arbitrary", "arbitrary"), 32),
        name="final_norm",
    )(x, *fparts, mod, g.reshape(1, d))


def _pack_even_w_in(w_in):
    w4 = 4 * MLSTM_W
    ng = 4 * MLSTM_HEADS
    wg = w_in[:, w4:w4 + ng]
    half = ng // 2
    pad = jnp.zeros((w_in.shape[0], GATE_PAD - half), w_in.dtype)
    packed = jnp.concatenate([w_in[:, :w4], w_in[:, w4 + ng:], wg[:, :half], pad, wg[:, half:], pad], axis=1)
    return packed.astype(BF16), wg.T.astype(BF16)


def kernel(x, c, ctx, c_ctx, mod_w, mod_b, norm1_g, norm2_g, final_g, ev_w_in, ev_qk_conv_w, ev_qk_conv_b, ev_gate_b, ev_mnorm_g, ev_lru_conv_w, ev_lru_conv_b, ev_lru_wa, ev_lru_ba, ev_lru_wx, ev_lru_bx, ev_lru_lam, ev_w_out, od_w_in, od_q_norm_g, od_k_norm_g, od_w_out, moe_w_r, moe_b_r, moe_w1, moe_b1, moe_w2, moe_b2):
    bsz, seq, d = x.shape
    n_ctx = ctx.shape[1]
    s = n_ctx + seq
    assert n_ctx % ROW_TILE == 0 and seq % ROW_TILE == 0 and seq % GRID_W == 0
    h = jnp.concatenate([ctx, x], axis=1)

    mod0 = _mod_table(c, c_ctx, mod_w[0], mod_b[0])
    w_packed, wg_t = _pack_even_w_in(ev_w_in[0])
    q, k, v, o_pre, xc, yg, gc, gr = _proj_even(h, mod0, norm1_g[0], w_packed, wg_t, ev_qk_conv_w[0], ev_qk_conv_b[0],
                                                ev_lru_conv_w[0], ev_lru_conv_b[0], n_ctx)
    hm = _mlstm(q, k, v, gc, gr, ev_gate_b[0], n_ctx)
    hl = [_lru(xc, ev_lru_wa[0, dd], ev_lru_wx[0, dd], ev_lru_ba[0, dd], ev_lru_bx[0, dd],
               ev_lru_lam[0, dd], n_ctx, dd == 1) for dd in range(2)]
    x_mid, v0, route0, cnt0 = _even_out(hm, hl[0], hl[1], o_pre, yg, ev_mnorm_g[0], ev_w_out[0].astype(BF16),
                                        h, mod0, norm2_g[0], moe_w_r[0], moe_b_r[0], n_ctx)
    f0 = _moe(v0.reshape(bsz * s, d), route0, cnt0, 0, moe_w1, moe_b1, moe_w2, moe_b2)
    f0 = [p.reshape(bsz, s, d) for p in f0]

    mod1 = _mod_table(c, c_ctx, mod_w[1], mod_b[1])
    cos_tab, sin_tab = _rope_tables(n_ctx, seq)
    h1, q1, k1, v1 = _proj_odd(x_mid, f0, mod0, mod1, norm1_g[1], od_w_in[0].astype(BF16),
                               od_q_norm_g[0], od_k_norm_g[0], cos_tab, sin_tab, n_ctx)
    attn = _attention(q1, k1, v1, n_ctx)
    x2, v2, route2, cnt2 = _odd_out(attn, od_w_out[0].astype(BF16), h1, mod1, norm2_g[1],
                                    moe_w_r[1], moe_b_r[1], n_ctx)
    f1 = _moe(v2.reshape(bsz * seq, d), route2, cnt2, 1, moe_w1, moe_b1, moe_w2, moe_b2)
    f1 = [p.reshape(bsz, seq, d) for p in f1]
    return _final(x2, f1, mod1, final_g)
```

```python
import functools

import jax
import jax.numpy as jnp
from jax import lax
from jax.experimental import pallas as pl
from jax.experimental.pallas import tpu as pltpu

F32 = jnp.float32
BF16 = jnp.bfloat16
HIGHEST = lax.Precision.HIGHEST

EPS = 1e-6
M_INIT = -1e30
NEG_BIG = -1e30

MLSTM_HEADS = 4
MLSTM_DH = 256
MLSTM_W = MLSTM_HEADS * MLSTM_DH
LRU_W = 1024
LRU_BLOCKS = 16
LRU_BW = LRU_W // LRU_BLOCKS
LRU_C = 8.0
CONV_W = 4
CONV_LEFT = 2
ATT_HEADS = 8
ATT_KV_HEADS = 2
ATT_GROUP = ATT_HEADS // ATT_KV_HEADS
ATT_DH = 128
GRID_W = 64
ROPE_AXIS_DIM = ATT_DH // 2
ROPE_THETA = 10000.0
N_EXPERTS = 32
TOP_K = 4
D_FF = 1024
SWIGLU_ALPHA = 1.702
SWIGLU_LIMIT = 7.0
LOG2_E = 1.4426950408889634

V7X_LANES = 128
V7X_MXU_DIM = 256
V7X_VMEM_BYTES = 64 * 1024 * 1024
MIB = 1024 * 1024

ROW_TILE = 256
HALO = 16
LRU_CHUNK = 128
TAIL_SUB = 1
MOE_TILE = 256
MOE_CHUNKS = 8
ROUTE_ROWS = 16
ATT_Q_TILE = 256
GATE_PAD = V7X_LANES


def _cparams(semantics, vmem_mib):
    assert vmem_mib * MIB < V7X_VMEM_BYTES
    return pltpu.CompilerParams(dimension_semantics=semantics, vmem_limit_bytes=vmem_mib * MIB)


def _dot(a, b):
    return jnp.dot(a, b, preferred_element_type=F32)


def _dot_nt(a, b, precision=None):
    return lax.dot_general(a, b, (((1,), (1,)), ((), ())), precision=precision,
                           preferred_element_type=F32)


def _dot_tn(a, b):
    return lax.dot_general(a, b, (((0,), (0,)), ((), ())), preferred_element_type=F32)


def _sigmoid(x):
    return 0.5 * jnp.tanh(0.5 * x) + 0.5


def _log_sigmoid(x):
    return jnp.minimum(x, 0.0) - jnp.log1p(jnp.exp(-jnp.abs(x)))


def _softplus(x):
    return jnp.maximum(x, 0.0) + jnp.log1p(jnp.exp(-jnp.abs(x)))


def _gelu_tanh(x):
    return 0.5 * x * (1.0 + jnp.tanh(0.7978845608028654 * (x + 0.044715 * x * x * x)))


def _rms(x, g):
    return x * lax.rsqrt(jnp.mean(x * x, axis=-1, keepdims=True) + EPS) * g


def _modnorm(x, g, scale, shift):
    return _rms(x, g) * (1.0 + scale) + shift


def _modvec_kernel(c_ref, w_ref, b_ref, o_ref):
    c = c_ref[...]
    s = c * _sigmoid(c)
    o_ref[...] = jnp.dot(s, w_ref[...], precision=HIGHEST, preferred_element_type=F32) + b_ref[...]


def _modvec(cc, w, b):
    rows, d = cc.shape
    n = w.shape[1]
    tn = 1536
    return pl.pallas_call(
        _modvec_kernel,
        grid=(n // tn,),
        in_specs=[pl.BlockSpec((rows, d), lambda j: (0, 0)),
                  pl.BlockSpec((d, tn), lambda j: (0, j)),
                  pl.BlockSpec((1, tn), lambda j: (0, j))],
        out_specs=pl.BlockSpec((rows, tn), lambda j: (0, j)),
        out_shape=jax.ShapeDtypeStruct((rows, n), F32),
        compiler_params=_cparams(("arbitrary",), 32),
        name="modvec",
    )(cc, w, b.reshape(1, n))


def _mod_table(c, c_ctx, mod_w, mod_b):
    bsz, d = c.shape
    rows = ((bsz + 1 + 7) // 8) * 8
    cc = jnp.zeros((rows, d), F32).at[:bsz].set(c).at[bsz].set(c_ctx)
    mod = _modvec(cc, mod_w, mod_b)
    lat = mod[:bsz].reshape(bsz, 6, d)
    ctx = jnp.broadcast_to(mod[bsz].reshape(1, 6, d), (bsz, 6, d))
    tbl = jnp.stack([ctx, lat], axis=1)
    return jnp.pad(tbl, ((0, 0), (0, 0), (0, 2), (0, 0)))


def _proj_even_kernel(nct, x_ref, xp_ref, xn_ref, mod_ref, g_ref, w_ref, wgt_ref,
                      wqk_ref, bqk_ref, wxr_ref, bxr_ref,
                      q_ref, k_ref, v_ref, o_ref, xc_ref, yg_ref, gc_ref, gr_ref, ext_qk, ext_xr):
    i = pl.program_id(1)
    nt = pl.num_programs(1)
    tm = x_ref.shape[1]
    w = MLSTM_W
    first = jnp.logical_or(i == 0, i == nct)
    last = jnp.logical_or(i == nct - 1, i == nt - 1)
    mod = mod_ref[0, 0]
    xe = jnp.concatenate([xp_ref[0], x_ref[0], xn_ref[0]], axis=0)
    ue = _modnorm(xe, g_ref[...], mod[1:2], mod[0:1]).astype(BF16)
    rowi = lax.broadcasted_iota(jnp.int32, (tm + 2 * HALO, 1), 0)
    keep = jnp.where(rowi < HALO, jnp.where(first, 0.0, 1.0),
                     jnp.where(rowi >= HALO + tm, jnp.where(last, 0.0, 1.0), 1.0))

    def conv(z, ext, w_ref, b_ref):
        ext[...] = z * keep
        acc = b_ref[...] + w_ref[0:1, :] * ext[pl.ds(HALO - CONV_LEFT, tm), :]
        for j in range(1, CONV_W):
            acc = acc + w_ref[j:j + 1, :] * ext[pl.ds(HALO - CONV_LEFT + j, tm), :]
        return acc

    y = conv(_dot(ue, w_ref[:, 0:2 * w]), ext_qk, wqk_ref, bqk_ref)
    y = y * _sigmoid(y)
    q_ref[0] = y[:, :w].astype(q_ref.dtype)
    k_ref[0] = (y[:, w:] * (MLSTM_DH ** -0.5)).astype(k_ref.dtype)
    xc_ref[0] = conv(_dot(ue, w_ref[:, 4 * w:4 * w + LRU_W]), ext_xr, wxr_ref, bxr_ref)
    u = ue[HALO:HALO + tm]
    v_ref[0] = _dot(u, w_ref[:, 2 * w:3 * w]).astype(v_ref.dtype)
    o_ref[0] = _dot(u, w_ref[:, 3 * w:4 * w]).astype(o_ref.dtype)
    yg_ref[0] = _dot(u, w_ref[:, 4 * w + LRU_W:4 * w + 2 * LRU_W]).astype(yg_ref.dtype)
    gc_ref[0] = _dot(u, w_ref[:, 4 * w + 2 * LRU_W:4 * w + 2 * LRU_W + 2 * GATE_PAD])
    gr_ref[0] = _dot_nt(wgt_ref[...], u)


def _proj_even(h, mod, g, w_packed, wg_t, wqk, bqk, wxr, bxr, n_ctx):
    bsz, s, d = h.shape
    tm = ROW_TILE
    nt = s // tm
    nct = n_ctx // tm
    ntot = w_packed.shape[1]
    ng = wg_t.shape[0]
    hb = tm // HALO
    nhb = s // HALO
    cq = 2 * MLSTM_W
    row = lambda b, i: (b, i, 0)
    prev = lambda b, i: (b, jnp.maximum(i * hb - 1, 0), 0)
    nxt = lambda b, i: (b, jnp.minimum((i + 1) * hb, nhb - 1), 0)
    full = lambda shape: pl.BlockSpec(shape, lambda b, i: (0, 0))
    return pl.pallas_call(
        functools.partial(_proj_even_kernel, nct),
        grid=(bsz, nt),
        in_specs=[pl.BlockSpec((1, tm, d), row), pl.BlockSpec((1, HALO, d), prev),
                  pl.BlockSpec((1, HALO, d), nxt),
                  pl.BlockSpec((1, 1, 8, d), lambda b, i: (b, (i >= nct).astype(jnp.int32), 0, 0)),
                  full((1, d)), full((d, ntot)), full((ng, d)),
                  full((CONV_W, cq)), full((1, cq)), full((CONV_W, LRU_W)), full((1, LRU_W))],
        out_specs=[pl.BlockSpec((1, tm, MLSTM_W), row),
                   pl.BlockSpec((1, tm, MLSTM_W), row),
                   pl.BlockSpec((1, tm, MLSTM_W), row),
                   pl.BlockSpec((1, tm, MLSTM_W), row),
                   pl.BlockSpec((1, tm, LRU_W), row),
                   pl.BlockSpec((1, tm, LRU_W), row),
                   pl.BlockSpec((1, tm, 2 * GATE_PAD), row),
                   pl.BlockSpec((1, ng, tm), lambda b, i: (b, 0, i))],
        out_shape=[jax.ShapeDtypeStruct((bsz, s, MLSTM_W), BF16),
                   jax.ShapeDtypeStruct((bsz, s, MLSTM_W), BF16),
                   jax.ShapeDtypeStruct((bsz, s, MLSTM_W), BF16),
                   jax.ShapeDtypeStruct((bsz, s, MLSTM_W), BF16),
                   jax.ShapeDtypeStruct((bsz, s, LRU_W), F32),
                   jax.ShapeDtypeStruct((bsz, s, LRU_W), BF16),
                   jax.ShapeDtypeStruct((bsz, s, 2 * GATE_PAD), F32),
                   jax.ShapeDtypeStruct((bsz, ng, s), F32)],
        scratch_shapes=[pltpu.VMEM((tm + 2 * HALO, cq), F32),
                        pltpu.VMEM((tm + 2 * HALO, LRU_W), F32)],
        compiler_params=_cparams(("arbitrary", "arbitrary"), 56),
        name="proj_even",
    )(h, h, h, mod, g.reshape(1, d), w_packed, wg_t, wqk, bqk.reshape(1, cq), wxr, bxr.reshape(1, LRU_W))


def _mlstm_kernel(q_ref, k_ref, v_ref, gc_ref, gr_ref, bc_ref, br_ref, h_ref, c_scr, n_scr, m_scr):
    d = pl.program_id(1)
    j = pl.program_id(2)
    lc = q_ref.shape[1]
    nh = MLSTM_HEADS
    dh = MLSTM_DH

    @pl.when(j == 0)
    def _():
        c_scr[...] = jnp.zeros_like(c_scr)
        n_scr[...] = jnp.zeros_like(n_scr)
        m_scr[...] = jnp.full(m_scr.shape, M_INIT, F32)

    row = lax.broadcasted_iota(jnp.int32, (lc, lc), 0)
    col = lax.broadcasted_iota(jnp.int32, (lc, lc), 1)
    lo = jnp.where(d == 1, row, col)
    hi = jnp.where(d == 1, col, row)
    tri = lo <= hi
    trif = tri.astype(F32)

    gc = gc_ref[0] + bc_ref[0]
    gr = gr_ref[0, 0] + br_ref[0]
    lfr = _log_sigmoid(gr)
    bcum_r = _dot_nt(lfr, trif, precision=HIGHEST)
    bcum_c = jnp.transpose(jnp.concatenate([bcum_r, jnp.zeros((GATE_PAD - 2 * nh, lc), F32)], axis=0))

    for h in range(nh):
        sl = slice(h * dh, (h + 1) * dh)
        q = q_ref[0, :, sl]
        k = k_ref[0, :, sl]
        v = v_ref[0, :, sl]
        i_col = gc[:, h:h + 1]
        b_col = bcum_c[:, nh + h:nh + h + 1]
        i_row = gr[h:h + 1, :]
        b_row = bcum_r[nh + h:nh + h + 1, :]
        m_prev = m_scr[h][0:1, 0:1]
        c_mat = c_scr[h]
        n_vec = n_scr[h]

        log_intra = jnp.where(tri, b_col - b_row + i_row, NEG_BIG)
        log_inter = b_col + m_prev
        m_t = jnp.maximum(log_inter, jnp.max(log_intra, axis=1, keepdims=True))
        w_inter = jnp.exp(log_inter - m_t)
        scores = _dot_nt(q, k) * jnp.exp(log_intra - m_t)
        num = w_inter * _dot(q, c_mat.astype(BF16)) + _dot(scores.astype(BF16), v)
        den = (w_inter * jnp.sum(q.astype(F32) * n_vec, axis=1, keepdims=True)
               + jnp.sum(scores, axis=1, keepdims=True))
        hh = num / jnp.maximum(jnp.abs(den), jnp.exp(-m_t))
        h_ref[0, 0, :, sl] = hh.astype(h_ref.dtype)

        total_f = jnp.sum(lfr[nh + h:nh + h + 1, :], axis=1, keepdims=True)
        log_w_row = total_f - b_row + i_row
        m_new = jnp.maximum(total_f + m_prev, jnp.max(log_w_row, axis=1, keepdims=True))
        decay = jnp.exp(total_f + m_prev - m_new)
        w_col = jnp.exp(total_f - b_col + i_col - m_new)
        wv = (w_col * v.astype(F32)).astype(BF16)
        c_scr[h] = decay * c_mat + _dot_tn(k, wv)
        n_scr[h] = decay * n_vec + jnp.sum(w_col * k.astype(F32), axis=0, keepdims=True)
        m_scr[h] = jnp.broadcast_to(m_new, m_scr.shape[1:])


def _chunk_order(n_ctx_chunks, nchunks):
    def order(d, j):
        bwd = jnp.where(j < n_ctx_chunks, n_ctx_chunks - 1 - j, nchunks - 1 - (j - n_ctx_chunks))
        return jnp.where(d == 0, j, bwd)
    return order


def _mlstm(q, k, v, gc, gr, gate_b, n_ctx):
    bsz, s, w = q.shape
    lc = ROW_TILE
    nchunks = s // lc
    order = _chunk_order(n_ctx // lc, nchunks)
    nh = MLSTM_HEADS
    gb = gate_b.reshape(2, 2 * nh)
    bc = jnp.pad(gb, ((0, 0), (0, GATE_PAD - 2 * nh))).reshape(2, 1, GATE_PAD)
    br = gb.reshape(2, 2 * nh, 1)
    gr4 = gr.reshape(bsz, 2, 2 * nh, s)
    row = lambda b, d, j: (b, order(d, j), 0)
    return pl.pallas_call(
        _mlstm_kernel,
        grid=(bsz, 2, nchunks),
        in_specs=[pl.BlockSpec((1, lc, w), row), pl.BlockSpec((1, lc, w), row),
                  pl.BlockSpec((1, lc, w), row),
                  pl.BlockSpec((1, lc, GATE_PAD), lambda b, d, j: (b, order(d, j), d)),
                  pl.BlockSpec((1, 1, 2 * nh, lc), lambda b, d, j: (b, d, 0, order(d, j))),
                  pl.BlockSpec((1, 1, GATE_PAD), lambda b, d, j: (d, 0, 0)),
                  pl.BlockSpec((1, 2 * nh, 1), lambda b, d, j: (d, 0, 0))],
        out_specs=pl.BlockSpec((1, 1, lc, w), lambda b, d, j: (d, b, order(d, j), 0)),
        out_shape=jax.ShapeDtypeStruct((2, bsz, s, w), BF16),
        scratch_shapes=[pltpu.VMEM((nh, MLSTM_DH, MLSTM_DH), F32),
                        pltpu.VMEM((nh, 1, MLSTM_DH), F32),
                        pltpu.VMEM((nh, 8, V7X_LANES), F32)],
        compiler_params=_cparams(("arbitrary", "arbitrary", "arbitrary"), 48),
        name="mlstm",
    )(q, k, v, gc, gr4, bc, br)


def _lru_kernel(xf_ref, xb_ref, wa_ref, wx_ref, ba_ref, bx_ref, lam_ref, hf_ref, hb_ref,
                a_scr, b_scr, carry):
    j = pl.program_id(0)
    bsz, t_rows, w = xf_ref.shape
    bw = V7X_MXU_DIM

    @pl.when(j == 0)
    def _():
        carry[...] = jnp.zeros_like(carry)

    for d, x_ref in enumerate((xf_ref, xb_ref)):
        x = x_ref[...].reshape(bsz * t_rows, w)
        xb = x.astype(BF16)
        sp = _softplus(-lam_ref[d])
        for jj in range(w // bw):
            sl = slice(jj * bw, (jj + 1) * bw)
            r = _sigmoid(_dot(xb[:, sl], wa_ref[d, jj]) + ba_ref[d, :, sl])
            gi = _sigmoid(_dot(xb[:, sl], wx_ref[d, jj]) + bx_ref[d, :, sl])
            a = jnp.exp(-LRU_C * r * sp[:, sl])
            b = jnp.sqrt(1.0 - a * a) * gi * x[:, sl]
            for half in range(bw // V7X_LANES):
                hs = slice(half * V7X_LANES, (half + 1) * V7X_LANES)
                a_scr[d, jj * (bw // V7X_LANES) + half] = a[:, hs]
                b_scr[d, jj * (bw // V7X_LANES) + half] = b[:, hs]

    nlt = w // V7X_LANES

    def body(t, hc):
        rows = (pl.ds(t, bsz, stride=t_rows), pl.ds(t_rows - 1 - t, bsz, stride=t_rows))
        out = []
        for d in range(2):
            for c in range(nlt):
                hn = a_scr[d, c, rows[d], :] * hc[d * nlt + c] + b_scr[d, c, rows[d], :]
                b_scr[d, c, rows[d], :] = hn
                out.append(hn)
        return tuple(out)

    init = tuple(carry[d, c] for d in range(2) for c in range(nlt))
    fin = lax.fori_loop(0, t_rows, body, init, unroll=4)
    for d, h_ref in enumerate((hf_ref, hb_ref)):
        for c in range(nlt):
            carry[d, c] = fin[d * nlt + c]
            h_ref[:, :, c * V7X_LANES:(c + 1) * V7X_LANES] = (
                b_scr[d, c].reshape(bsz, t_rows, V7X_LANES).astype(h_ref.dtype))


def _lru_blockdiag(w):
    per = V7X_MXU_DIM // LRU_BW
    nt = LRU_BLOCKS // per
    w4 = w.reshape(nt, per, LRU_BW, LRU_BW)
    eye = jnp.eye(per, dtype=w.dtype)
    t = jnp.einsum('tpcd,pq->tpcqd', w4, eye)
    return t.reshape(nt, V7X_MXU_DIM, V7X_MXU_DIM).astype(BF16)


def _lru(xc, wa, wx, ba, bx, lam, n_ctx):
    bsz, s, w = xc.shape
    tm = LRU_CHUNK
    nchunks = s // tm
    order = _chunk_order(n_ctx // tm, nchunks)
    ntile = w // V7X_MXU_DIM
    full4 = pl.BlockSpec((2, ntile, V7X_MXU_DIM, V7X_MXU_DIM), lambda j: (0, 0, 0, 0))
    vec = pl.BlockSpec((2, 1, w), lambda j: (0, 0, 0))
    fwd = pl.BlockSpec((bsz, tm, w), lambda j: (0, order(0, j), 0))
    bwd = pl.BlockSpec((bsz, tm, w), lambda j: (0, order(1, j), 0))
    blockdiag = lambda wts: jnp.stack([_lru_blockdiag(wts[0]), _lru_blockdiag(wts[1])])
    return pl.pallas_call(
        _lru_kernel,
        grid=(nchunks,),
        in_specs=[fwd, bwd, full4, full4, vec, vec, vec],
        out_specs=[fwd, bwd],
        out_shape=[jax.ShapeDtypeStruct((bsz, s, w), BF16), jax.ShapeDtypeStruct((bsz, s, w), BF16)],
        scratch_shapes=[pltpu.VMEM((2, w // V7X_LANES, bsz * tm, V7X_LANES), F32),
                        pltpu.VMEM((2, w // V7X_LANES, bsz * tm, V7X_LANES), F32),
                        pltpu.VMEM((2, w // V7X_LANES, bsz, V7X_LANES), F32)],
        compiler_params=_cparams(("arbitrary",), 56),
        name="lru",
    )(xc, xc, blockdiag(wa), blockdiag(wx), ba.reshape(2, 1, w), bx.reshape(2, 1, w), lam.reshape(2, 1, w))


def _route(lt, first, route_ref, cnt_ref):
    ne, tm = lt.shape
    erow = lax.broadcasted_iota(jnp.int32, (ne, tm), 0).astype(F32)
    lg = lt
    tops, hots = [], []
    for _ in range(TOP_K):
        m = jnp.max(lg, axis=0, keepdims=True)
        idx = jnp.min(jnp.where(lg == m, erow, float(ne)), axis=0, keepdims=True)
        hot = erow == idx
        lg = jnp.where(hot, -jnp.inf, lg)
        tops.append((m, idx))
        hots.append(hot)
    es = [jnp.exp(m - tops[0][0]) for m, _ in tops]
    denom = es[0]
    for e in es[1:]:
        denom = denom + e

    if first is not None:
        @pl.when(first)
        def _():
            cnt_ref[...] = jnp.zeros_like(cnt_ref)

    chosen = hots[0]
    for hot in hots[1:]:
        chosen = jnp.logical_or(chosen, hot)
    chosen_f = jnp.where(chosen, 1.0, 0.0)
    row = lax.broadcasted_iota(jnp.int32, (tm, tm), 0)
    col = lax.broadcasted_iota(jnp.int32, (tm, tm), 1)
    before = jnp.where(row < col, 1.0, 0.0).astype(BF16)
    ranks = _dot(chosen_f.astype(BF16), before) + cnt_ref[:, 0:1]
    cnt_ref[...] = cnt_ref[...] + jnp.sum(chosen_f, axis=1, keepdims=True)
    srow = lax.broadcasted_iota(jnp.int32, (route_ref.shape[0], tm), 0)
    out = jnp.zeros((route_ref.shape[0], tm), F32)
    for kk in range(TOP_K):
        rank = jnp.sum(jnp.where(hots[kk], ranks, 0.0), axis=0, keepdims=True)
        out = jnp.where(srow == kk, tops[kk][1], out)
        out = jnp.where(srow == TOP_K + kk, rank, out)
        out = jnp.where(srow == 2 * TOP_K + kk, es[kk] / denom, out)
    route_ref[...] = out


def _tail(sub, y, x, mod, n2_ref, wrt_ref, br_ref, xo_ref, v_ref, route_ref, cnt_ref):
    tm = y.shape[0]
    rows = slice(sub * tm, (sub + 1) * tm)
    xn = x + mod[2:3] * y
    xo_ref[rows, :] = xn
    v = _modnorm(xn, n2_ref[...], mod[4:5], mod[3:4])
    v_ref[rows, :] = v.astype(v_ref.dtype)
    logits_t = _dot_nt(wrt_ref[...], v, precision=HIGHEST) + br_ref[...]
    first = (pl.program_id(0) == 0) if sub == 0 else None
    _route(logits_t, first, route_ref.at[:, rows], cnt_ref)


def _tail_specs(n_rows, tm, d):
    full = lambda shape: pl.BlockSpec(shape, lambda p: (0,) * len(shape))
    step = TAIL_SUB * tm
    in_specs = [full((N_EXPERTS, d)), full((N_EXPERTS, 1))]
    out_specs = [pl.BlockSpec((step, d), lambda p: (p, 0)), pl.BlockSpec((step, d), lambda p: (p, 0)),
                 pl.BlockSpec((ROUTE_ROWS, step), lambda p: (0, p)),
                 full((N_EXPERTS, V7X_LANES))]
    out_shape = [jax.ShapeDtypeStruct((n_rows, d), F32),
                 jax.ShapeDtypeStruct((n_rows, d), BF16),
                 jax.ShapeDtypeStruct((ROUTE_ROWS, n_rows), F32),
                 jax.ShapeDtypeStruct((N_EXPERTS, V7X_LANES), F32)]
    return in_specs, out_specs, out_shape


def _even_out_kernel(hm_ref, hl0_ref, hl1_ref, o_ref, yg_ref, mg_ref, wout_ref, x_ref, *rest):
    mod_refs = rest[:TAIL_SUB]
    n2_ref, wr_ref, br_ref, xo_ref, v_ref, route_ref, cnt_ref = rest[TAIL_SUB:]
    tm = x_ref.shape[0] // TAIL_SUB
    for sub in range(TAIL_SUB):
        rows = slice(sub * tm, (sub + 1) * tm)
        hm = hm_ref[0, rows, :].astype(F32) + hm_ref[1, rows, :].astype(F32)
        parts = []
        for h in range(MLSTM_HEADS):
            sl = slice(h * MLSTM_DH, (h + 1) * MLSTM_DH)
            parts.append(_rms(hm[:, sl], mg_ref[:, sl]))
        hmn = jnp.concatenate(parts, axis=1) * _sigmoid(o_ref[rows, :].astype(F32))
        hl = ((hl0_ref[rows, :].astype(F32) + hl1_ref[rows, :].astype(F32))
              * _gelu_tanh(yg_ref[rows, :].astype(F32)))
        y = (_dot(hmn.astype(BF16), wout_ref[0:MLSTM_W, :])
             + _dot(hl.astype(BF16), wout_ref[MLSTM_W:MLSTM_W + LRU_W, :]))
        _tail(sub, y, x_ref[rows, :], mod_refs[sub][0, 0], n2_ref, wr_ref, br_ref, xo_ref, v_ref, route_ref,
              cnt_ref)


def _even_out(hm, hl0, hl1, o_pre, yg, mnorm_g, w_out, x, mod, n2, w_r, b_r, n_ctx):
    bsz, s, d = x.shape
    tm = ROW_TILE
    nt = s // tm
    nct = n_ctx // tm
    assert (bsz * nt) % TAIL_SUB == 0
    step = TAIL_SUB * tm
    flat = lambda a: a.reshape(bsz * s, a.shape[-1])
    rowp = lambda w: pl.BlockSpec((step, w), lambda p: (p, 0))
    full = lambda shape: pl.BlockSpec(shape, lambda p: (0,) * len(shape))

    def mod_spec(sub):
        def index(p):
            q = p * TAIL_SUB + sub
            return (q // nt, (q % nt >= nct).astype(jnp.int32), 0, 0)
        return pl.BlockSpec((1, 1, 8, d), index)

    tail_in, out_specs, out_shape = _tail_specs(bsz * s, tm, d)
    return pl.pallas_call(
        _even_out_kernel,
        grid=(bsz * nt // TAIL_SUB,),
        in_specs=[pl.BlockSpec((2, step, MLSTM_W), lambda p: (0, p, 0)),
                  rowp(LRU_W), rowp(LRU_W), rowp(MLSTM_W), rowp(LRU_W),
                  full((1, MLSTM_W)), full((MLSTM_W + LRU_W, d)), rowp(d)]
                 + [mod_spec(sub) for sub in range(TAIL_SUB)] + [full((1, d))] + tail_in,
        out_specs=out_specs,
        out_shape=out_shape,
        compiler_params=_cparams(("arbitrary",), 56),
        name="even_out",
    )(hm.reshape(2, bsz * s, MLSTM_W), flat(hl0), flat(hl1), flat(o_pre), flat(yg),
      mnorm_g.reshape(1, MLSTM_W), w_out, flat(x), *([mod] * TAIL_SUB), n2.reshape(1, d),
      w_r.T, b_r.reshape(N_EXPERTS, 1))


def _odd_out_kernel(a_ref, wout_ref, *rest):
    x_refs = rest[:TAIL_SUB]
    mod_refs = rest[TAIL_SUB:2 * TAIL_SUB]
    n2_ref, wr_ref, br_ref, xo_ref, v_ref, route_ref, cnt_ref = rest[2 * TAIL_SUB:]
    tm = a_ref.shape[0] // TAIL_SUB
    for sub in range(TAIL_SUB):
        y = _dot(a_ref[sub * tm:(sub + 1) * tm, :], wout_ref[...])
        _tail(sub, y, x_refs[sub][...], mod_refs[sub][0, 0], n2_ref, wr_ref, br_ref, xo_ref, v_ref, route_ref,
              cnt_ref)


def _odd_out(attn, w_out, x, mod, n2, w_r, b_r, n_ctx):
    bsz, seq, d = attn.shape
    s = x.shape[1]
    tm = ROW_TILE
    ntl = seq // tm
    nta = s // tm
    nct = n_ctx // tm
    assert (bsz * ntl) % TAIL_SUB == 0
    step = TAIL_SUB * tm
    full = lambda shape: pl.BlockSpec(shape, lambda p: (0,) * len(shape))

    def x_spec(sub):
        def index(p):
            q = p * TAIL_SUB + sub
            return ((q // ntl) * nta + nct + q % ntl, 0)
        return pl.BlockSpec((tm, d), index)

    def mod_spec(sub):
        return pl.BlockSpec((1, 1, 8, d), lambda p: ((p * TAIL_SUB + sub) // ntl, 1, 0, 0))

    tail_in, out_specs, out_shape = _tail_specs(bsz * seq, tm, d)
    xf = x.reshape(bsz * s, d)
    return pl.pallas_call(
        _odd_out_kernel,
        grid=(bsz * ntl // TAIL_SUB,),
        in_specs=[pl.BlockSpec((step, d), lambda p: (p, 0)), full((d, d))]
                 + [x_spec(sub) for sub in range(TAIL_SUB)] + [mod_spec(sub) for sub in range(TAIL_SUB)]
                 + [full((1, d))] + tail_in,
        out_specs=out_specs,
        out_shape=out_shape,
        compiler_params=_cparams(("arbitrary",), 48),
        name="odd_out",
    )(attn.reshape(bsz * seq, d), w_out, *([xf] * TAIL_SUB), *([mod] * TAIL_SUB), n2.reshape(1, d),
      w_r.T, b_r.reshape(N_EXPERTS, 1))


def _expert_kernel(te_ref, nu_ref, x_ref, w1_ref, b1_ref, w2_ref, b2_ref, rw_ref, *rest):
    o_ref, w1b, w2b = rest[-3:]
    t = pl.program_id(0)
    tm = x_ref.shape[0]
    used = t < nu_ref[0]
    new_expert = jnp.logical_or(t == 0, te_ref[t] != te_ref[jnp.maximum(t - 1, 0)])

    @pl.when(jnp.logical_and(used, new_expert))
    def _():
        w1b[...] = w1_ref[0, 0].astype(BF16)
        w2b[...] = w2_ref[0, 0].astype(BF16)

    @pl.when(used)
    def _():
        hid = _dot(x_ref[...], w1b[...]) + b1_ref[0, 0]
        gate = jnp.minimum(hid[:, :D_FF], SWIGLU_LIMIT)
        up = jnp.clip(hid[:, D_FF:], -SWIGLU_LIMIT, SWIGLU_LIMIT)
        act = (up + 1.0) * gate * _sigmoid(SWIGLU_ALPHA * gate)
        y = _dot(act.astype(BF16), w2b[...]) + b2_ref[0, 0]
        w_col = jnp.transpose(jnp.broadcast_to(rw_ref[0], (V7X_LANES, tm)))[:, 0:1]
        o_ref[...] = (y * w_col).astype(o_ref.dtype)

    @pl.when(jnp.logical_not(used))
    def _():
        o_ref[...] = jnp.zeros_like(o_ref)


def _experts(xs, row_w, tile_expert, n_used, layer, w1, b1, w2, b2, chunk, ys_buf):
    rows, d = xs.shape
    nt, _, tm = row_w.shape
    nl, ne, _, ff2 = w1.shape
    in_specs = [pl.BlockSpec((tm, d), lambda t, te, nu: (t, 0)),
                pl.BlockSpec((1, 1, d, ff2), lambda t, te, nu: (layer, te[t], 0, 0)),
                pl.BlockSpec((1, 1, 1, ff2), lambda t, te, nu: (layer, te[t], 0, 0)),
                pl.BlockSpec((1, 1, ff2 // 2, d), lambda t, te, nu: (layer, te[t], 0, 0)),
                pl.BlockSpec((1, 1, 1, d), lambda t, te, nu: (layer, te[t], 0, 0)),
                pl.BlockSpec((1, 1, tm), lambda t, te, nu: (t, 0, 0))]
    args = [tile_expert, n_used, xs, w1, b1.reshape(nl, ne, 1, ff2), w2, b2.reshape(nl, ne, 1, d), row_w]
    aliases = {}
    if ys_buf is not None:
        in_specs.append(pl.BlockSpec(memory_space=pl.ANY))
        aliases = {len(args): 0}
        args.append(ys_buf)
    return pl.pallas_call(
        _expert_kernel,
        grid_spec=pltpu.PrefetchScalarGridSpec(
            num_scalar_prefetch=2,
            grid=(nt,),
            in_specs=in_specs,
            out_specs=pl.BlockSpec((tm, d), lambda t, te, nu: (chunk * nt + t, 0)),
            scratch_shapes=[pltpu.VMEM((d, ff2), BF16), pltpu.VMEM((ff2 // 2, d), BF16)],
        ),
        out_shape=jax.ShapeDtypeStruct((MOE_CHUNKS * rows, d), BF16),
        input_output_aliases=aliases,
        compiler_params=_cparams(("arbitrary",), 56),
        name="moe_experts",
    )(*args)


def _moe(v, route, counts, layer, w1, b1, w2, b2):
    t, d = v.shape
    tm = MOE_TILE
    nrows = t * TOP_K
    ntc = -(-(-(-nrows // tm) + N_EXPERTS) // MOE_CHUNKS)
    nt = ntc * MOE_CHUNKS
    idx = route[0:TOP_K].astype(jnp.int32)
    rank = route[TOP_K:2 * TOP_K].astype(jnp.int32)
    weight = route[2 * TOP_K:3 * TOP_K]
    sizes = counts[:, 0].astype(jnp.int32)
    start = jnp.cumsum(sizes) - sizes
    padded = (sizes + tm - 1) // tm * tm
    pad_end = jnp.cumsum(padded)
    pad_start = pad_end - padded
    pair_pos = rank
    for e in range(N_EXPERTS):
        pair_pos = pair_pos + jnp.where(idx == e, pad_start[e], 0)
    bits = max(1, (nrows - 1).bit_length())
    assert N_EXPERTS << bits < 2 ** 31
    pair_id = (jnp.arange(t, dtype=jnp.int32)[None, :] * TOP_K + jnp.arange(TOP_K, dtype=jnp.int32)[:, None])
    order = jnp.sort(((idx << bits) + pair_id).reshape(-1)) & ((1 << bits) - 1)
    tile_row0 = jnp.arange(nt, dtype=jnp.int32) * tm
    tile_expert = jnp.minimum(jnp.sum(pad_end[None, :] <= tile_row0[:, None], axis=1, dtype=jnp.int32),
                              N_EXPERTS - 1)
    n_used = (pad_end[-1] // tm).reshape(1).astype(jnp.int32)
    onehot_te = tile_expert[:, None] == jnp.arange(N_EXPERTS, dtype=jnp.int32)[None, :]
    pick = lambda tbl: jnp.sum(jnp.where(onehot_te, tbl[None, :], 0), axis=1)
    r_in = tile_row0[:, None] + jnp.arange(tm, dtype=jnp.int32)[None, :] - pick(pad_start)[:, None]
    valid = r_in < pick(sizes)[:, None]
    src = jnp.where(valid, pick(start)[:, None] + r_in, 0).reshape(-1)
    pair = order.at[src].get(mode='promise_in_bounds')
    row_token = pair // TOP_K
    flat_w = weight.reshape(-1).at[(pair % TOP_K) * t + row_token].get(mode='promise_in_bounds')
    row_w = jnp.where(valid, flat_w.reshape(nt, tm), 0.0)
    row_token = row_token.reshape(MOE_CHUNKS, ntc * tm)
    row_w = row_w.reshape(MOE_CHUNKS, ntc, 1, tm)
    tile_expert = tile_expert.reshape(MOE_CHUNKS, ntc)
    ys = None
    for c in range(MOE_CHUNKS):
        xs = v.at[row_token[c]].get(mode='promise_in_bounds')
        ys = _experts(xs, row_w[c], tile_expert[c], jnp.clip(n_used - c * ntc, 0, ntc), layer,
                      w1, b1, w2, b2, c, ys)
    return [ys.at[pair_pos[kk]].get(mode='promise_in_bounds') for kk in range(TOP_K)]


def _rope(t, cos, sin, lane_lo):
    swapped = jnp.where(lane_lo, pltpu.roll(t, ATT_DH - ROPE_AXIS_DIM // 2, 1),
                        pltpu.roll(t, ROPE_AXIS_DIM // 2, 1))
    return t * cos + swapped * sin


def _proj_odd_kernel(x_ref, f0, f1, f2, f3, mod0_ref, mod_ref, g_ref, w_ref, qg_ref, kg_ref,
                     cos_ref, sin_ref, h_ref, q_ref, k_ref, v_ref):
    f = (f0[0].astype(F32) + f1[0].astype(F32)) + (f2[0].astype(F32) + f3[0].astype(F32))
    hcur = x_ref[0] + mod0_ref[0, 0][5:6] * f
    h_ref[0] = hcur
    mod = mod_ref[0, 0]
    u = _modnorm(hcur, g_ref[...], mod[1:2], mod[0:1]).astype(BF16)
    z = _dot(u, w_ref[...])
    cos = cos_ref[...]
    sin = sin_ref[...]
    lane = lax.broadcasted_iota(jnp.int32, cos.shape, 1)
    lane_lo = (lane % ROPE_AXIS_DIM) < (ROPE_AXIS_DIM // 2)
    qw = ATT_HEADS * ATT_DH
    kw = ATT_KV_HEADS * ATT_DH
    for hh in range(ATT_HEADS):
        sl = slice(hh * ATT_DH, (hh + 1) * ATT_DH)
        t = _rope(_rms(z[:, sl], qg_ref[...]), cos, sin, lane_lo)
        q_ref[0, :, sl] = (t * (ATT_DH ** -0.5 * LOG2_E)).astype(q_ref.dtype)
    for hh in range(ATT_KV_HEADS):
        sl = slice(hh * ATT_DH, (hh + 1) * ATT_DH)
        t = _rope(_rms(z[:, qw + hh * ATT_DH:qw + (hh + 1) * ATT_DH], kg_ref[...]), cos, sin, lane_lo)
        k_ref[0, :, sl] = t.astype(k_ref.dtype)
    v_ref[0] = z[:, qw + kw:qw + 2 * kw].astype(v_ref.dtype)


def _proj_odd(x, fparts, mod0, mod, g, w, qg, kg, cos_tab, sin_tab, n_ctx):
    bsz, s, d = x.shape
    tm = ROW_TILE
    nt = s // tm
    nct = n_ctx // tm
    n = w.shape[1]
    qw = ATT_HEADS * ATT_DH
    kw = ATT_KV_HEADS * ATT_DH
    row = lambda b, i: (b, i, 0)
    seg = lambda b, i: (b, (i >= nct).astype(jnp.int32), 0, 0)
    full = lambda shape: pl.BlockSpec(shape, lambda b, i: (0,) * len(shape))
    tab = pl.BlockSpec((tm, ATT_DH), lambda b, i: (i, 0))
    return pl.pallas_call(
        _proj_odd_kernel,
        grid=(bsz, nt),
        in_specs=[pl.BlockSpec((1, tm, d), row)] + [pl.BlockSpec((1, tm, d), row)] * TOP_K
                 + [pl.BlockSpec((1, 1, 8, d), seg), pl.BlockSpec((1, 1, 8, d), seg),
                    full((1, d)), full((d, n)), full((1, ATT_DH)), full((1, ATT_DH)), tab, tab],
        out_specs=[pl.BlockSpec((1, tm, d), row), pl.BlockSpec((1, tm, qw), row),
                   pl.BlockSpec((1, tm, kw), row), pl.BlockSpec((1, tm, kw), row)],
        out_shape=[jax.ShapeDtypeStruct((bsz, s, d), F32),
                   jax.ShapeDtypeStruct((bsz, s, qw), BF16),
                   jax.ShapeDtypeStruct((bsz, s, kw), BF16),
                   jax.ShapeDtypeStruct((bsz, s, kw), BF16)],
        compiler_params=_cparams(("arbitrary", "arbitrary"), 48),
        name="proj_odd",
    )(x, *fparts, mod0, mod, g.reshape(1, d), w, qg.reshape(1, ATT_DH), kg.reshape(1, ATT_DH),
      cos_tab, sin_tab)


def _rope_tables(n_ctx, seq):
    rows = seq // GRID_W
    pos_r = jnp.repeat(jnp.arange(rows), GRID_W).astype(F32)
    pos_c = jnp.tile(jnp.arange(GRID_W), rows).astype(F32)
    inv_freq = ROPE_THETA ** (-jnp.arange(0, ROPE_AXIS_DIM, 2, dtype=F32) / ROPE_AXIS_DIM)
    ar = pos_r[:, None] * inv_freq
    ac = pos_c[:, None] * inv_freq
    cos = jnp.concatenate([jnp.cos(ar), jnp.cos(ar), jnp.cos(ac), jnp.cos(ac)], axis=-1)
    sin = jnp.concatenate([-jnp.sin(ar), jnp.sin(ar), -jnp.sin(ac), jnp.sin(ac)], axis=-1)
    cos = jnp.concatenate([jnp.ones((n_ctx, ATT_DH), F32), cos], axis=0)
    sin = jnp.concatenate([jnp.zeros((n_ctx, ATT_DH), F32), sin], axis=0)
    return cos, sin


def _attn_kernel(q_ref, k_ref, v_ref, o_ref):
    k = k_ref[0]
    v = v_ref[0]
    for g in range(ATT_GROUP):
        sl = slice(g * ATT_DH, (g + 1) * ATT_DH)
        s = _dot_nt(q_ref[0, :, sl], k)
        p = jnp.exp2(s - jnp.max(s, axis=1, keepdims=True))
        l = jnp.sum(p, axis=1, keepdims=True)
        o_ref[0, :, sl] = (_dot(p.astype(BF16), v) / l).astype(o_ref.dtype)


def _attention(q, k, v, n_ctx):
    bsz, s, qw = q.shape
    seq = s - n_ctx
    tq = ATT_Q_TILE
    nct = n_ctx // tq
    gw = ATT_GROUP * ATT_DH
    return pl.pallas_call(
        _attn_kernel,
        grid=(bsz, ATT_KV_HEADS, seq // tq),
        in_specs=[pl.BlockSpec((1, tq, gw), lambda b, h, i: (b, i + nct, h)),
                  pl.BlockSpec((1, s, ATT_DH), lambda b, h, i: (b, 0, h)),
                  pl.BlockSpec((1, s, ATT_DH), lambda b, h, i: (b, 0, h))],
        out_specs=pl.BlockSpec((1, tq, gw), lambda b, h, i: (b, i, h)),
        out_shape=jax.ShapeDtypeStruct((bsz, seq, qw), BF16),
        compiler_params=_cparams(("arbitrary", "arbitrary", "arbitrary"), 48),
        name="attention",
    )(q, k, v)


def _final_kernel(x_ref, f0, f1, f2, f3, mod_ref, g_ref, o_ref):
    f = (f0[0].astype(F32) + f1[0].astype(F32)) + (f2[0].astype(F32) + f3[0].astype(F32))
    o_ref[0] = _rms(x_ref[0] + mod_ref[0, 0][5:6] * f, g_ref[...])


def _final(x, fparts, mod, g):
    bsz, seq, d = x.shape
    tm = ROW_TILE
    row = lambda b, i: (b, i, 0)
    return pl.pallas_call(
        _final_kernel,
        grid=(bsz, seq // tm),
        in_specs=[pl.BlockSpec((1, tm, d), row)] * (1 + TOP_K)
                 + [pl.BlockSpec((1, 1, 8, d), lambda b, i: (b, 1, 0, 0)),
                    pl.BlockSpec((1, d), lambda b, i: (0, 0))],
        out_specs=pl.BlockSpec((1, tm, d), row),
        out_shape=jax.ShapeDtypeStruct((bsz, seq, d), F32),
        compiler_params=_cparams(("arbitrary", "arbitrary"), 32),
        name="final_norm",
    )(x, *fparts, mod, g.reshape(1, d))


def _pack_even_w_in(w_in):
    w4 = 4 * MLSTM_W
    ng = 4 * MLSTM_HEADS
    wg = w_in[:, w4:w4 + ng]
    half = ng // 2
    pad = jnp.zeros((w_in.shape[0], GATE_PAD - half), w_in.dtype)
    packed = jnp.concatenate([w_in[:, :w4], w_in[:, w4 + ng:], wg[:, :half], pad, wg[:, half:], pad], axis=1)
    return packed.astype(BF16), wg.T.astype(BF16)


def kernel(x, c, ctx, c_ctx, mod_w, mod_b, norm1_g, norm2_g, final_g, ev_w_in, ev_qk_conv_w, ev_qk_conv_b, ev_gate_b, ev_mnorm_g, ev_lru_conv_w, ev_lru_conv_b, ev_lru_wa, ev_lru_ba, ev_lru_wx, ev_lru_bx, ev_lru_lam, ev_w_out, od_w_in, od_q_norm_g, od_k_norm_g, od_w_out, moe_w_r, moe_b_r, moe_w1, moe_b1, moe_w2, moe_b2):
    bsz, seq, d = x.shape
    n_ctx = ctx.shape[1]
    s = n_ctx + seq
    assert n_ctx % ROW_TILE == 0 and seq % ROW_TILE == 0 and seq % GRID_W == 0
    h = jnp.concatenate([ctx, x], axis=1)

    mod0 = _mod_table(c, c_ctx, mod_w[0], mod_b[0])
    w_packed, wg_t = _pack_even_w_in(ev_w_in[0])
    q, k, v, o_pre, xc, yg, gc, gr = _proj_even(h, mod0, norm1_g[0], w_packed, wg_t, ev_qk_conv_w[0], ev_qk_conv_b[0],
                                                ev_lru_conv_w[0], ev_lru_conv_b[0], n_ctx)
    hm = _mlstm(q, k, v, gc, gr, ev_gate_b[0], n_ctx)
    hl = _lru(xc, ev_lru_wa[0], ev_lru_wx[0], ev_lru_ba[0].reshape(2, LRU_W), ev_lru_bx[0].reshape(2, LRU_W),
              ev_lru_lam[0], n_ctx)
    x_mid, v0, route0, cnt0 = _even_out(hm, hl[0], hl[1], o_pre, yg, ev_mnorm_g[0], ev_w_out[0].astype(BF16),
                                        h, mod0, norm2_g[0], moe_w_r[0], moe_b_r[0], n_ctx)
    f0 = _moe(v0.reshape(bsz * s, d), route0, cnt0, 0, moe_w1, moe_b1, moe_w2, moe_b2)
    f0 = [p.reshape(bsz, s, d) for p in f0]

    mod1 = _mod_table(c, c_ctx, mod_w[1], mod_b[1])
    cos_tab, sin_tab = _rope_tables(n_ctx, seq)
    h1, q1, k1, v1 = _proj_odd(x_mid.reshape(bsz, s, d), f0, mod0, mod1, norm1_g[1], od_w_in[0].astype(BF16),
                               od_q_norm_g[0], od_k_norm_g[0], cos_tab, sin_tab, n_ctx)
    attn = _attention(q1, k1, v1, n_ctx)
    x2, v2, route2, cnt2 = _odd_out(attn, od_w_out[0].astype(BF16), h1, mod1, norm2_g[1],
                                    moe_w_r[1], moe_b_r[1], n_ctx)
    f1 = _moe(v2.reshape(bsz * seq, d), route2, cnt2, 1, moe_w1, moe_b1, moe_w2, moe_b2)
    f1 = [p.reshape(bsz, seq, d) for p in f1]
    return _final(x2.reshape(bsz, seq, d), f1, mod1, final_g)
```

```python
import functools

import jax
import jax.numpy as jnp
from jax import lax
from jax.experimental import pallas as pl
from jax.experimental.pallas import tpu as pltpu

F32 = jnp.float32
BF16 = jnp.bfloat16
HIGHEST = lax.Precision.HIGHEST

EPS = 1e-6
M_INIT = -1e30
NEG_BIG = -1e30

MLSTM_HEADS = 4
MLSTM_DH = 256
MLSTM_W = MLSTM_HEADS * MLSTM_DH
LRU_W = 1024
LRU_BLOCKS = 16
LRU_BW = LRU_W // LRU_BLOCKS
LRU_C = 8.0
CONV_W = 4
CONV_LEFT = 2
ATT_HEADS = 8
ATT_KV_HEADS = 2
ATT_GROUP = ATT_HEADS // ATT_KV_HEADS
ATT_DH = 128
GRID_W = 64
ROPE_AXIS_DIM = ATT_DH // 2
ROPE_THETA = 10000.0
N_EXPERTS = 32
TOP_K = 4
D_FF = 1024
SWIGLU_ALPHA = 1.702
SWIGLU_LIMIT = 7.0
LOG2_E = 1.4426950408889634

V7X_LANES = 128
V7X_MXU_DIM = 256
V7X_VMEM_BYTES = 64 * 1024 * 1024
MIB = 1024 * 1024

ROW_TILE = 256
HALO = 16
TAIL_SUB = 1
MOE_TILE = 512
MOE_CHUNKS = 8
ROUTE_ROWS = 16
ATT_Q_TILE = 256
GATE_PAD = V7X_LANES


def _cparams(semantics, vmem_mib):
    assert vmem_mib * MIB < V7X_VMEM_BYTES
    return pltpu.CompilerParams(dimension_semantics=semantics, vmem_limit_bytes=vmem_mib * MIB)


def _dot(a, b):
    return jnp.dot(a, b, preferred_element_type=F32)


def _dot_nt(a, b, precision=None):
    return lax.dot_general(a, b, (((1,), (1,)), ((), ())), precision=precision,
                           preferred_element_type=F32)


def _dot_tn(a, b):
    return lax.dot_general(a, b, (((0,), (0,)), ((), ())), preferred_element_type=F32)


def _sigmoid(x):
    return 0.5 * jnp.tanh(0.5 * x) + 0.5


def _log_sigmoid(x):
    return jnp.minimum(x, 0.0) - jnp.log1p(jnp.exp(-jnp.abs(x)))


def _softplus(x):
    return jnp.maximum(x, 0.0) + jnp.log1p(jnp.exp(-jnp.abs(x)))


def _gelu_tanh(x):
    return 0.5 * x * (1.0 + jnp.tanh(0.7978845608028654 * (x + 0.044715 * x * x * x)))


def _rms(x, g):
    return x * lax.rsqrt(jnp.mean(x * x, axis=-1, keepdims=True) + EPS) * g


def _modnorm(x, g, scale, shift):
    return _rms(x, g) * (1.0 + scale) + shift


def _modvec_kernel(c_ref, w_ref, b_ref, o_ref):
    c = c_ref[...]
    s = c * _sigmoid(c)
    o_ref[...] = jnp.dot(s, w_ref[...], precision=HIGHEST, preferred_element_type=F32) + b_ref[...]


def _modvec(cc, w, b):
    rows, d = cc.shape
    n = w.shape[1]
    tn = 1536
    return pl.pallas_call(
        _modvec_kernel,
        grid=(n // tn,),
        in_specs=[pl.BlockSpec((rows, d), lambda j: (0, 0)),
                  pl.BlockSpec((d, tn), lambda j: (0, j)),
                  pl.BlockSpec((1, tn), lambda j: (0, j))],
        out_specs=pl.BlockSpec((rows, tn), lambda j: (0, j)),
        out_shape=jax.ShapeDtypeStruct((rows, n), F32),
        compiler_params=_cparams(("arbitrary",), 32),
        name="modvec",
    )(cc, w, b.reshape(1, n))


def _mod_table(c, c_ctx, mod_w, mod_b):
    bsz, d = c.shape
    rows = ((bsz + 1 + 7) // 8) * 8
    cc = jnp.zeros((rows, d), F32).at[:bsz].set(c).at[bsz].set(c_ctx)
    mod = _modvec(cc, mod_w, mod_b)
    lat = mod[:bsz].reshape(bsz, 6, d)
    ctx = jnp.broadcast_to(mod[bsz].reshape(1, 6, d), (bsz, 6, d))
    tbl = jnp.stack([ctx, lat], axis=1)
    return jnp.pad(tbl, ((0, 0), (0, 0), (0, 2), (0, 0)))


def _proj_even_kernel(nct, x_ref, xp_ref, xn_ref, mod_ref, g_ref, w_ref, wgt_ref,
                      wqk_ref, bqk_ref, wxr_ref, bxr_ref,
                      q_ref, k_ref, v_ref, o_ref, xc_ref, yg_ref, gc_ref, gr_ref):
    i = pl.program_id(1)
    nt = pl.num_programs(1)
    tm = x_ref.shape[1]
    w = MLSTM_W
    first = jnp.logical_or(i == 0, i == nct)
    last = jnp.logical_or(i == nct - 1, i == nt - 1)
    mod = mod_ref[0, 0]
    xe = jnp.concatenate([xp_ref[0], x_ref[0], xn_ref[0]], axis=0)
    ue = _modnorm(xe, g_ref[...], mod[1:2], mod[0:1]).astype(BF16)
    rowi = lax.broadcasted_iota(jnp.int32, (tm + 2 * HALO, 1), 0)
    keep = jnp.where(rowi < HALO, jnp.where(first, 0.0, 1.0),
                     jnp.where(rowi >= HALO + tm, jnp.where(last, 0.0, 1.0), 1.0))

    te = tm + 2 * HALO

    def conv(z, w_ref, b_ref):
        z = z * keep
        acc = b_ref[...]
        for j in range(CONV_W):
            shift = (CONV_LEFT - j) % te
            zj = pltpu.roll(z, shift, 0) if shift else z
            acc = acc + w_ref[j:j + 1, :] * zj[HALO:HALO + tm]
        return acc

    y = conv(_dot(ue, w_ref[:, 0:2 * w]), wqk_ref, bqk_ref)
    y = y * _sigmoid(y)
    q_ref[0] = y[:, :w].astype(q_ref.dtype)
    k_ref[0] = (y[:, w:] * (MLSTM_DH ** -0.5)).astype(k_ref.dtype)
    xc_ref[0] = conv(_dot(ue, w_ref[:, 4 * w:4 * w + LRU_W]), wxr_ref, bxr_ref)
    u = ue[HALO:HALO + tm]
    v_ref[0] = _dot(u, w_ref[:, 2 * w:3 * w]).astype(v_ref.dtype)
    o_ref[0] = _dot(u, w_ref[:, 3 * w:4 * w]).astype(o_ref.dtype)
    yg_ref[0] = _dot(u, w_ref[:, 4 * w + LRU_W:4 * w + 2 * LRU_W]).astype(yg_ref.dtype)
    gc_ref[0] = _dot(u, w_ref[:, 4 * w + 2 * LRU_W:4 * w + 2 * LRU_W + 2 * GATE_PAD])
    gr_ref[0] = _dot_nt(wgt_ref[...], u)


def _proj_even(h, mod, g, w_packed, wg_t, wqk, bqk, wxr, bxr, n_ctx):
    bsz, s, d = h.shape
    tm = ROW_TILE
    nt = s // tm
    nct = n_ctx // tm
    ntot = w_packed.shape[1]
    ng = wg_t.shape[0]
    hb = tm // HALO
    nhb = s // HALO
    cq = 2 * MLSTM_W
    row = lambda b, i: (b, i, 0)
    prev = lambda b, i: (b, jnp.maximum(i * hb - 1, 0), 0)
    nxt = lambda b, i: (b, jnp.minimum((i + 1) * hb, nhb - 1), 0)
    full = lambda shape: pl.BlockSpec(shape, lambda b, i: (0, 0))
    return pl.pallas_call(
        functools.partial(_proj_even_kernel, nct),
        grid=(bsz, nt),
        in_specs=[pl.BlockSpec((1, tm, d), row), pl.BlockSpec((1, HALO, d), prev),
                  pl.BlockSpec((1, HALO, d), nxt),
                  pl.BlockSpec((1, 1, 8, d), lambda b, i: (b, (i >= nct).astype(jnp.int32), 0, 0)),
                  full((1, d)), full((d, ntot)), full((ng, d)),
                  full((CONV_W, cq)), full((1, cq)), full((CONV_W, LRU_W)), full((1, LRU_W))],
        out_specs=[pl.BlockSpec((1, tm, MLSTM_W), row),
                   pl.BlockSpec((1, tm, MLSTM_W), row),
                   pl.BlockSpec((1, tm, MLSTM_W), row),
                   pl.BlockSpec((1, tm, MLSTM_W), row),
                   pl.BlockSpec((1, tm, LRU_W), row),
                   pl.BlockSpec((1, tm, LRU_W), row),
                   pl.BlockSpec((1, tm, 2 * GATE_PAD), row),
                   pl.BlockSpec((1, ng, tm), lambda b, i: (b, 0, i))],
        out_shape=[jax.ShapeDtypeStruct((bsz, s, MLSTM_W), BF16),
                   jax.ShapeDtypeStruct((bsz, s, MLSTM_W), BF16),
                   jax.ShapeDtypeStruct((bsz, s, MLSTM_W), BF16),
                   jax.ShapeDtypeStruct((bsz, s, MLSTM_W), BF16),
                   jax.ShapeDtypeStruct((bsz, s, LRU_W), F32),
                   jax.ShapeDtypeStruct((bsz, s, LRU_W), BF16),
                   jax.ShapeDtypeStruct((bsz, s, 2 * GATE_PAD), F32),
                   jax.ShapeDtypeStruct((bsz, ng, s), F32)],
        compiler_params=_cparams(("arbitrary", "arbitrary"), 56),
        name="proj_even",
    )(h, h, h, mod, g.reshape(1, d), w_packed, wg_t, wqk, bqk.reshape(1, cq), wxr, bxr.reshape(1, LRU_W))


def _mlstm_kernel(q_ref, k_ref, v_ref, gc_ref, gr_ref, bc_ref, br_ref, h_ref, c_scr, n_scr, m_scr):
    d = pl.program_id(1)
    j = pl.program_id(2)
    lc = q_ref.shape[1]
    nh = MLSTM_HEADS
    dh = MLSTM_DH

    @pl.when(j == 0)
    def _():
        c_scr[...] = jnp.zeros_like(c_scr)
        n_scr[...] = jnp.zeros_like(n_scr)
        m_scr[...] = jnp.full(m_scr.shape, M_INIT, F32)

    row = lax.broadcasted_iota(jnp.int32, (lc, lc), 0)
    col = lax.broadcasted_iota(jnp.int32, (lc, lc), 1)
    lo = jnp.where(d == 1, row, col)
    hi = jnp.where(d == 1, col, row)
    tri = lo <= hi
    trif = tri.astype(F32)

    gc = gc_ref[0] + bc_ref[0]
    gr = gr_ref[0, 0] + br_ref[0]
    lfr = _log_sigmoid(gr)
    bcum_r = _dot_nt(lfr, trif, precision=HIGHEST)
    bcum_c = jnp.transpose(jnp.concatenate([bcum_r, jnp.zeros((GATE_PAD - 2 * nh, lc), F32)], axis=0))

    for h in range(nh):
        sl = slice(h * dh, (h + 1) * dh)
        q = q_ref[0, :, sl]
        k = k_ref[0, :, sl]
        v = v_ref[0, :, sl]
        i_col = gc[:, h:h + 1]
        b_col = bcum_c[:, nh + h:nh + h + 1]
        i_row = gr[h:h + 1, :]
        b_row = bcum_r[nh + h:nh + h + 1, :]
        m_prev = m_scr[h][0:1, 0:1]
        c_mat = c_scr[h]
        n_vec = n_scr[h]

        log_intra = jnp.where(tri, b_col - b_row + i_row, NEG_BIG)
        log_inter = b_col + m_prev
        m_t = jnp.maximum(log_inter, jnp.max(log_intra, axis=1, keepdims=True))
        w_inter = jnp.exp(log_inter - m_t)
        scores = _dot_nt(q, k) * jnp.exp(log_intra - m_t)
        num = w_inter * _dot(q, c_mat.astype(BF16)) + _dot(scores.astype(BF16), v)
        den = (w_inter * jnp.sum(q.astype(F32) * n_vec, axis=1, keepdims=True)
               + jnp.sum(scores, axis=1, keepdims=True))
        hh = num / jnp.maximum(jnp.abs(den), jnp.exp(-m_t))
        h_ref[0, 0, :, sl] = hh.astype(h_ref.dtype)

        total_f = jnp.sum(lfr[nh + h:nh + h + 1, :], axis=1, keepdims=True)
        log_w_row = total_f - b_row + i_row
        m_new = jnp.maximum(total_f + m_prev, jnp.max(log_w_row, axis=1, keepdims=True))
        decay = jnp.exp(total_f + m_prev - m_new)
        w_col = jnp.exp(total_f - b_col + i_col - m_new)
        wv = (w_col * v.astype(F32)).astype(BF16)
        c_scr[h] = decay * c_mat + _dot_tn(k, wv)
        n_scr[h] = decay * n_vec + jnp.sum(w_col * k.astype(F32), axis=0, keepdims=True)
        m_scr[h] = jnp.broadcast_to(m_new, m_scr.shape[1:])


def _chunk_order(n_ctx_chunks, nchunks):
    def order(d, j):
        bwd = jnp.where(j < n_ctx_chunks, n_ctx_chunks - 1 - j, nchunks - 1 - (j - n_ctx_chunks))
        return jnp.where(d == 0, j, bwd)
    return order


def _mlstm(q, k, v, gc, gr, gate_b, n_ctx):
    bsz, s, w = q.shape
    lc = ROW_TILE
    nchunks = s // lc
    order = _chunk_order(n_ctx // lc, nchunks)
    nh = MLSTM_HEADS
    gb = gate_b.reshape(2, 2 * nh)
    bc = jnp.pad(gb, ((0, 0), (0, GATE_PAD - 2 * nh))).reshape(2, 1, GATE_PAD)
    br = gb.reshape(2, 2 * nh, 1)
    gr4 = gr.reshape(bsz, 2, 2 * nh, s)
    row = lambda b, d, j: (b, order(d, j), 0)
    return pl.pallas_call(
        _mlstm_kernel,
        grid=(bsz, 2, nchunks),
        in_specs=[pl.BlockSpec((1, lc, w), row), pl.BlockSpec((1, lc, w), row),
                  pl.BlockSpec((1, lc, w), row),
                  pl.BlockSpec((1, lc, GATE_PAD), lambda b, d, j: (b, order(d, j), d)),
                  pl.BlockSpec((1, 1, 2 * nh, lc), lambda b, d, j: (b, d, 0, order(d, j))),
                  pl.BlockSpec((1, 1, GATE_PAD), lambda b, d, j: (d, 0, 0)),
                  pl.BlockSpec((1, 2 * nh, 1), lambda b, d, j: (d, 0, 0))],
        out_specs=pl.BlockSpec((1, 1, lc, w), lambda b, d, j: (d, b, order(d, j), 0)),
        out_shape=jax.ShapeDtypeStruct((2, bsz, s, w), BF16),
        scratch_shapes=[pltpu.VMEM((nh, MLSTM_DH, MLSTM_DH), F32),
                        pltpu.VMEM((nh, 1, MLSTM_DH), F32),
                        pltpu.VMEM((nh, 8, V7X_LANES), F32)],
        compiler_params=_cparams(("arbitrary", "arbitrary", "arbitrary"), 48),
        name="mlstm",
    )(q, k, v, gc, gr4, bc, br)


def _lru_kernel(reverse, x_ref, wa_ref, wx_ref, ba_ref, bx_ref, lam_ref, h_ref, a_scr, b_scr, carry):
    j = pl.program_id(1)
    t_rows = x_ref.shape[1]
    bw = V7X_MXU_DIM

    @pl.when(j == 0)
    def _():
        carry[...] = jnp.zeros_like(carry)

    x = x_ref[0]
    xb = x.astype(BF16)
    sp = _softplus(-lam_ref[...])
    for jj in range(LRU_W // bw):
        sl = slice(jj * bw, (jj + 1) * bw)
        r = _sigmoid(_dot(xb[:, sl], wa_ref[jj]) + ba_ref[:, sl])
        gi = _sigmoid(_dot(xb[:, sl], wx_ref[jj]) + bx_ref[:, sl])
        a = jnp.exp(-LRU_C * r * sp[:, sl])
        a_scr[:, sl] = a
        b_scr[:, sl] = jnp.sqrt(1.0 - a * a) * gi * x[:, sl]

    def body(t, hc):
        tt = t_rows - 1 - t if reverse else t
        hn = a_scr[pl.ds(tt, 1), :] * hc + b_scr[pl.ds(tt, 1), :]
        b_scr[pl.ds(tt, 1), :] = hn
        return hn

    carry[...] = lax.fori_loop(0, t_rows, body, carry[...], unroll=8)
    h_ref[0] = b_scr[...].astype(h_ref.dtype)


def _lru_blockdiag(w):
    per = V7X_MXU_DIM // LRU_BW
    nt = LRU_BLOCKS // per
    w4 = w.reshape(nt, per, LRU_BW, LRU_BW)
    eye = jnp.eye(per, dtype=w.dtype)
    t = jnp.einsum('tpcd,pq->tpcqd', w4, eye)
    return t.reshape(nt, V7X_MXU_DIM, V7X_MXU_DIM).astype(BF16)


def _lru(xc, wa, wx, ba, bx, lam, n_ctx, reverse):
    bsz, s, w = xc.shape
    tm = ROW_TILE
    nchunks = s // tm
    order = _chunk_order(n_ctx // tm, nchunks)
    d = 1 if reverse else 0
    row = lambda b, j: (b, order(d, j), 0)
    ntile = w // V7X_MXU_DIM
    full3 = pl.BlockSpec((ntile, V7X_MXU_DIM, V7X_MXU_DIM), lambda b, j: (0, 0, 0))
    vec = pl.BlockSpec((1, w), lambda b, j: (0, 0))
    return pl.pallas_call(
        functools.partial(_lru_kernel, reverse),
        grid=(bsz, nchunks),
        in_specs=[pl.BlockSpec((1, tm, w), row), full3, full3, vec, vec, vec],
        out_specs=pl.BlockSpec((1, tm, w), row),
        out_shape=jax.ShapeDtypeStruct((bsz, s, w), BF16),
        scratch_shapes=[pltpu.VMEM((tm, w), F32), pltpu.VMEM((tm, w), F32), pltpu.VMEM((1, w), F32)],
        compiler_params=_cparams(("arbitrary", "arbitrary"), 32),
        name="lru_bwd" if reverse else "lru_fwd",
    )(xc, _lru_blockdiag(wa), _lru_blockdiag(wx), ba.reshape(1, w), bx.reshape(1, w), lam.reshape(1, w))


def _route(lt, first, route_ref, cnt_ref):
    ne, tm = lt.shape
    erow = lax.broadcasted_iota(jnp.int32, (ne, tm), 0).astype(F32)
    lg = lt
    tops, hots = [], []
    for _ in range(TOP_K):
        m = jnp.max(lg, axis=0, keepdims=True)
        idx = jnp.min(jnp.where(lg == m, erow, float(ne)), axis=0, keepdims=True)
        hot = erow == idx
        lg = jnp.where(hot, -jnp.inf, lg)
        tops.append((m, idx))
        hots.append(hot)
    es = [jnp.exp(m - tops[0][0]) for m, _ in tops]
    denom = es[0]
    for e in es[1:]:
        denom = denom + e

    if first is not None:
        @pl.when(first)
        def _():
            cnt_ref[...] = jnp.zeros_like(cnt_ref)

    chosen = hots[0]
    for hot in hots[1:]:
        chosen = jnp.logical_or(chosen, hot)
    chosen_f = jnp.where(chosen, 1.0, 0.0)
    row = lax.broadcasted_iota(jnp.int32, (tm, tm), 0)
    col = lax.broadcasted_iota(jnp.int32, (tm, tm), 1)
    before = jnp.where(row < col, 1.0, 0.0).astype(BF16)
    ranks = _dot(chosen_f.astype(BF16), before) + cnt_ref[:, 0:1]
    cnt_ref[...] = cnt_ref[...] + jnp.sum(chosen_f, axis=1, keepdims=True)
    srow = lax.broadcasted_iota(jnp.int32, (route_ref.shape[0], tm), 0)
    out = jnp.zeros((route_ref.shape[0], tm), F32)
    for kk in range(TOP_K):
        rank = jnp.sum(jnp.where(hots[kk], ranks, 0.0), axis=0, keepdims=True)
        out = jnp.where(srow == kk, tops[kk][1], out)
        out = jnp.where(srow == TOP_K + kk, rank, out)
        out = jnp.where(srow == 2 * TOP_K + kk, es[kk] / denom, out)
    route_ref[...] = out


def _tail(sub, y, x, mod, n2_ref, wrt_ref, br_ref, xo_ref, v_ref, route_ref, cnt_ref):
    tm = y.shape[0]
    rows = slice(sub * tm, (sub + 1) * tm)
    xn = x + mod[2:3] * y
    xo_ref[rows, :] = xn
    v = _modnorm(xn, n2_ref[...], mod[4:5], mod[3:4])
    v_ref[rows, :] = v.astype(v_ref.dtype)
    logits_t = _dot_nt(wrt_ref[...], v, precision=HIGHEST) + br_ref[...]
    first = (pl.program_id(0) == 0) if sub == 0 else None
    _route(logits_t, first, route_ref.at[:, rows], cnt_ref)


def _tail_specs(n_rows, tm, d):
    full = lambda shape: pl.BlockSpec(shape, lambda p: (0,) * len(shape))
    step = TAIL_SUB * tm
    in_specs = [full((N_EXPERTS, d)), full((N_EXPERTS, 1))]
    out_specs = [pl.BlockSpec((step, d), lambda p: (p, 0)), pl.BlockSpec((step, d), lambda p: (p, 0)),
                 pl.BlockSpec((ROUTE_ROWS, step), lambda p: (0, p)),
                 full((N_EXPERTS, V7X_LANES))]
    out_shape = [jax.ShapeDtypeStruct((n_rows, d), F32),
                 jax.ShapeDtypeStruct((n_rows, d), BF16),
                 jax.ShapeDtypeStruct((ROUTE_ROWS, n_rows), F32),
                 jax.ShapeDtypeStruct((N_EXPERTS, V7X_LANES), F32)]
    return in_specs, out_specs, out_shape


def _even_out_kernel(hm_ref, hl0_ref, hl1_ref, o_ref, yg_ref, mg_ref, wout_ref, x_ref, *rest):
    mod_refs = rest[:TAIL_SUB]
    n2_ref, wr_ref, br_ref, xo_ref, v_ref, route_ref, cnt_ref = rest[TAIL_SUB:]
    tm = x_ref.shape[0] // TAIL_SUB
    for sub in range(TAIL_SUB):
        rows = slice(sub * tm, (sub + 1) * tm)
        hm = hm_ref[0, rows, :].astype(F32) + hm_ref[1, rows, :].astype(F32)
        parts = []
        for h in range(MLSTM_HEADS):
            sl = slice(h * MLSTM_DH, (h + 1) * MLSTM_DH)
            parts.append(_rms(hm[:, sl], mg_ref[:, sl]))
        hmn = jnp.concatenate(parts, axis=1) * _sigmoid(o_ref[rows, :].astype(F32))
        hl = ((hl0_ref[rows, :].astype(F32) + hl1_ref[rows, :].astype(F32))
              * _gelu_tanh(yg_ref[rows, :].astype(F32)))
        y = (_dot(hmn.astype(BF16), wout_ref[0:MLSTM_W, :])
             + _dot(hl.astype(BF16), wout_ref[MLSTM_W:MLSTM_W + LRU_W, :]))
        _tail(sub, y, x_ref[rows, :], mod_refs[sub][0, 0], n2_ref, wr_ref, br_ref, xo_ref, v_ref, route_ref,
              cnt_ref)


def _even_out(hm, hl0, hl1, o_pre, yg, mnorm_g, w_out, x, mod, n2, w_r, b_r, n_ctx):
    bsz, s, d = x.shape
    tm = ROW_TILE
    nt = s // tm
    nct = n_ctx // tm
    assert (bsz * nt) % TAIL_SUB == 0
    step = TAIL_SUB * tm
    flat = lambda a: a.reshape(bsz * s, a.shape[-1])
    rowp = lambda w: pl.BlockSpec((step, w), lambda p: (p, 0))
    full = lambda shape: pl.BlockSpec(shape, lambda p: (0,) * len(shape))

    def mod_spec(sub):
        def index(p):
            q = p * TAIL_SUB + sub
            return (q // nt, (q % nt >= nct).astype(jnp.int32), 0, 0)
        return pl.BlockSpec((1, 1, 8, d), index)

    tail_in, out_specs, out_shape = _tail_specs(bsz * s, tm, d)
    return pl.pallas_call(
        _even_out_kernel,
        grid=(bsz * nt // TAIL_SUB,),
        in_specs=[pl.BlockSpec((2, step, MLSTM_W), lambda p: (0, p, 0)),
                  rowp(LRU_W), rowp(LRU_W), rowp(MLSTM_W), rowp(LRU_W),
                  full((1, MLSTM_W)), full((MLSTM_W + LRU_W, d)), rowp(d)]
                 + [mod_spec(sub) for sub in range(TAIL_SUB)] + [full((1, d))] + tail_in,
        out_specs=out_specs,
        out_shape=out_shape,
        compiler_params=_cparams(("arbitrary",), 56),
        name="even_out",
    )(hm.reshape(2, bsz * s, MLSTM_W), flat(hl0), flat(hl1), flat(o_pre), flat(yg),
      mnorm_g.reshape(1, MLSTM_W), w_out, flat(x), *([mod] * TAIL_SUB), n2.reshape(1, d),
      w_r.T, b_r.reshape(N_EXPERTS, 1))


def _odd_out_kernel(a_ref, wout_ref, *rest):
    x_refs = rest[:TAIL_SUB]
    mod_refs = rest[TAIL_SUB:2 * TAIL_SUB]
    n2_ref, wr_ref, br_ref, xo_ref, v_ref, route_ref, cnt_ref = rest[2 * TAIL_SUB:]
    tm = a_ref.shape[0] // TAIL_SUB
    for sub in range(TAIL_SUB):
        y = _dot(a_ref[sub * tm:(sub + 1) * tm, :], wout_ref[...])
        _tail(sub, y, x_refs[sub][...], mod_refs[sub][0, 0], n2_ref, wr_ref, br_ref, xo_ref, v_ref, route_ref,
              cnt_ref)


def _odd_out(attn, w_out, x, mod, n2, w_r, b_r, n_ctx):
    bsz, seq, d = attn.shape
    s = x.shape[1]
    tm = ROW_TILE
    ntl = seq // tm
    nta = s // tm
    nct = n_ctx // tm
    assert (bsz * ntl) % TAIL_SUB == 0
    step = TAIL_SUB * tm
    full = lambda shape: pl.BlockSpec(shape, lambda p: (0,) * len(shape))

    def x_spec(sub):
        def index(p):
            q = p * TAIL_SUB + sub
            return ((q // ntl) * nta + nct + q % ntl, 0)
        return pl.BlockSpec((tm, d), index)

    def mod_spec(sub):
        return pl.BlockSpec((1, 1, 8, d), lambda p: ((p * TAIL_SUB + sub) // ntl, 1, 0, 0))

    tail_in, out_specs, out_shape = _tail_specs(bsz * seq, tm, d)
    xf = x.reshape(bsz * s, d)
    return pl.pallas_call(
        _odd_out_kernel,
        grid=(bsz * ntl // TAIL_SUB,),
        in_specs=[pl.BlockSpec((step, d), lambda p: (p, 0)), full((d, d))]
                 + [x_spec(sub) for sub in range(TAIL_SUB)] + [mod_spec(sub) for sub in range(TAIL_SUB)]
                 + [full((1, d))] + tail_in,
        out_specs=out_specs,
        out_shape=out_shape,
        compiler_params=_cparams(("arbitrary",), 48),
        name="odd_out",
    )(attn.reshape(bsz * seq, d), w_out, *([xf] * TAIL_SUB), *([mod] * TAIL_SUB), n2.reshape(1, d),
      w_r.T, b_r.reshape(N_EXPERTS, 1))


def _expert_kernel(te_ref, nu_ref, x_ref, w1_ref, b1_ref, w2_ref, b2_ref, rw_ref, *rest):
    o_ref, w1b, w2b = rest[-3:]
    t = pl.program_id(0)
    tm = x_ref.shape[0]
    used = t < nu_ref[0]
    new_expert = jnp.logical_or(t == 0, te_ref[t] != te_ref[jnp.maximum(t - 1, 0)])

    @pl.when(jnp.logical_and(used, new_expert))
    def _():
        w1b[...] = w1_ref[0, 0].astype(BF16)
        w2b[...] = w2_ref[0, 0].astype(BF16)

    @pl.when(used)
    def _():
        hid = _dot(x_ref[...], w1b[...]) + b1_ref[0, 0]
        gate = jnp.minimum(hid[:, :D_FF], SWIGLU_LIMIT)
        up = jnp.clip(hid[:, D_FF:], -SWIGLU_LIMIT, SWIGLU_LIMIT)
        act = (up + 1.0) * gate * _sigmoid(SWIGLU_ALPHA * gate)
        y = _dot(act.astype(BF16), w2b[...]) + b2_ref[0, 0]
        w_col = jnp.transpose(jnp.broadcast_to(rw_ref[0], (V7X_LANES, tm)))[:, 0:1]
        o_ref[...] = (y * w_col).astype(o_ref.dtype)

    @pl.when(jnp.logical_not(used))
    def _():
        o_ref[...] = jnp.zeros_like(o_ref)


def _experts(xs, row_w, tile_expert, n_used, layer, w1, b1, w2, b2, chunk, ys_buf):
    rows, d = xs.shape
    nt, _, tm = row_w.shape
    nl, ne, _, ff2 = w1.shape
    in_specs = [pl.BlockSpec((tm, d), lambda t, te, nu: (t, 0)),
                pl.BlockSpec((1, 1, d, ff2), lambda t, te, nu: (layer, te[t], 0, 0)),
                pl.BlockSpec((1, 1, 1, ff2), lambda t, te, nu: (layer, te[t], 0, 0)),
                pl.BlockSpec((1, 1, ff2 // 2, d), lambda t, te, nu: (layer, te[t], 0, 0)),
                pl.BlockSpec((1, 1, 1, d), lambda t, te, nu: (layer, te[t], 0, 0)),
                pl.BlockSpec((1, 1, tm), lambda t, te, nu: (t, 0, 0))]
    args = [tile_expert, n_used, xs, w1, b1.reshape(nl, ne, 1, ff2), w2, b2.reshape(nl, ne, 1, d), row_w]
    aliases = {}
    if ys_buf is not None:
        in_specs.append(pl.BlockSpec(memory_space=pl.ANY))
        aliases = {len(args): 0}
        args.append(ys_buf)
    return pl.pallas_call(
        _expert_kernel,
        grid_spec=pltpu.PrefetchScalarGridSpec(
            num_scalar_prefetch=2,
            grid=(nt,),
            in_specs=in_specs,
            out_specs=pl.BlockSpec((tm, d), lambda t, te, nu: (chunk * nt + t, 0)),
            scratch_shapes=[pltpu.VMEM((d, ff2), BF16), pltpu.VMEM((ff2 // 2, d), BF16)],
        ),
        out_shape=jax.ShapeDtypeStruct((MOE_CHUNKS * rows, d), BF16),
        input_output_aliases=aliases,
        compiler_params=_cparams(("arbitrary",), 56),
        name="moe_experts",
    )(*args)


def _moe(v, route, counts, layer, w1, b1, w2, b2):
    t, d = v.shape
    tm = MOE_TILE
    nrows = t * TOP_K
    ntc = -(-(-(-nrows // tm) + N_EXPERTS) // MOE_CHUNKS)
    nt = ntc * MOE_CHUNKS
    idx = route[0:TOP_K].astype(jnp.int32)
    rank = route[TOP_K:2 * TOP_K].astype(jnp.int32)
    weight = route[2 * TOP_K:3 * TOP_K]
    sizes = counts[:, 0].astype(jnp.int32)
    start = jnp.cumsum(sizes) - sizes
    padded = (sizes + tm - 1) // tm * tm
    pad_end = jnp.cumsum(padded)
    pad_start = pad_end - padded
    pair_pos = rank
    for e in range(N_EXPERTS):
        pair_pos = pair_pos + jnp.where(idx == e, pad_start[e], 0)
    bits = max(1, (nrows - 1).bit_length())
    assert N_EXPERTS << bits < 2 ** 31
    pair_id = (jnp.arange(t, dtype=jnp.int32)[None, :] * TOP_K + jnp.arange(TOP_K, dtype=jnp.int32)[:, None])
    order = jnp.sort(((idx << bits) + pair_id).reshape(-1)) & ((1 << bits) - 1)
    tile_row0 = jnp.arange(nt, dtype=jnp.int32) * tm
    tile_expert = jnp.minimum(jnp.sum(pad_end[None, :] <= tile_row0[:, None], axis=1, dtype=jnp.int32),
                              N_EXPERTS - 1)
    n_used = (pad_end[-1] // tm).reshape(1).astype(jnp.int32)
    onehot_te = tile_expert[:, None] == jnp.arange(N_EXPERTS, dtype=jnp.int32)[None, :]
    pick = lambda tbl: jnp.sum(jnp.where(onehot_te, tbl[None, :], 0), axis=1)
    r_in = tile_row0[:, None] + jnp.arange(tm, dtype=jnp.int32)[None, :] - pick(pad_start)[:, None]
    valid = r_in < pick(sizes)[:, None]
    src = jnp.where(valid, pick(start)[:, None] + r_in, 0).reshape(-1)
    pair = order.at[src].get(mode='promise_in_bounds')
    row_token = pair // TOP_K
    flat_w = weight.reshape(-1).at[(pair % TOP_K) * t + row_token].get(mode='promise_in_bounds')
    row_w = jnp.where(valid, flat_w.reshape(nt, tm), 0.0)
    row_token = row_token.reshape(MOE_CHUNKS, ntc * tm)
    row_w = row_w.reshape(MOE_CHUNKS, ntc, 1, tm)
    tile_expert = tile_expert.reshape(MOE_CHUNKS, ntc)
    ys = None
    for c in range(MOE_CHUNKS):
        xs = v.at[row_token[c]].get(mode='promise_in_bounds')
        ys = _experts(xs, row_w[c], tile_expert[c], jnp.clip(n_used - c * ntc, 0, ntc), layer,
                      w1, b1, w2, b2, c, ys)
    return [ys.at[pair_pos[kk]].get(mode='promise_in_bounds') for kk in range(TOP_K)]


def _rope(t, cos, sin, lane_lo):
    swapped = jnp.where(lane_lo, pltpu.roll(t, ATT_DH - ROPE_AXIS_DIM // 2, 1),
                        pltpu.roll(t, ROPE_AXIS_DIM // 2, 1))
    return t * cos + swapped * sin


def _proj_odd_kernel(x_ref, f0, f1, f2, f3, mod0_ref, mod_ref, g_ref, w_ref, qg_ref, kg_ref,
                     cos_ref, sin_ref, h_ref, q_ref, k_ref, v_ref):
    f = (f0[0].astype(F32) + f1[0].astype(F32)) + (f2[0].astype(F32) + f3[0].astype(F32))
    hcur = x_ref[0] + mod0_ref[0, 0][5:6] * f
    h_ref[0] = hcur
    mod = mod_ref[0, 0]
    u = _modnorm(hcur, g_ref[...], mod[1:2], mod[0:1]).astype(BF16)
    z = _dot(u, w_ref[...])
    cos = cos_ref[...]
    sin = sin_ref[...]
    lane = lax.broadcasted_iota(jnp.int32, cos.shape, 1)
    lane_lo = (lane % ROPE_AXIS_DIM) < (ROPE_AXIS_DIM // 2)
    qw = ATT_HEADS * ATT_DH
    kw = ATT_KV_HEADS * ATT_DH
    for hh in range(ATT_HEADS):
        sl = slice(hh * ATT_DH, (hh + 1) * ATT_DH)
        t = _rope(_rms(z[:, sl], qg_ref[...]), cos, sin, lane_lo)
        q_ref[0, :, sl] = (t * (ATT_DH ** -0.5 * LOG2_E)).astype(q_ref.dtype)
    for hh in range(ATT_KV_HEADS):
        sl = slice(hh * ATT_DH, (hh + 1) * ATT_DH)
        t = _rope(_rms(z[:, qw + hh * ATT_DH:qw + (hh + 1) * ATT_DH], kg_ref[...]), cos, sin, lane_lo)
        k_ref[0, :, sl] = t.astype(k_ref.dtype)
    v_ref[0] = z[:, qw + kw:qw + 2 * kw].astype(v_ref.dtype)


def _proj_odd(x, fparts, mod0, mod, g, w, qg, kg, cos_tab, sin_tab, n_ctx):
    bsz, s, d = x.shape
    tm = ROW_TILE
    nt = s // tm
    nct = n_ctx // tm
    n = w.shape[1]
    qw = ATT_HEADS * ATT_DH
    kw = ATT_KV_HEADS * ATT_DH
    row = lambda b, i: (b, i, 0)
    seg = lambda b, i: (b, (i >= nct).astype(jnp.int32), 0, 0)
    full = lambda shape: pl.BlockSpec(shape, lambda b, i: (0,) * len(shape))
    tab = pl.BlockSpec((tm, ATT_DH), lambda b, i: (i, 0))
    return pl.pallas_call(
        _proj_odd_kernel,
        grid=(bsz, nt),
        in_specs=[pl.BlockSpec((1, tm, d), row)] + [pl.BlockSpec((1, tm, d), row)] * TOP_K
                 + [pl.BlockSpec((1, 1, 8, d), seg), pl.BlockSpec((1, 1, 8, d), seg),
                    full((1, d)), full((d, n)), full((1, ATT_DH)), full((1, ATT_DH)), tab, tab],
        out_specs=[pl.BlockSpec((1, tm, d), row), pl.BlockSpec((1, tm, qw), row),
                   pl.BlockSpec((1, tm, kw), row), pl.BlockSpec((1, tm, kw), row)],
        out_shape=[jax.ShapeDtypeStruct((bsz, s, d), F32),
                   jax.ShapeDtypeStruct((bsz, s, qw), BF16),
                   jax.ShapeDtypeStruct((bsz, s, kw), BF16),
                   jax.ShapeDtypeStruct((bsz, s, kw), BF16)],
        compiler_params=_cparams(("arbitrary", "arbitrary"), 48),
        name="proj_odd",
    )(x, *fparts, mod0, mod, g.reshape(1, d), w, qg.reshape(1, ATT_DH), kg.reshape(1, ATT_DH),
      cos_tab, sin_tab)


def _rope_tables(n_ctx, seq):
    rows = seq // GRID_W
    pos_r = jnp.repeat(jnp.arange(rows), GRID_W).astype(F32)
    pos_c = jnp.tile(jnp.arange(GRID_W), rows).astype(F32)
    inv_freq = ROPE_THETA ** (-jnp.arange(0, ROPE_AXIS_DIM, 2, dtype=F32) / ROPE_AXIS_DIM)
    ar = pos_r[:, None] * inv_freq
    ac = pos_c[:, None] * inv_freq
    cos = jnp.concatenate([jnp.cos(ar), jnp.cos(ar), jnp.cos(ac), jnp.cos(ac)], axis=-1)
    sin = jnp.concatenate([-jnp.sin(ar), jnp.sin(ar), -jnp.sin(ac), jnp.sin(ac)], axis=-1)
    cos = jnp.concatenate([jnp.ones((n_ctx, ATT_DH), F32), cos], axis=0)
    sin = jnp.concatenate([jnp.zeros((n_ctx, ATT_DH), F32), sin], axis=0)
    return cos, sin


def _attn_kernel(q_ref, k_ref, v_ref, o_ref):
    k = k_ref[0]
    v = v_ref[0]
    for g in range(ATT_GROUP):
        sl = slice(g * ATT_DH, (g + 1) * ATT_DH)
        s = _dot_nt(q_ref[0, :, sl], k)
        p = jnp.exp2(s - jnp.max(s, axis=1, keepdims=True))
        l = jnp.sum(p, axis=1, keepdims=True)
        o_ref[0, :, sl] = (_dot(p.astype(BF16), v) / l).astype(o_ref.dtype)


def _attention(q, k, v, n_ctx):
    bsz, s, qw = q.shape
    seq = s - n_ctx
    tq = ATT_Q_TILE
    nct = n_ctx // tq
    gw = ATT_GROUP * ATT_DH
    return pl.pallas_call(
        _attn_kernel,
        grid=(bsz, ATT_KV_HEADS, seq // tq),
        in_specs=[pl.BlockSpec((1, tq, gw), lambda b, h, i: (b, i + nct, h)),
                  pl.BlockSpec((1, s, ATT_DH), lambda b, h, i: (b, 0, h)),
                  pl.BlockSpec((1, s, ATT_DH), lambda b, h, i: (b, 0, h))],
        out_specs=pl.BlockSpec((1, tq, gw), lambda b, h, i: (b, i, h)),
        out_shape=jax.ShapeDtypeStruct((bsz, seq, qw), BF16),
        compiler_params=_cparams(("arbitrary", "arbitrary", "arbitrary"), 48),
        name="attention",
    )(q, k, v)


def _final_kernel(x_ref, f0, f1, f2, f3, mod_ref, g_ref, o_ref):
    f = (f0[0].astype(F32) + f1[0].astype(F32)) + (f2[0].astype(F32) + f3[0].astype(F32))
    o_ref[0] = _rms(x_ref[0] + mod_ref[0, 0][5:6] * f, g_ref[...])


def _final(x, fparts, mod, g):
    bsz, seq, d = x.shape
    tm = ROW_TILE
    row = lambda b, i: (b, i, 0)
    return pl.pallas_call(
        _final_kernel,
        grid=(bsz, seq // tm),
        in_specs=[pl.BlockSpec((1, tm, d), row)] * (1 + TOP_K)
                 + [pl.BlockSpec((1, 1, 8, d), lambda b, i: (b, 1, 0, 0)),
                    pl.BlockSpec((1, d), lambda b, i: (0, 0))],
        out_specs=pl.BlockSpec((1, tm, d), row),
        out_shape=jax.ShapeDtypeStruct((bsz, seq, d), F32),
        compiler_params=_cparams(("arbitrary", "arbitrary"), 32),
        name="final_norm",
    )(x, *fparts, mod, g.reshape(1, d))


def _pack_even_w_in(w_in):
    w4 = 4 * MLSTM_W
    ng = 4 * MLSTM_HEADS
    wg = w_in[:, w4:w4 + ng]
    half = ng // 2
    pad = jnp.zeros((w_in.shape[0], GATE_PAD - half), w_in.dtype)
    packed = jnp.concatenate([w_in[:, :w4], w_in[:, w4 + ng:], wg[:, :half], pad, wg[:, half:], pad], axis=1)
    return packed.astype(BF16), wg.T.astype(BF16)


def kernel(x, c, ctx, c_ctx, mod_w, mod_b, norm1_g, norm2_g, final_g, ev_w_in, ev_qk_conv_w, ev_qk_conv_b, ev_gate_b, ev_mnorm_g, ev_lru_conv_w, ev_lru_conv_b, ev_lru_wa, ev_lru_ba, ev_lru_wx, ev_lru_bx, ev_lru_lam, ev_w_out, od_w_in, od_q_norm_g, od_k_norm_g, od_w_out, moe_w_r, moe_b_r, moe_w1, moe_b1, moe_w2, moe_b2):
    bsz, seq, d = x.shape
    n_ctx = ctx.shape[1]
    s = n_ctx + seq
    assert n_ctx % ROW_TILE == 0 and seq % ROW_TILE == 0 and seq % GRID_W == 0
    h = jnp.concatenate([ctx, x], axis=1)

    mod0 = _mod_table(c, c_ctx, mod_w[0], mod_b[0])
    w_packed, wg_t = _pack_even_w_in(ev_w_in[0])
    q, k, v, o_pre, xc, yg, gc, gr = _proj_even(h, mod0, norm1_g[0], w_packed, wg_t, ev_qk_conv_w[0], ev_qk_conv_b[0],
                                                ev_lru_conv_w[0], ev_lru_conv_b[0], n_ctx)
    hm = _mlstm(q, k, v, gc, gr, ev_gate_b[0], n_ctx)
    hl = [_lru(xc, ev_lru_wa[0, dd], ev_lru_wx[0, dd], ev_lru_ba[0, dd], ev_lru_bx[0, dd],
               ev_lru_lam[0, dd], n_ctx, dd == 1) for dd in range(2)]
    x_mid, v0, route0, cnt0 = _even_out(hm, hl[0], hl[1], o_pre, yg, ev_mnorm_g[0], ev_w_out[0].astype(BF16),
                                        h, mod0, norm2_g[0], moe_w_r[0], moe_b_r[0], n_ctx)
    f0 = _moe(v0.reshape(bsz * s, d), route0, cnt0, 0, moe_w1, moe_b1, moe_w2, moe_b2)
    f0 = [p.reshape(bsz, s, d) for p in f0]

    mod1 = _mod_table(c, c_ctx, mod_w[1], mod_b[1])
    cos_tab, sin_tab = _rope_tables(n_ctx, seq)
    h1, q1, k1, v1 = _proj_odd(x_mid.reshape(bsz, s, d), f0, mod0, mod1, norm1_g[1], od_w_in[0].astype(BF16),
                               od_q_norm_g[0], od_k_norm_g[0], cos_tab, sin_tab, n_ctx)
    attn = _attention(q1, k1, v1, n_ctx)
    x2, v2, route2, cnt2 = _odd_out(attn, od_w_out[0].astype(BF16), h1, mod1, norm2_g[1],
                                    moe_w_r[1], moe_b_r[1], n_ctx)
    f1 = _moe(v2.reshape(bsz * seq, d), route2, cnt2, 1, moe_w1, moe_b1, moe_w2, moe_b2)
    f1 = [p.reshape(bsz, seq, d) for p in f1]
    return _final(x2.reshape(bsz, seq, d), f1, mod1, final_g)
```

```python
import functools

import jax
import jax.numpy as jnp
from jax import lax
from jax.experimental import pallas as pl
from jax.experimental.pallas import tpu as pltpu

F32 = jnp.float32
BF16 = jnp.bfloat16
HIGHEST = lax.Precision.HIGHEST

EPS = 1e-6
M_INIT = -1e30
NEG_BIG = -1e30

MLSTM_HEADS = 4
MLSTM_DH = 256
MLSTM_W = MLSTM_HEADS * MLSTM_DH
LRU_W = 1024
LRU_BLOCKS = 16
LRU_BW = LRU_W // LRU_BLOCKS
LRU_C = 8.0
CONV_W = 4
CONV_LEFT = 2
ATT_HEADS = 8
ATT_KV_HEADS = 2
ATT_GROUP = ATT_HEADS // ATT_KV_HEADS
ATT_DH = 128
GRID_W = 64
ROPE_AXIS_DIM = ATT_DH // 2
ROPE_THETA = 10000.0
N_EXPERTS = 32
TOP_K = 4
D_FF = 1024
SWIGLU_ALPHA = 1.702
SWIGLU_LIMIT = 7.0
LOG2_E = 1.4426950408889634

V7X_LANES = 128
V7X_MXU_DIM = 256
V7X_VMEM_BYTES = 64 * 1024 * 1024
MIB = 1024 * 1024

ROW_TILE = 256
HALO = 16
TAIL_SUB = 1
MOE_TILE = 256
MOE_CHUNKS = 8
ROUTE_ROWS = 16
ATT_Q_TILE = 256
GATE_PAD = V7X_LANES


def _cparams(semantics, vmem_mib):
    assert vmem_mib * MIB < V7X_VMEM_BYTES
    return pltpu.CompilerParams(dimension_semantics=semantics, vmem_limit_bytes=vmem_mib * MIB)


def _dot(a, b):
    return jnp.dot(a, b, preferred_element_type=F32)


def _dot_nt(a, b, precision=None):
    return lax.dot_general(a, b, (((1,), (1,)), ((), ())), precision=precision,
                           preferred_element_type=F32)


def _dot_tn(a, b):
    return lax.dot_general(a, b, (((0,), (0,)), ((), ())), preferred_element_type=F32)


def _sigmoid(x):
    return 0.5 * jnp.tanh(0.5 * x) + 0.5


def _log_sigmoid(x):
    return jnp.minimum(x, 0.0) - jnp.log1p(jnp.exp(-jnp.abs(x)))


def _softplus(x):
    return jnp.maximum(x, 0.0) + jnp.log1p(jnp.exp(-jnp.abs(x)))


def _gelu_tanh(x):
    return 0.5 * x * (1.0 + jnp.tanh(0.7978845608028654 * (x + 0.044715 * x * x * x)))


def _rms(x, g):
    return x * lax.rsqrt(jnp.mean(x * x, axis=-1, keepdims=True) + EPS) * g


def _modnorm(x, g, scale, shift):
    return _rms(x, g) * (1.0 + scale) + shift


def _modvec_kernel(c_ref, w_ref, b_ref, o_ref):
    c = c_ref[...]
    s = c * _sigmoid(c)
    o_ref[...] = jnp.dot(s, w_ref[...], precision=HIGHEST, preferred_element_type=F32) + b_ref[...]


def _modvec(cc, w, b):
    rows, d = cc.shape
    n = w.shape[1]
    tn = 1536
    return pl.pallas_call(
        _modvec_kernel,
        grid=(n // tn,),
        in_specs=[pl.BlockSpec((rows, d), lambda j: (0, 0)),
                  pl.BlockSpec((d, tn), lambda j: (0, j)),
                  pl.BlockSpec((1, tn), lambda j: (0, j))],
        out_specs=pl.BlockSpec((rows, tn), lambda j: (0, j)),
        out_shape=jax.ShapeDtypeStruct((rows, n), F32),
        compiler_params=_cparams(("arbitrary",), 32),
        name="modvec",
    )(cc, w, b.reshape(1, n))


def _mod_table(c, c_ctx, mod_w, mod_b):
    bsz, d = c.shape
    rows = ((bsz + 1 + 7) // 8) * 8
    cc = jnp.zeros((rows, d), F32).at[:bsz].set(c).at[bsz].set(c_ctx)
    mod = _modvec(cc, mod_w, mod_b)
    lat = mod[:bsz].reshape(bsz, 6, d)
    ctx = jnp.broadcast_to(mod[bsz].reshape(1, 6, d), (bsz, 6, d))
    tbl = jnp.stack([ctx, lat], axis=1)
    return jnp.pad(tbl, ((0, 0), (0, 0), (0, 2), (0, 0)))


def _proj_even_kernel(nct, x_ref, xp_ref, xn_ref, mod_ref, g_ref, w_ref, wgt_ref,
                      wqk_ref, bqk_ref, wxr_ref, bxr_ref,
                      q_ref, k_ref, v_ref, o_ref, xc_ref, yg_ref, gc_ref, gr_ref):
    i = pl.program_id(1)
    nt = pl.num_programs(1)
    tm = x_ref.shape[1]
    w = MLSTM_W
    first = jnp.logical_or(i == 0, i == nct)
    last = jnp.logical_or(i == nct - 1, i == nt - 1)
    mod = mod_ref[0, 0]
    xe = jnp.concatenate([xp_ref[0], x_ref[0], xn_ref[0]], axis=0)
    ue = _modnorm(xe, g_ref[...], mod[1:2], mod[0:1]).astype(BF16)
    rowi = lax.broadcasted_iota(jnp.int32, (tm + 2 * HALO, 1), 0)
    keep = jnp.where(rowi < HALO, jnp.where(first, 0.0, 1.0),
                     jnp.where(rowi >= HALO + tm, jnp.where(last, 0.0, 1.0), 1.0))

    te = tm + 2 * HALO

    def conv(z, w_ref, b_ref):
        z = z * keep
        acc = b_ref[...]
        for j in range(CONV_W):
            shift = (CONV_LEFT - j) % te
            zj = pltpu.roll(z, shift, 0) if shift else z
            acc = acc + w_ref[j:j + 1, :] * zj[HALO:HALO + tm]
        return acc

    y = conv(_dot(ue, w_ref[:, 0:2 * w]), wqk_ref, bqk_ref)
    y = y * _sigmoid(y)
    q_ref[0] = y[:, :w].astype(q_ref.dtype)
    k_ref[0] = (y[:, w:] * (MLSTM_DH ** -0.5)).astype(k_ref.dtype)
    xc_ref[0] = conv(_dot(ue, w_ref[:, 4 * w:4 * w + LRU_W]), wxr_ref, bxr_ref)
    u = ue[HALO:HALO + tm]
    v_ref[0] = _dot(u, w_ref[:, 2 * w:3 * w]).astype(v_ref.dtype)
    o_ref[0] = _dot(u, w_ref[:, 3 * w:4 * w]).astype(o_ref.dtype)
    yg_ref[0] = _dot(u, w_ref[:, 4 * w + LRU_W:4 * w + 2 * LRU_W]).astype(yg_ref.dtype)
    gc_ref[0] = _dot(u, w_ref[:, 4 * w + 2 * LRU_W:4 * w + 2 * LRU_W + 2 * GATE_PAD])
    gr_ref[0] = _dot_nt(wgt_ref[...], u)


def _proj_even(h, mod, g, w_packed, wg_t, wqk, bqk, wxr, bxr, n_ctx):
    bsz, s, d = h.shape
    tm = ROW_TILE
    nt = s // tm
    nct = n_ctx // tm
    ntot = w_packed.shape[1]
    ng = wg_t.shape[0]
    hb = tm // HALO
    nhb = s // HALO
    cq = 2 * MLSTM_W
    row = lambda b, i: (b, i, 0)
    prev = lambda b, i: (b, jnp.maximum(i * hb - 1, 0), 0)
    nxt = lambda b, i: (b, jnp.minimum((i + 1) * hb, nhb - 1), 0)
    full = lambda shape: pl.BlockSpec(shape, lambda b, i: (0, 0))
    return pl.pallas_call(
        functools.partial(_proj_even_kernel, nct),
        grid=(bsz, nt),
        in_specs=[pl.BlockSpec((1, tm, d), row), pl.BlockSpec((1, HALO, d), prev),
                  pl.BlockSpec((1, HALO, d), nxt),
                  pl.BlockSpec((1, 1, 8, d), lambda b, i: (b, (i >= nct).astype(jnp.int32), 0, 0)),
                  full((1, d)), full((d, ntot)), full((ng, d)),
                  full((CONV_W, cq)), full((1, cq)), full((CONV_W, LRU_W)), full((1, LRU_W))],
        out_specs=[pl.BlockSpec((1, tm, MLSTM_W), row),
                   pl.BlockSpec((1, tm, MLSTM_W), row),
                   pl.BlockSpec((1, tm, MLSTM_W), row),
                   pl.BlockSpec((1, tm, MLSTM_W), row),
                   pl.BlockSpec((1, tm, LRU_W), row),
                   pl.BlockSpec((1, tm, LRU_W), row),
                   pl.BlockSpec((1, tm, 2 * GATE_PAD), row),
                   pl.BlockSpec((1, ng, tm), lambda b, i: (b, 0, i))],
        out_shape=[jax.ShapeDtypeStruct((bsz, s, MLSTM_W), BF16),
                   jax.ShapeDtypeStruct((bsz, s, MLSTM_W), BF16),
                   jax.ShapeDtypeStruct((bsz, s, MLSTM_W), BF16),
                   jax.ShapeDtypeStruct((bsz, s, MLSTM_W), BF16),
                   jax.ShapeDtypeStruct((bsz, s, LRU_W), F32),
                   jax.ShapeDtypeStruct((bsz, s, LRU_W), BF16),
                   jax.ShapeDtypeStruct((bsz, s, 2 * GATE_PAD), F32),
                   jax.ShapeDtypeStruct((bsz, ng, s), F32)],
        compiler_params=_cparams(("arbitrary", "arbitrary"), 56),
        name="proj_even",
    )(h, h, h, mod, g.reshape(1, d), w_packed, wg_t, wqk, bqk.reshape(1, cq), wxr, bxr.reshape(1, LRU_W))


def _mlstm_kernel(q_ref, k_ref, v_ref, gc_ref, gr_ref, bc_ref, br_ref, h_ref, c_scr, n_scr, m_scr):
    d = pl.program_id(1)
    j = pl.program_id(2)
    lc = q_ref.shape[1]
    nh = MLSTM_HEADS
    dh = MLSTM_DH

    @pl.when(j == 0)
    def _():
        c_scr[...] = jnp.zeros_like(c_scr)
        n_scr[...] = jnp.zeros_like(n_scr)
        m_scr[...] = jnp.full(m_scr.shape, M_INIT, F32)

    row = lax.broadcasted_iota(jnp.int32, (lc, lc), 0)
    col = lax.broadcasted_iota(jnp.int32, (lc, lc), 1)
    lo = jnp.where(d == 1, row, col)
    hi = jnp.where(d == 1, col, row)
    tri = lo <= hi
    trif = tri.astype(F32)

    gc = gc_ref[0] + bc_ref[0]
    gr = gr_ref[0, 0] + br_ref[0]
    lfr = _log_sigmoid(gr)
    bcum_r = _dot_nt(lfr, trif, precision=HIGHEST)
    bcum_c = jnp.transpose(jnp.concatenate([bcum_r, jnp.zeros((GATE_PAD - 2 * nh, lc), F32)], axis=0))

    for h in range(nh):
        sl = slice(h * dh, (h + 1) * dh)
        q = q_ref[0, :, sl]
        k = k_ref[0, :, sl]
        v = v_ref[0, :, sl]
        i_col = gc[:, h:h + 1]
        b_col = bcum_c[:, nh + h:nh + h + 1]
        i_row = gr[h:h + 1, :]
        b_row = bcum_r[nh + h:nh + h + 1, :]
        m_prev = m_scr[h][0:1, 0:1]
        c_mat = c_scr[h]
        n_vec = n_scr[h]

        log_intra = jnp.where(tri, b_col - b_row + i_row, NEG_BIG)
        log_inter = b_col + m_prev
        m_t = jnp.maximum(log_inter, jnp.max(log_intra, axis=1, keepdims=True))
        w_inter = jnp.exp(log_inter - m_t)
        scores = _dot_nt(q, k) * jnp.exp(log_intra - m_t)
        num = w_inter * _dot(q, c_mat.astype(BF16)) + _dot(scores.astype(BF16), v)
        den = (w_inter * jnp.sum(q.astype(F32) * n_vec, axis=1, keepdims=True)
               + jnp.sum(scores, axis=1, keepdims=True))
        hh = num / jnp.maximum(jnp.abs(den), jnp.exp(-m_t))
        h_ref[0, 0, :, sl] = hh.astype(h_ref.dtype)

        total_f = jnp.sum(lfr[nh + h:nh + h + 1, :], axis=1, keepdims=True)
        log_w_row = total_f - b_row + i_row
        m_new = jnp.maximum(total_f + m_prev, jnp.max(log_w_row, axis=1, keepdims=True))
        decay = jnp.exp(total_f + m_prev - m_new)
        w_col = jnp.exp(total_f - b_col + i_col - m_new)
        wv = (w_col * v.astype(F32)).astype(BF16)
        c_scr[h] = decay * c_mat + _dot_tn(k, wv)
        n_scr[h] = decay * n_vec + jnp.sum(w_col * k.astype(F32), axis=0, keepdims=True)
        m_scr[h] = jnp.broadcast_to(m_new, m_scr.shape[1:])


def _chunk_order(n_ctx_chunks, nchunks):
    def order(d, j):
        bwd = jnp.where(j < n_ctx_chunks, n_ctx_chunks - 1 - j, nchunks - 1 - (j - n_ctx_chunks))
        return jnp.where(d == 0, j, bwd)
    return order


def _mlstm(q, k, v, gc, gr, gate_b, n_ctx):
    bsz, s, w = q.shape
    lc = ROW_TILE
    nchunks = s // lc
    order = _chunk_order(n_ctx // lc, nchunks)
    nh = MLSTM_HEADS
    gb = gate_b.reshape(2, 2 * nh)
    bc = jnp.pad(gb, ((0, 0), (0, GATE_PAD - 2 * nh))).reshape(2, 1, GATE_PAD)
    br = gb.reshape(2, 2 * nh, 1)
    gr4 = gr.reshape(bsz, 2, 2 * nh, s)
    row = lambda b, d, j: (b, order(d, j), 0)
    return pl.pallas_call(
        _mlstm_kernel,
        grid=(bsz, 2, nchunks),
        in_specs=[pl.BlockSpec((1, lc, w), row), pl.BlockSpec((1, lc, w), row),
                  pl.BlockSpec((1, lc, w), row),
                  pl.BlockSpec((1, lc, GATE_PAD), lambda b, d, j: (b, order(d, j), d)),
                  pl.BlockSpec((1, 1, 2 * nh, lc), lambda b, d, j: (b, d, 0, order(d, j))),
                  pl.BlockSpec((1, 1, GATE_PAD), lambda b, d, j: (d, 0, 0)),
                  pl.BlockSpec((1, 2 * nh, 1), lambda b, d, j: (d, 0, 0))],
        out_specs=pl.BlockSpec((1, 1, lc, w), lambda b, d, j: (d, b, order(d, j), 0)),
        out_shape=jax.ShapeDtypeStruct((2, bsz, s, w), BF16),
        scratch_shapes=[pltpu.VMEM((nh, MLSTM_DH, MLSTM_DH), F32),
                        pltpu.VMEM((nh, 1, MLSTM_DH), F32),
                        pltpu.VMEM((nh, 8, V7X_LANES), F32)],
        compiler_params=_cparams(("arbitrary", "arbitrary", "arbitrary"), 48),
        name="mlstm",
    )(q, k, v, gc, gr4, bc, br)


def _lru_kernel(reverse, x_ref, wa_ref, wx_ref, ba_ref, bx_ref, lam_ref, h_ref, a_scr, b_scr, carry):
    j = pl.program_id(1)
    t_rows = x_ref.shape[1]
    bw = V7X_MXU_DIM

    @pl.when(j == 0)
    def _():
        carry[...] = jnp.zeros_like(carry)

    x = x_ref[0]
    xb = x.astype(BF16)
    sp = _softplus(-lam_ref[...])
    for jj in range(LRU_W // bw):
        sl = slice(jj * bw, (jj + 1) * bw)
        r = _sigmoid(_dot(xb[:, sl], wa_ref[jj]) + ba_ref[:, sl])
        gi = _sigmoid(_dot(xb[:, sl], wx_ref[jj]) + bx_ref[:, sl])
        a = jnp.exp(-LRU_C * r * sp[:, sl])
        a_scr[:, sl] = a
        b_scr[:, sl] = jnp.sqrt(1.0 - a * a) * gi * x[:, sl]

    def body(t, hc):
        tt = t_rows - 1 - t if reverse else t
        hn = a_scr[pl.ds(tt, 1), :] * hc + b_scr[pl.ds(tt, 1), :]
        b_scr[pl.ds(tt, 1), :] = hn
        return hn

    carry[...] = lax.fori_loop(0, t_rows, body, carry[...], unroll=8)
    h_ref[0] = b_scr[...].astype(h_ref.dtype)


def _lru_blockdiag(w):
    per = V7X_MXU_DIM // LRU_BW
    nt = LRU_BLOCKS // per
    w4 = w.reshape(nt, per, LRU_BW, LRU_BW)
    eye = jnp.eye(per, dtype=w.dtype)
    t = jnp.einsum('tpcd,pq->tpcqd', w4, eye)
    return t.reshape(nt, V7X_MXU_DIM, V7X_MXU_DIM).astype(BF16)


def _lru(xc, wa, wx, ba, bx, lam, n_ctx, reverse):
    bsz, s, w = xc.shape
    tm = ROW_TILE
    nchunks = s // tm
    order = _chunk_order(n_ctx // tm, nchunks)
    d = 1 if reverse else 0
    row = lambda b, j: (b, order(d, j), 0)
    ntile = w // V7X_MXU_DIM
    full3 = pl.BlockSpec((ntile, V7X_MXU_DIM, V7X_MXU_DIM), lambda b, j: (0, 0, 0))
    vec = pl.BlockSpec((1, w), lambda b, j: (0, 0))
    return pl.pallas_call(
        functools.partial(_lru_kernel, reverse),
        grid=(bsz, nchunks),
        in_specs=[pl.BlockSpec((1, tm, w), row), full3, full3, vec, vec, vec],
        out_specs=pl.BlockSpec((1, tm, w), row),
        out_shape=jax.ShapeDtypeStruct((bsz, s, w), BF16),
        scratch_shapes=[pltpu.VMEM((tm, w), F32), pltpu.VMEM((tm, w), F32), pltpu.VMEM((1, w), F32)],
        compiler_params=_cparams(("arbitrary", "arbitrary"), 32),
        name="lru_bwd" if reverse else "lru_fwd",
    )(xc, _lru_blockdiag(wa), _lru_blockdiag(wx), ba.reshape(1, w), bx.reshape(1, w), lam.reshape(1, w))


def _route(lt, first, route_ref, cnt_ref):
    ne, tm = lt.shape
    erow = lax.broadcasted_iota(jnp.int32, (ne, tm), 0).astype(F32)
    lg = lt
    tops, hots = [], []
    for _ in range(TOP_K):
        m = jnp.max(lg, axis=0, keepdims=True)
        idx = jnp.min(jnp.where(lg == m, erow, float(ne)), axis=0, keepdims=True)
        hot = erow == idx
        lg = jnp.where(hot, -jnp.inf, lg)
        tops.append((m, idx))
        hots.append(hot)
    es = [jnp.exp(m - tops[0][0]) for m, _ in tops]
    denom = es[0]
    for e in es[1:]:
        denom = denom + e

    if first is not None:
        @pl.when(first)
        def _():
            cnt_ref[...] = jnp.zeros_like(cnt_ref)

    chosen = hots[0]
    for hot in hots[1:]:
        chosen = jnp.logical_or(chosen, hot)
    chosen_f = jnp.where(chosen, 1.0, 0.0)
    row = lax.broadcasted_iota(jnp.int32, (tm, tm), 0)
    col = lax.broadcasted_iota(jnp.int32, (tm, tm), 1)
    before = jnp.where(row < col, 1.0, 0.0).astype(BF16)
    ranks = _dot(chosen_f.astype(BF16), before) + cnt_ref[:, 0:1]
    cnt_ref[...] = cnt_ref[...] + jnp.sum(chosen_f, axis=1, keepdims=True)
    srow = lax.broadcasted_iota(jnp.int32, (route_ref.shape[0], tm), 0)
    out = jnp.zeros((route_ref.shape[0], tm), F32)
    for kk in range(TOP_K):
        rank = jnp.sum(jnp.where(hots[kk], ranks, 0.0), axis=0, keepdims=True)
        out = jnp.where(srow == kk, tops[kk][1], out)
        out = jnp.where(srow == TOP_K + kk, rank, out)
        out = jnp.where(srow == 2 * TOP_K + kk, es[kk] / denom, out)
    route_ref[...] = out


def _tail(sub, y, x, mod, n2_ref, wrt_ref, br_ref, xo_ref, v_ref, route_ref, cnt_ref):
    tm = y.shape[0]
    rows = slice(sub * tm, (sub + 1) * tm)
    xn = x + mod[2:3] * y
    xo_ref[rows, :] = xn
    v = _modnorm(xn, n2_ref[...], mod[4:5], mod[3:4])
    v_ref[rows, :] = v.astype(v_ref.dtype)
    logits_t = _dot_nt(wrt_ref[...], v, precision=HIGHEST) + br_ref[...]
    first = (pl.program_id(0) == 0) if sub == 0 else None
    _route(logits_t, first, route_ref.at[:, rows], cnt_ref)


def _tail_specs(n_rows, tm, d):
    full = lambda shape: pl.BlockSpec(shape, lambda p: (0,) * len(shape))
    step = TAIL_SUB * tm
    in_specs = [full((N_EXPERTS, d)), full((N_EXPERTS, 1))]
    out_specs = [pl.BlockSpec((step, d), lambda p: (p, 0)), pl.BlockSpec((step, d), lambda p: (p, 0)),
                 pl.BlockSpec((ROUTE_ROWS, step), lambda p: (0, p)),
                 full((N_EXPERTS, V7X_LANES))]
    out_shape = [jax.ShapeDtypeStruct((n_rows, d), F32),
                 jax.ShapeDtypeStruct((n_rows, d), BF16),
                 jax.ShapeDtypeStruct((ROUTE_ROWS, n_rows), F32),
                 jax.ShapeDtypeStruct((N_EXPERTS, V7X_LANES), F32)]
    return in_specs, out_specs, out_shape


def _even_out_kernel(hm_ref, hl0_ref, hl1_ref, o_ref, yg_ref, mg_ref, wout_ref, x_ref, *rest):
    mod_refs = rest[:TAIL_SUB]
    n2_ref, wr_ref, br_ref, xo_ref, v_ref, route_ref, cnt_ref = rest[TAIL_SUB:]
    tm = x_ref.shape[0] // TAIL_SUB
    for sub in range(TAIL_SUB):
        rows = slice(sub * tm, (sub + 1) * tm)
        hm = hm_ref[0, rows, :].astype(F32) + hm_ref[1, rows, :].astype(F32)
        parts = []
        for h in range(MLSTM_HEADS):
            sl = slice(h * MLSTM_DH, (h + 1) * MLSTM_DH)
            parts.append(_rms(hm[:, sl], mg_ref[:, sl]))
        hmn = jnp.concatenate(parts, axis=1) * _sigmoid(o_ref[rows, :].astype(F32))
        hl = ((hl0_ref[rows, :].astype(F32) + hl1_ref[rows, :].astype(F32))
              * _gelu_tanh(yg_ref[rows, :].astype(F32)))
        y = (_dot(hmn.astype(BF16), wout_ref[0:MLSTM_W, :])
             + _dot(hl.astype(BF16), wout_ref[MLSTM_W:MLSTM_W + LRU_W, :]))
        _tail(sub, y, x_ref[rows, :], mod_refs[sub][0, 0], n2_ref, wr_ref, br_ref, xo_ref, v_ref, route_ref,
              cnt_ref)


def _even_out(hm, hl0, hl1, o_pre, yg, mnorm_g, w_out, x, mod, n2, w_r, b_r, n_ctx):
    bsz, s, d = x.shape
    tm = ROW_TILE
    nt = s // tm
    nct = n_ctx // tm
    assert (bsz * nt) % TAIL_SUB == 0
    step = TAIL_SUB * tm
    flat = lambda a: a.reshape(bsz * s, a.shape[-1])
    rowp = lambda w: pl.BlockSpec((step, w), lambda p: (p, 0))
    full = lambda shape: pl.BlockSpec(shape, lambda p: (0,) * len(shape))

    def mod_spec(sub):
        def index(p):
            q = p * TAIL_SUB + sub
            return (q // nt, (q % nt >= nct).astype(jnp.int32), 0, 0)
        return pl.BlockSpec((1, 1, 8, d), index)

    tail_in, out_specs, out_shape = _tail_specs(bsz * s, tm, d)
    return pl.pallas_call(
        _even_out_kernel,
        grid=(bsz * nt // TAIL_SUB,),
        in_specs=[pl.BlockSpec((2, step, MLSTM_W), lambda p: (0, p, 0)),
                  rowp(LRU_W), rowp(LRU_W), rowp(MLSTM_W), rowp(LRU_W),
                  full((1, MLSTM_W)), full((MLSTM_W + LRU_W, d)), rowp(d)]
                 + [mod_spec(sub) for sub in range(TAIL_SUB)] + [full((1, d))] + tail_in,
        out_specs=out_specs,
        out_shape=out_shape,
        compiler_params=_cparams(("arbitrary",), 56),
        name="even_out",
    )(hm.reshape(2, bsz * s, MLSTM_W), flat(hl0), flat(hl1), flat(o_pre), flat(yg),
      mnorm_g.reshape(1, MLSTM_W), w_out, flat(x), *([mod] * TAIL_SUB), n2.reshape(1, d),
      w_r.T, b_r.reshape(N_EXPERTS, 1))


def _odd_out_kernel(a_ref, wout_ref, *rest):
    x_refs = rest[:TAIL_SUB]
    mod_refs = rest[TAIL_SUB:2 * TAIL_SUB]
    n2_ref, wr_ref, br_ref, xo_ref, v_ref, route_ref, cnt_ref = rest[2 * TAIL_SUB:]
    tm = a_ref.shape[0] // TAIL_SUB
    for sub in range(TAIL_SUB):
        y = _dot(a_ref[sub * tm:(sub + 1) * tm, :], wout_ref[...])
        _tail(sub, y, x_refs[sub][...], mod_refs[sub][0, 0], n2_ref, wr_ref, br_ref, xo_ref, v_ref, route_ref,
              cnt_ref)


def _odd_out(attn, w_out, x, mod, n2, w_r, b_r, n_ctx):
    bsz, seq, d = attn.shape
    s = x.shape[1]
    tm = ROW_TILE
    ntl = seq // tm
    nta = s // tm
    nct = n_ctx // tm
    assert (bsz * ntl) % TAIL_SUB == 0
    step = TAIL_SUB * tm
    full = lambda shape: pl.BlockSpec(shape, lambda p: (0,) * len(shape))

    def x_spec(sub):
        def index(p):
            q = p * TAIL_SUB + sub
            return ((q // ntl) * nta + nct + q % ntl, 0)
        return pl.BlockSpec((tm, d), index)

    def mod_spec(sub):
        return pl.BlockSpec((1, 1, 8, d), lambda p: ((p * TAIL_SUB + sub) // ntl, 1, 0, 0))

    tail_in, out_specs, out_shape = _tail_specs(bsz * seq, tm, d)
    xf = x.reshape(bsz * s, d)
    return pl.pallas_call(
        _odd_out_kernel,
        grid=(bsz * ntl // TAIL_SUB,),
        in_specs=[pl.BlockSpec((step, d), lambda p: (p, 0)), full((d, d))]
                 + [x_spec(sub) for sub in range(TAIL_SUB)] + [mod_spec(sub) for sub in range(TAIL_SUB)]
                 + [full((1, d))] + tail_in,
        out_specs=out_specs,
        out_shape=out_shape,
        compiler_params=_cparams(("arbitrary",), 48),
        name="odd_out",
    )(attn.reshape(bsz * seq, d), w_out, *([xf] * TAIL_SUB), *([mod] * TAIL_SUB), n2.reshape(1, d),
      w_r.T, b_r.reshape(N_EXPERTS, 1))


def _expert_kernel(te_ref, nu_ref, x_ref, w1_ref, b1_ref, w2_ref, b2_ref, rw_ref, *rest):
    o_ref, w1b, w2b = rest[-3:]
    t = pl.program_id(0)
    tm = x_ref.shape[0]
    used = t < nu_ref[0]
    new_expert = jnp.logical_or(t == 0, te_ref[t] != te_ref[jnp.maximum(t - 1, 0)])

    @pl.when(jnp.logical_and(used, new_expert))
    def _():
        w1b[...] = w1_ref[0, 0].astype(BF16)
        w2b[...] = w2_ref[0, 0].astype(BF16)

    @pl.when(used)
    def _():
        hid = _dot(x_ref[...], w1b[...]) + b1_ref[0, 0]
        gate = jnp.minimum(hid[:, :D_FF], SWIGLU_LIMIT)
        up = jnp.clip(hid[:, D_FF:], -SWIGLU_LIMIT, SWIGLU_LIMIT)
        act = (up + 1.0) * gate * _sigmoid(SWIGLU_ALPHA * gate)
        y = _dot(act.astype(BF16), w2b[...]) + b2_ref[0, 0]
        w_col = jnp.transpose(jnp.broadcast_to(rw_ref[0], (V7X_LANES, tm)))[:, 0:1]
        o_ref[...] = (y * w_col).astype(o_ref.dtype)

    @pl.when(jnp.logical_not(used))
    def _():
        o_ref[...] = jnp.zeros_like(o_ref)


def _experts(xs, row_w, tile_expert, n_used, layer, w1, b1, w2, b2, chunk, ys_buf):
    rows, d = xs.shape
    nt, _, tm = row_w.shape
    nl, ne, _, ff2 = w1.shape
    in_specs = [pl.BlockSpec((tm, d), lambda t, te, nu: (t, 0)),
                pl.BlockSpec((1, 1, d, ff2), lambda t, te, nu: (layer, te[t], 0, 0)),
                pl.BlockSpec((1, 1, 1, ff2), lambda t, te, nu: (layer, te[t], 0, 0)),
                pl.BlockSpec((1, 1, ff2 // 2, d), lambda t, te, nu: (layer, te[t], 0, 0)),
                pl.BlockSpec((1, 1, 1, d), lambda t, te, nu: (layer, te[t], 0, 0)),
                pl.BlockSpec((1, 1, tm), lambda t, te, nu: (t, 0, 0))]
    args = [tile_expert, n_used, xs, w1, b1.reshape(nl, ne, 1, ff2), w2, b2.reshape(nl, ne, 1, d), row_w]
    aliases = {}
    if ys_buf is not None:
        in_specs.append(pl.BlockSpec(memory_space=pl.ANY))
        aliases = {len(args): 0}
        args.append(ys_buf)
    return pl.pallas_call(
        _expert_kernel,
        grid_spec=pltpu.PrefetchScalarGridSpec(
            num_scalar_prefetch=2,
            grid=(nt,),
            in_specs=in_specs,
            out_specs=pl.BlockSpec((tm, d), lambda t, te, nu: (chunk * nt + t, 0)),
            scratch_shapes=[pltpu.VMEM((d, ff2), BF16), pltpu.VMEM((ff2 // 2, d), BF16)],
        ),
        out_shape=jax.ShapeDtypeStruct((MOE_CHUNKS * rows, d), BF16),
        input_output_aliases=aliases,
        compiler_params=_cparams(("arbitrary",), 56),
        name="moe_experts",
    )(*args)


def _moe(v, route, counts, layer, w1, b1, w2, b2):
    t, d = v.shape
    tm = MOE_TILE
    nrows = t * TOP_K
    ntc = -(-(-(-nrows // tm) + N_EXPERTS) // MOE_CHUNKS)
    nt = ntc * MOE_CHUNKS
    idx = route[0:TOP_K].astype(jnp.int32)
    rank = route[TOP_K:2 * TOP_K].astype(jnp.int32)
    weight = route[2 * TOP_K:3 * TOP_K]
    sizes = counts[:, 0].astype(jnp.int32)
    start = jnp.cumsum(sizes) - sizes
    padded = (sizes + tm - 1) // tm * tm
    pad_end = jnp.cumsum(padded)
    pad_start = pad_end - padded
    pair_pos = rank
    for e in range(N_EXPERTS):
        pair_pos = pair_pos + jnp.where(idx == e, pad_start[e], 0)
    bits = max(1, (nrows - 1).bit_length())
    assert N_EXPERTS << bits < 2 ** 31
    pair_id = (jnp.arange(t, dtype=jnp.int32)[None, :] * TOP_K + jnp.arange(TOP_K, dtype=jnp.int32)[:, None])
    order = jnp.sort(((idx << bits) + pair_id).reshape(-1)) & ((1 << bits) - 1)
    tile_row0 = jnp.arange(nt, dtype=jnp.int32) * tm
    tile_expert = jnp.minimum(jnp.sum(pad_end[None, :] <= tile_row0[:, None], axis=1, dtype=jnp.int32),
                              N_EXPERTS - 1)
    n_used = (pad_end[-1] // tm).reshape(1).astype(jnp.int32)
    onehot_te = tile_expert[:, None] == jnp.arange(N_EXPERTS, dtype=jnp.int32)[None, :]
    pick = lambda tbl: jnp.sum(jnp.where(onehot_te, tbl[None, :], 0), axis=1)
    r_in = tile_row0[:, None] + jnp.arange(tm, dtype=jnp.int32)[None, :] - pick(pad_start)[:, None]
    valid = r_in < pick(sizes)[:, None]
    src = jnp.where(valid, pick(start)[:, None] + r_in, 0).reshape(-1)
    pair = order.at[src].get(mode='promise_in_bounds')
    row_token = pair // TOP_K
    flat_w = weight.reshape(-1).at[(pair % TOP_K) * t + row_token].get(mode='promise_in_bounds')
    row_w = jnp.where(valid, flat_w.reshape(nt, tm), 0.0)
    row_token = row_token.reshape(MOE_CHUNKS, ntc * tm)
    row_w = row_w.reshape(MOE_CHUNKS, ntc, 1, tm)
    tile_expert = tile_expert.reshape(MOE_CHUNKS, ntc)
    ys = None
    for c in range(MOE_CHUNKS):
        xs = v.at[row_token[c]].get(mode='promise_in_bounds')
        ys = _experts(xs, row_w[c], tile_expert[c], jnp.clip(n_used - c * ntc, 0, ntc), layer,
                      w1, b1, w2, b2, c, ys)
    return [ys.at[pair_pos[kk]].get(mode='promise_in_bounds') for kk in range(TOP_K)]


def _rope(t, cos, sin, lane_lo):
    swapped = jnp.where(lane_lo, pltpu.roll(t, ATT_DH - ROPE_AXIS_DIM // 2, 1),
                        pltpu.roll(t, ROPE_AXIS_DIM // 2, 1))
    return t * cos + swapped * sin


def _proj_odd_kernel(x_ref, f0, f1, f2, f3, mod0_ref, mod_ref, g_ref, w_ref, qg_ref, kg_ref,
                     cos_ref, sin_ref, h_ref, q_ref, k_ref, v_ref):
    f = (f0[0].astype(F32) + f1[0].astype(F32)) + (f2[0].astype(F32) + f3[0].astype(F32))
    hcur = x_ref[0] + mod0_ref[0, 0][5:6] * f
    h_ref[0] = hcur
    mod = mod_ref[0, 0]
    u = _modnorm(hcur, g_ref[...], mod[1:2], mod[0:1]).astype(BF16)
    z = _dot(u, w_ref[...])
    cos = cos_ref[...]
    sin = sin_ref[...]
    lane = lax.broadcasted_iota(jnp.int32, cos.shape, 1)
    lane_lo = (lane % ROPE_AXIS_DIM) < (ROPE_AXIS_DIM // 2)
    qw = ATT_HEADS * ATT_DH
    kw = ATT_KV_HEADS * ATT_DH
    for hh in range(ATT_HEADS):
        sl = slice(hh * ATT_DH, (hh + 1) * ATT_DH)
        t = _rope(_rms(z[:, sl], qg_ref[...]), cos, sin, lane_lo)
        q_ref[0, :, sl] = (t * (ATT_DH ** -0.5 * LOG2_E)).astype(q_ref.dtype)
    for hh in range(ATT_KV_HEADS):
        sl = slice(hh * ATT_DH, (hh + 1) * ATT_DH)
        t = _rope(_rms(z[:, qw + hh * ATT_DH:qw + (hh + 1) * ATT_DH], kg_ref[...]), cos, sin, lane_lo)
        k_ref[0, :, sl] = t.astype(k_ref.dtype)
    v_ref[0] = z[:, qw + kw:qw + 2 * kw].astype(v_ref.dtype)


def _proj_odd(x, fparts, mod0, mod, g, w, qg, kg, cos_tab, sin_tab, n_ctx):
    bsz, s, d = x.shape
    tm = ROW_TILE
    nt = s // tm
    nct = n_ctx // tm
    n = w.shape[1]
    qw = ATT_HEADS * ATT_DH
    kw = ATT_KV_HEADS * ATT_DH
    row = lambda b, i: (b, i, 0)
    seg = lambda b, i: (b, (i >= nct).astype(jnp.int32), 0, 0)
    full = lambda shape: pl.BlockSpec(shape, lambda b, i: (0,) * len(shape))
    tab = pl.BlockSpec((tm, ATT_DH), lambda b, i: (i, 0))
    return pl.pallas_call(
        _proj_odd_kernel,
        grid=(bsz, nt),
        in_specs=[pl.BlockSpec((1, tm, d), row)] + [pl.BlockSpec((1, tm, d), row)] * TOP_K
                 + [pl.BlockSpec((1, 1, 8, d), seg), pl.BlockSpec((1, 1, 8, d), seg),
                    full((1, d)), full((d, n)), full((1, ATT_DH)), full((1, ATT_DH)), tab, tab],
        out_specs=[pl.BlockSpec((1, tm, d), row), pl.BlockSpec((1, tm, qw), row),
                   pl.BlockSpec((1, tm, kw), row), pl.BlockSpec((1, tm, kw), row)],
        out_shape=[jax.ShapeDtypeStruct((bsz, s, d), F32),
                   jax.ShapeDtypeStruct((bsz, s, qw), BF16),
                   jax.ShapeDtypeStruct((bsz, s, kw), BF16),
                   jax.ShapeDtypeStruct((bsz, s, kw), BF16)],
        compiler_params=_cparams(("arbitrary", "arbitrary"), 48),
        name="proj_odd",
    )(x, *fparts, mod0, mod, g.reshape(1, d), w, qg.reshape(1, ATT_DH), kg.reshape(1, ATT_DH),
      cos_tab, sin_tab)


def _rope_tables(n_ctx, seq):
    rows = seq // GRID_W
    pos_r = jnp.repeat(jnp.arange(rows), GRID_W).astype(F32)
    pos_c = jnp.tile(jnp.arange(GRID_W), rows).astype(F32)
    inv_freq = ROPE_THETA ** (-jnp.arange(0, ROPE_AXIS_DIM, 2, dtype=F32) / ROPE_AXIS_DIM)
    ar = pos_r[:, None] * inv_freq
    ac = pos_c[:, None] * inv_freq
    cos = jnp.concatenate([jnp.cos(ar), jnp.cos(ar), jnp.cos(ac), jnp.cos(ac)], axis=-1)
    sin = jnp.concatenate([-jnp.sin(ar), jnp.sin(ar), -jnp.sin(ac), jnp.sin(ac)], axis=-1)
    cos = jnp.concatenate([jnp.ones((n_ctx, ATT_DH), F32), cos], axis=0)
    sin = jnp.concatenate([jnp.zeros((n_ctx, ATT_DH), F32), sin], axis=0)
    return cos, sin


def _attn_kernel(q_ref, k_ref, v_ref, o_ref):
    k = k_ref[0]
    v = v_ref[0]
    for g in range(ATT_GROUP):
        sl = slice(g * ATT_DH, (g + 1) * ATT_DH)
        s = _dot_nt(q_ref[0, :, sl], k)
        p = jnp.exp2(s - jnp.max(s, axis=1, keepdims=True))
        l = jnp.sum(p, axis=1, keepdims=True)
        o_ref[0, :, sl] = (_dot(p.astype(BF16), v) / l).astype(o_ref.dtype)


def _attention(q, k, v, n_ctx):
    bsz, s, qw = q.shape
    seq = s - n_ctx
    tq = ATT_Q_TILE
    nct = n_ctx // tq
    gw = ATT_GROUP * ATT_DH
    return pl.pallas_call(
        _attn_kernel,
        grid=(bsz, ATT_KV_HEADS, seq // tq),
        in_specs=[pl.BlockSpec((1, tq, gw), lambda b, h, i: (b, i + nct, h)),
                  pl.BlockSpec((1, s, ATT_DH), lambda b, h, i: (b, 0, h)),
                  pl.BlockSpec((1, s, ATT_DH), lambda b, h, i: (b, 0, h))],
        out_specs=pl.BlockSpec((1, tq, gw), lambda b, h, i: (b, i, h)),
        out_shape=jax.ShapeDtypeStruct((bsz, seq, qw), BF16),
        compiler_params=_cparams(("arbitrary", "arbitrary", "arbitrary"), 48),
        name="attention",
    )(q, k, v)


def _final_kernel(x_ref, f0, f1, f2, f3, mod_ref, g_ref, o_ref):
    f = (f0[0].astype(F32) + f1[0].astype(F32)) + (f2[0].astype(F32) + f3[0].astype(F32))
    o_ref[0] = _rms(x_ref[0] + mod_ref[0, 0][5:6] * f, g_ref[...])


def _final(x, fparts, mod, g):
    bsz, seq, d = x.shape
    tm = ROW_TILE
    row = lambda b, i: (b, i, 0)
    return pl.pallas_call(
        _final_kernel,
        grid=(bsz, seq // tm),
        in_specs=[pl.BlockSpec((1, tm, d), row)] * (1 + TOP_K)
                 + [pl.BlockSpec((1, 1, 8, d), lambda b, i: (b, 1, 0, 0)),
                    pl.BlockSpec((1, d), lambda b, i: (0, 0))],
        out_specs=pl.BlockSpec((1, tm, d), row),
        out_shape=jax.ShapeDtypeStruct((bsz, seq, d), F32),
        compiler_params=_cparams(("arbitrary", "arbitrary"), 32),
        name="final_norm",
    )(x, *fparts, mod, g.reshape(1, d))


def _pack_even_w_in(w_in):
    w4 = 4 * MLSTM_W
    ng = 4 * MLSTM_HEADS
    wg = w_in[:, w4:w4 + ng]
    half = ng // 2
    pad = jnp.zeros((w_in.shape[0], GATE_PAD - half), w_in.dtype)
    packed = jnp.concatenate([w_in[:, :w4], w_in[:, w4 + ng:], wg[:, :half], pad, wg[:, half:], pad], axis=1)
    return packed.astype(BF16), wg.T.astype(BF16)


def kernel(x, c, ctx, c_ctx, mod_w, mod_b, norm1_g, norm2_g, final_g, ev_w_in, ev_qk_conv_w, ev_qk_conv_b, ev_gate_b, ev_mnorm_g, ev_lru_conv_w, ev_lru_conv_b, ev_lru_wa, ev_lru_ba, ev_lru_wx, ev_lru_bx, ev_lru_lam, ev_w_out, od_w_in, od_q_norm_g, od_k_norm_g, od_w_out, moe_w_r, moe_b_r, moe_w1, moe_b1, moe_w2, moe_b2):
    bsz, seq, d = x.shape
    n_ctx = ctx.shape[1]
    s = n_ctx + seq
    assert n_ctx % ROW_TILE == 0 and seq % ROW_TILE == 0 and seq % GRID_W == 0
    h = jnp.concatenate([ctx, x], axis=1)

    mod0 = _mod_table(c, c_ctx, mod_w[0], mod_b[0])
    w_packed, wg_t = _pack_even_w_in(ev_w_in[0])
    q, k, v, o_pre, xc, yg, gc, gr = _proj_even(h, mod0, norm1_g[0], w_packed, wg_t, ev_qk_conv_w[0], ev_qk_conv_b[0],
                                                ev_lru_conv_w[0], ev_lru_conv_b[0], n_ctx)
    hm = _mlstm(q, k, v, gc, gr, ev_gate_b[0], n_ctx)
    hl = [_lru(xc, ev_lru_wa[0, dd], ev_lru_wx[0, dd], ev_lru_ba[0, dd], ev_lru_bx[0, dd],
               ev_lru_lam[0, dd], n_ctx, dd == 1) for dd in range(2)]
    x_mid, v0, route0, cnt0 = _even_out(hm, hl[0], hl[1], o_pre, yg, ev_mnorm_g[0], ev_w_out[0].astype(BF16),
                                        h, mod0, norm2_g[0], moe_w_r[0], moe_b_r[0], n_ctx)
    f0 = _moe(v0.reshape(bsz * s, d), route0, cnt0, 0, moe_w1, moe_b1, moe_w2, moe_b2)
    f0 = [p.reshape(bsz, s, d) for p in f0]

    mod1 = _mod_table(c, c_ctx, mod_w[1], mod_b[1])
    cos_tab, sin_tab = _rope_tables(n_ctx, seq)
    h1, q1, k1, v1 = _proj_odd(x_mid.reshape(bsz, s, d), f0, mod0, mod1, norm1_g[1], od_w_in[0].astype(BF16),
                               od_q_norm_g[0], od_k_norm_g[0], cos_tab, sin_tab, n_ctx)
    attn = _attention(q1, k1, v1, n_ctx)
    x2, v2, route2, cnt2 = _odd_out(attn, od_w_out[0].astype(BF16), h1, mod1, norm2_g[1],
                                    moe_w_r[1], moe_b_r[1], n_ctx)
    f1 = _moe(v2.reshape(bsz * seq, d), route2, cnt2, 1, moe_w1, moe_b1, moe_w2, moe_b2)
    f1 = [p.reshape(bsz, seq, d) for p in f1]
    return _final(x2.reshape(bsz, seq, d), f1, mod1, final_g)
```

```python
import functools

import jax
import jax.numpy as jnp
from jax import lax
from jax.experimental import pallas as pl
from jax.experimental.pallas import tpu as pltpu

F32 = jnp.float32
BF16 = jnp.bfloat16
HIGHEST = lax.Precision.HIGHEST

EPS = 1e-6
M_INIT = -1e30
NEG_BIG = -1e30

MLSTM_HEADS = 4
MLSTM_DH = 256
MLSTM_W = MLSTM_HEADS * MLSTM_DH
LRU_W = 1024
LRU_BLOCKS = 16
LRU_BW = LRU_W // LRU_BLOCKS
LRU_C = 8.0
CONV_W = 4
CONV_LEFT = 2
ATT_HEADS = 8
ATT_KV_HEADS = 2
ATT_GROUP = ATT_HEADS // ATT_KV_HEADS
ATT_DH = 128
GRID_W = 64
ROPE_AXIS_DIM = ATT_DH // 2
ROPE_THETA = 10000.0
N_EXPERTS = 32
TOP_K = 4
D_FF = 1024
SWIGLU_ALPHA = 1.702
SWIGLU_LIMIT = 7.0
LOG2_E = 1.4426950408889634

V7X_LANES = 128
V7X_MXU_DIM = 256
V7X_VMEM_BYTES = 64 * 1024 * 1024
MIB = 1024 * 1024

ROW_TILE = 256
HALO = 16
TAIL_SUB = 1
MOE_TILE = 256
MOE_CHUNKS = 8
ROUTE_ROWS = 16
ATT_Q_TILE = 256
GATE_PAD = V7X_LANES


def _cparams(semantics, vmem_mib):
    assert vmem_mib * MIB < V7X_VMEM_BYTES
    return pltpu.CompilerParams(dimension_semantics=semantics, vmem_limit_bytes=vmem_mib * MIB)


def _dot(a, b):
    return jnp.dot(a, b, preferred_element_type=F32)


def _dot_nt(a, b, precision=None):
    return lax.dot_general(a, b, (((1,), (1,)), ((), ())), precision=precision,
                           preferred_element_type=F32)


def _dot_tn(a, b):
    return lax.dot_general(a, b, (((0,), (0,)), ((), ())), preferred_element_type=F32)


def _sigmoid(x):
    return 0.5 * jnp.tanh(0.5 * x) + 0.5


def _log_sigmoid(x):
    return jnp.minimum(x, 0.0) - jnp.log1p(jnp.exp(-jnp.abs(x)))


def _softplus(x):
    return jnp.maximum(x, 0.0) + jnp.log1p(jnp.exp(-jnp.abs(x)))


def _gelu_tanh(x):
    return 0.5 * x * (1.0 + jnp.tanh(0.7978845608028654 * (x + 0.044715 * x * x * x)))


def _rms(x, g):
    return x * lax.rsqrt(jnp.mean(x * x, axis=-1, keepdims=True) + EPS) * g


def _modnorm(x, g, scale, shift):
    return _rms(x, g) * (1.0 + scale) + shift


def _modvec_kernel(c_ref, w_ref, b_ref, o_ref):
    c = c_ref[...]
    s = c * _sigmoid(c)
    o_ref[...] = jnp.dot(s, w_ref[...], precision=HIGHEST, preferred_element_type=F32) + b_ref[...]


def _modvec(cc, w, b):
    rows, d = cc.shape
    n = w.shape[1]
    tn = 1536
    return pl.pallas_call(
        _modvec_kernel,
        grid=(n // tn,),
        in_specs=[pl.BlockSpec((rows, d), lambda j: (0, 0)),
                  pl.BlockSpec((d, tn), lambda j: (0, j)),
                  pl.BlockSpec((1, tn), lambda j: (0, j))],
        out_specs=pl.BlockSpec((rows, tn), lambda j: (0, j)),
        out_shape=jax.ShapeDtypeStruct((rows, n), F32),
        compiler_params=_cparams(("arbitrary",), 32),
        name="modvec",
    )(cc, w, b.reshape(1, n))


def _mod_table(c, c_ctx, mod_w, mod_b):
    bsz, d = c.shape
    rows = ((bsz + 1 + 7) // 8) * 8
    cc = jnp.zeros((rows, d), F32).at[:bsz].set(c).at[bsz].set(c_ctx)
    mod = _modvec(cc, mod_w, mod_b)
    lat = mod[:bsz].reshape(bsz, 6, d)
    ctx = jnp.broadcast_to(mod[bsz].reshape(1, 6, d), (bsz, 6, d))
    tbl = jnp.stack([ctx, lat], axis=1)
    return jnp.pad(tbl, ((0, 0), (0, 0), (0, 2), (0, 0)))


def _proj_even_kernel(nct, x_ref, xp_ref, xn_ref, mod_ref, g_ref, w_ref, wgt_ref,
                      wqk_ref, bqk_ref, wxr_ref, bxr_ref,
                      q_ref, k_ref, v_ref, o_ref, xc_ref, yg_ref, gc_ref, gr_ref):
    i = pl.program_id(1)
    nt = pl.num_programs(1)
    tm = x_ref.shape[1]
    w = MLSTM_W
    first = jnp.logical_or(i == 0, i == nct)
    last = jnp.logical_or(i == nct - 1, i == nt - 1)
    mod = mod_ref[0, 0]
    xe = jnp.concatenate([xp_ref[0], x_ref[0], xn_ref[0]], axis=0)
    ue = _modnorm(xe, g_ref[...], mod[1:2], mod[0:1]).astype(BF16)
    rowi = lax.broadcasted_iota(jnp.int32, (tm + 2 * HALO, 1), 0)
    keep = jnp.where(rowi < HALO, jnp.where(first, 0.0, 1.0),
                     jnp.where(rowi >= HALO + tm, jnp.where(last, 0.0, 1.0), 1.0))

    te = tm + 2 * HALO

    def conv(z, w_ref, b_ref):
        z = z * keep
        acc = b_ref[...]
        for j in range(CONV_W):
            shift = (CONV_LEFT - j) % te
            zj = pltpu.roll(z, shift, 0) if shift else z
            acc = acc + w_ref[j:j + 1, :] * zj[HALO:HALO + tm]
        return acc

    y = conv(_dot(ue, w_ref[:, 0:2 * w]), wqk_ref, bqk_ref)
    y = y * _sigmoid(y)
    q_ref[0] = y[:, :w].astype(q_ref.dtype)
    k_ref[0] = (y[:, w:] * (MLSTM_DH ** -0.5)).astype(k_ref.dtype)
    xc_ref[0] = conv(_dot(ue, w_ref[:, 4 * w:4 * w + LRU_W]), wxr_ref, bxr_ref)
    u = ue[HALO:HALO + tm]
    v_ref[0] = _dot(u, w_ref[:, 2 * w:3 * w]).astype(v_ref.dtype)
    o_ref[0] = _dot(u, w_ref[:, 3 * w:4 * w]).astype(o_ref.dtype)
    yg_ref[0] = _dot(u, w_ref[:, 4 * w + LRU_W:4 * w + 2 * LRU_W]).astype(yg_ref.dtype)
    gc_ref[0] = _dot(u, w_ref[:, 4 * w + 2 * LRU_W:4 * w + 2 * LRU_W + 2 * GATE_PAD])
    gr_ref[0] = _dot_nt(wgt_ref[...], u)


def _proj_even(h, mod, g, w_packed, wg_t, wqk, bqk, wxr, bxr, n_ctx):
    bsz, s, d = h.shape
    tm = ROW_TILE
    nt = s // tm
    nct = n_ctx // tm
    ntot = w_packed.shape[1]
    ng = wg_t.shape[0]
    hb = tm // HALO
    nhb = s // HALO
    cq = 2 * MLSTM_W
    row = lambda b, i: (b, i, 0)
    prev = lambda b, i: (b, jnp.maximum(i * hb - 1, 0), 0)
    nxt = lambda b, i: (b, jnp.minimum((i + 1) * hb, nhb - 1), 0)
    full = lambda shape: pl.BlockSpec(shape, lambda b, i: (0, 0))
    return pl.pallas_call(
        functools.partial(_proj_even_kernel, nct),
        grid=(bsz, nt),
        in_specs=[pl.BlockSpec((1, tm, d), row), pl.BlockSpec((1, HALO, d), prev),
                  pl.BlockSpec((1, HALO, d), nxt),
                  pl.BlockSpec((1, 1, 8, d), lambda b, i: (b, (i >= nct).astype(jnp.int32), 0, 0)),
                  full((1, d)), full((d, ntot)), full((ng, d)),
                  full((CONV_W, cq)), full((1, cq)), full((CONV_W, LRU_W)), full((1, LRU_W))],
        out_specs=[pl.BlockSpec((1, tm, MLSTM_W), row),
                   pl.BlockSpec((1, tm, MLSTM_W), row),
                   pl.BlockSpec((1, tm, MLSTM_W), row),
                   pl.BlockSpec((1, tm, MLSTM_W), row),
                   pl.BlockSpec((1, tm, LRU_W), row),
                   pl.BlockSpec((1, tm, LRU_W), row),
                   pl.BlockSpec((1, tm, 2 * GATE_PAD), row),
                   pl.BlockSpec((1, ng, tm), lambda b, i: (b, 0, i))],
        out_shape=[jax.ShapeDtypeStruct((bsz, s, MLSTM_W), BF16),
                   jax.ShapeDtypeStruct((bsz, s, MLSTM_W), BF16),
                   jax.ShapeDtypeStruct((bsz, s, MLSTM_W), BF16),
                   jax.ShapeDtypeStruct((bsz, s, MLSTM_W), BF16),
                   jax.ShapeDtypeStruct((bsz, s, LRU_W), F32),
                   jax.ShapeDtypeStruct((bsz, s, LRU_W), BF16),
                   jax.ShapeDtypeStruct((bsz, s, 2 * GATE_PAD), F32),
                   jax.ShapeDtypeStruct((bsz, ng, s), F32)],
        compiler_params=_cparams(("arbitrary", "arbitrary"), 56),
        name="proj_even",
    )(h, h, h, mod, g.reshape(1, d), w_packed, wg_t, wqk, bqk.reshape(1, cq), wxr, bxr.reshape(1, LRU_W))


def _mlstm_kernel(q_ref, k_ref, v_ref, gc_ref, gr_ref, bc_ref, br_ref, h_ref, c_scr, n_scr, m_scr):
    d = pl.program_id(1)
    j = pl.program_id(2)
    lc = q_ref.shape[1]
    nh = MLSTM_HEADS
    dh = MLSTM_DH

    @pl.when(j == 0)
    def _():
        c_scr[...] = jnp.zeros_like(c_scr)
        n_scr[...] = jnp.zeros_like(n_scr)
        m_scr[...] = jnp.full(m_scr.shape, M_INIT, F32)

    row = lax.broadcasted_iota(jnp.int32, (lc, lc), 0)
    col = lax.broadcasted_iota(jnp.int32, (lc, lc), 1)
    lo = jnp.where(d == 1, row, col)
    hi = jnp.where(d == 1, col, row)
    tri = lo <= hi
    trif = tri.astype(F32)

    gc = gc_ref[0] + bc_ref[0]
    gr = gr_ref[0, 0] + br_ref[0]
    lfr = _log_sigmoid(gr)
    trib = trif.astype(BF16)
    lf_hi = lfr.astype(BF16)
    lf_r1 = lfr - lf_hi.astype(F32)
    lf_mid = lf_r1.astype(BF16)
    lf_lo = (lf_r1 - lf_mid.astype(F32)).astype(BF16)
    bcum_r = (_dot_nt(lf_hi, trib) + _dot_nt(lf_mid, trib)) + _dot_nt(lf_lo, trib)
    bcum_c = jnp.transpose(jnp.concatenate([bcum_r, jnp.zeros((GATE_PAD - 2 * nh, lc), F32)], axis=0))

    for h in range(nh):
        sl = slice(h * dh, (h + 1) * dh)
        q = q_ref[0, :, sl]
        k = k_ref[0, :, sl]
        v = v_ref[0, :, sl]
        i_col = gc[:, h:h + 1]
        b_col = bcum_c[:, nh + h:nh + h + 1]
        i_row = gr[h:h + 1, :]
        b_row = bcum_r[nh + h:nh + h + 1, :]
        m_prev = m_scr[h][0:1, 0:1]
        c_mat = c_scr[h]
        n_vec = n_scr[h]

        log_intra = jnp.where(tri, b_col - b_row + i_row, NEG_BIG)
        log_inter = b_col + m_prev
        m_t = jnp.maximum(log_inter, jnp.max(log_intra, axis=1, keepdims=True))
        w_inter = jnp.exp(log_inter - m_t)
        scores = _dot_nt(q, k) * jnp.exp(log_intra - m_t)
        num = w_inter * _dot(q, c_mat.astype(BF16)) + _dot(scores.astype(BF16), v)
        den = (w_inter * jnp.sum(q.astype(F32) * n_vec, axis=1, keepdims=True)
               + jnp.sum(scores, axis=1, keepdims=True))
        hh = num / jnp.maximum(jnp.abs(den), jnp.exp(-m_t))
        h_ref[0, 0, :, sl] = hh.astype(h_ref.dtype)

        total_f = jnp.sum(lfr[nh + h:nh + h + 1, :], axis=1, keepdims=True)
        log_w_row = total_f - b_row + i_row
        m_new = jnp.maximum(total_f + m_prev, jnp.max(log_w_row, axis=1, keepdims=True))
        decay = jnp.exp(total_f + m_prev - m_new)
        w_col = jnp.exp(total_f - b_col + i_col - m_new)
        wv = (w_col * v.astype(F32)).astype(BF16)
        c_scr[h] = decay * c_mat + _dot_tn(k, wv)
        n_scr[h] = decay * n_vec + jnp.sum(w_col * k.astype(F32), axis=0, keepdims=True)
        m_scr[h] = jnp.broadcast_to(m_new, m_scr.shape[1:])


def _chunk_order(n_ctx_chunks, nchunks):
    def order(d, j):
        bwd = jnp.where(j < n_ctx_chunks, n_ctx_chunks - 1 - j, nchunks - 1 - (j - n_ctx_chunks))
        return jnp.where(d == 0, j, bwd)
    return order


def _mlstm(q, k, v, gc, gr, gate_b, n_ctx):
    bsz, s, w = q.shape
    lc = ROW_TILE
    nchunks = s // lc
    order = _chunk_order(n_ctx // lc, nchunks)
    nh = MLSTM_HEADS
    gb = gate_b.reshape(2, 2 * nh)
    bc = jnp.pad(gb, ((0, 0), (0, GATE_PAD - 2 * nh))).reshape(2, 1, GATE_PAD)
    br = gb.reshape(2, 2 * nh, 1)
    gr4 = gr.reshape(bsz, 2, 2 * nh, s)
    row = lambda b, d, j: (b, order(d, j), 0)
    return pl.pallas_call(
        _mlstm_kernel,
        grid=(bsz, 2, nchunks),
        in_specs=[pl.BlockSpec((1, lc, w), row), pl.BlockSpec((1, lc, w), row),
                  pl.BlockSpec((1, lc, w), row),
                  pl.BlockSpec((1, lc, GATE_PAD), lambda b, d, j: (b, order(d, j), d)),
                  pl.BlockSpec((1, 1, 2 * nh, lc), lambda b, d, j: (b, d, 0, order(d, j))),
                  pl.BlockSpec((1, 1, GATE_PAD), lambda b, d, j: (d, 0, 0)),
                  pl.BlockSpec((1, 2 * nh, 1), lambda b, d, j: (d, 0, 0))],
        out_specs=pl.BlockSpec((1, 1, lc, w), lambda b, d, j: (d, b, order(d, j), 0)),
        out_shape=jax.ShapeDtypeStruct((2, bsz, s, w), BF16),
        scratch_shapes=[pltpu.VMEM((nh, MLSTM_DH, MLSTM_DH), F32),
                        pltpu.VMEM((nh, 1, MLSTM_DH), F32),
                        pltpu.VMEM((nh, 8, V7X_LANES), F32)],
        compiler_params=_cparams(("arbitrary", "arbitrary", "arbitrary"), 48),
        name="mlstm",
    )(q, k, v, gc, gr4, bc, br)


def _lru_kernel(reverse, x_ref, wa_ref, wx_ref, ba_ref, bx_ref, lam_ref, h_ref, a_scr, b_scr, carry):
    j = pl.program_id(1)
    t_rows = x_ref.shape[1]
    bw = V7X_MXU_DIM

    @pl.when(j == 0)
    def _():
        carry[...] = jnp.zeros_like(carry)

    x = x_ref[0]
    xb = x.astype(BF16)
    sp = _softplus(-lam_ref[...])
    for jj in range(LRU_W // bw):
        sl = slice(jj * bw, (jj + 1) * bw)
        r = _sigmoid(_dot(xb[:, sl], wa_ref[jj]) + ba_ref[:, sl])
        gi = _sigmoid(_dot(xb[:, sl], wx_ref[jj]) + bx_ref[:, sl])
        a = jnp.exp(-LRU_C * r * sp[:, sl])
        a_scr[:, sl] = a
        b_scr[:, sl] = jnp.sqrt(1.0 - a * a) * gi * x[:, sl]

    def body(t, hc):
        tt = t_rows - 1 - t if reverse else t
        hn = a_scr[pl.ds(tt, 1), :] * hc + b_scr[pl.ds(tt, 1), :]
        b_scr[pl.ds(tt, 1), :] = hn
        return hn

    carry[...] = lax.fori_loop(0, t_rows, body, carry[...], unroll=8)
    h_ref[0] = b_scr[...].astype(h_ref.dtype)


def _lru_blockdiag(w):
    per = V7X_MXU_DIM // LRU_BW
    nt = LRU_BLOCKS // per
    w4 = w.reshape(nt, per, LRU_BW, LRU_BW)
    eye = jnp.eye(per, dtype=w.dtype)
    t = jnp.einsum('tpcd,pq->tpcqd', w4, eye)
    return t.reshape(nt, V7X_MXU_DIM, V7X_MXU_DIM).astype(BF16)


def _lru(xc, wa, wx, ba, bx, lam, n_ctx, reverse):
    bsz, s, w = xc.shape
    tm = ROW_TILE
    nchunks = s // tm
    order = _chunk_order(n_ctx // tm, nchunks)
    d = 1 if reverse else 0
    row = lambda b, j: (b, order(d, j), 0)
    ntile = w // V7X_MXU_DIM
    full3 = pl.BlockSpec((ntile, V7X_MXU_DIM, V7X_MXU_DIM), lambda b, j: (0, 0, 0))
    vec = pl.BlockSpec((1, w), lambda b, j: (0, 0))
    return pl.pallas_call(
        functools.partial(_lru_kernel, reverse),
        grid=(bsz, nchunks),
        in_specs=[pl.BlockSpec((1, tm, w), row), full3, full3, vec, vec, vec],
        out_specs=pl.BlockSpec((1, tm, w), row),
        out_shape=jax.ShapeDtypeStruct((bsz, s, w), BF16),
        scratch_shapes=[pltpu.VMEM((tm, w), F32), pltpu.VMEM((tm, w), F32), pltpu.VMEM((1, w), F32)],
        compiler_params=_cparams(("arbitrary", "arbitrary"), 32),
        name="lru_bwd" if reverse else "lru_fwd",
    )(xc, _lru_blockdiag(wa), _lru_blockdiag(wx), ba.reshape(1, w), bx.reshape(1, w), lam.reshape(1, w))


def _route(lt, first, route_ref, cnt_ref):
    ne, tm = lt.shape
    erow = lax.broadcasted_iota(jnp.int32, (ne, tm), 0).astype(F32)
    lg = lt
    tops, hots = [], []
    for _ in range(TOP_K):
        m = jnp.max(lg, axis=0, keepdims=True)
        idx = jnp.min(jnp.where(lg == m, erow, float(ne)), axis=0, keepdims=True)
        hot = erow == idx
        lg = jnp.where(hot, -jnp.inf, lg)
        tops.append((m, idx))
        hots.append(hot)
    es = [jnp.exp(m - tops[0][0]) for m, _ in tops]
    denom = es[0]
    for e in es[1:]:
        denom = denom + e

    if first is not None:
        @pl.when(first)
        def _():
            cnt_ref[...] = jnp.zeros_like(cnt_ref)

    chosen = hots[0]
    for hot in hots[1:]:
        chosen = jnp.logical_or(chosen, hot)
    chosen_f = jnp.where(chosen, 1.0, 0.0)
    row = lax.broadcasted_iota(jnp.int32, (tm, tm), 0)
    col = lax.broadcasted_iota(jnp.int32, (tm, tm), 1)
    before = jnp.where(row < col, 1.0, 0.0).astype(BF16)
    ranks = _dot(chosen_f.astype(BF16), before) + cnt_ref[:, 0:1]
    cnt_ref[...] = cnt_ref[...] + jnp.sum(chosen_f, axis=1, keepdims=True)
    srow = lax.broadcasted_iota(jnp.int32, (route_ref.shape[0], tm), 0)
    out = jnp.zeros((route_ref.shape[0], tm), F32)
    for kk in range(TOP_K):
        rank = jnp.sum(jnp.where(hots[kk], ranks, 0.0), axis=0, keepdims=True)
        out = jnp.where(srow == kk, tops[kk][1], out)
        out = jnp.where(srow == TOP_K + kk, rank, out)
        out = jnp.where(srow == 2 * TOP_K + kk, es[kk] / denom, out)
    route_ref[...] = out


def _tail(sub, y, x, mod, n2_ref, wrt_ref, br_ref, xo_ref, v_ref, route_ref, cnt_ref):
    tm = y.shape[0]
    rows = slice(sub * tm, (sub + 1) * tm)
    xn = x + mod[2:3] * y
    xo_ref[rows, :] = xn
    v = _modnorm(xn, n2_ref[...], mod[4:5], mod[3:4])
    v_hi = v.astype(BF16)
    v_ref[rows, :] = v_hi
    v_lo = (v - v_hi.astype(F32)).astype(BF16)
    wrt = wrt_ref[...]
    w_hi = wrt.astype(BF16)
    w_lo = (wrt - w_hi.astype(F32)).astype(BF16)
    logits_t = (_dot_nt(w_hi, v_hi) + _dot_nt(w_hi, v_lo)) + _dot_nt(w_lo, v_hi) + br_ref[...]
    first = (pl.program_id(0) == 0) if sub == 0 else None
    _route(logits_t, first, route_ref.at[:, rows], cnt_ref)


def _tail_specs(n_rows, tm, d):
    full = lambda shape: pl.BlockSpec(shape, lambda p: (0,) * len(shape))
    step = TAIL_SUB * tm
    in_specs = [full((N_EXPERTS, d)), full((N_EXPERTS, 1))]
    out_specs = [pl.BlockSpec((step, d), lambda p: (p, 0)), pl.BlockSpec((step, d), lambda p: (p, 0)),
                 pl.BlockSpec((ROUTE_ROWS, step), lambda p: (0, p)),
                 full((N_EXPERTS, V7X_LANES))]
    out_shape = [jax.ShapeDtypeStruct((n_rows, d), F32),
                 jax.ShapeDtypeStruct((n_rows, d), BF16),
                 jax.ShapeDtypeStruct((ROUTE_ROWS, n_rows), F32),
                 jax.ShapeDtypeStruct((N_EXPERTS, V7X_LANES), F32)]
    return in_specs, out_specs, out_shape


def _even_out_kernel(hm_ref, hl0_ref, hl1_ref, o_ref, yg_ref, mg_ref, wout_ref, x_ref, *rest):
    mod_refs = rest[:TAIL_SUB]
    n2_ref, wr_ref, br_ref, xo_ref, v_ref, route_ref, cnt_ref = rest[TAIL_SUB:]
    tm = x_ref.shape[0] // TAIL_SUB
    for sub in range(TAIL_SUB):
        rows = slice(sub * tm, (sub + 1) * tm)
        hm = hm_ref[0, rows, :].astype(F32) + hm_ref[1, rows, :].astype(F32)
        parts = []
        for h in range(MLSTM_HEADS):
            sl = slice(h * MLSTM_DH, (h + 1) * MLSTM_DH)
            parts.append(_rms(hm[:, sl], mg_ref[:, sl]))
        hmn = jnp.concatenate(parts, axis=1) * _sigmoid(o_ref[rows, :].astype(F32))
        hl = ((hl0_ref[rows, :].astype(F32) + hl1_ref[rows, :].astype(F32))
              * _gelu_tanh(yg_ref[rows, :].astype(F32)))
        y = (_dot(hmn.astype(BF16), wout_ref[0:MLSTM_W, :])
             + _dot(hl.astype(BF16), wout_ref[MLSTM_W:MLSTM_W + LRU_W, :]))
        _tail(sub, y, x_ref[rows, :], mod_refs[sub][0, 0], n2_ref, wr_ref, br_ref, xo_ref, v_ref, route_ref,
              cnt_ref)


def _even_out(hm, hl0, hl1, o_pre, yg, mnorm_g, w_out, x, mod, n2, w_r, b_r, n_ctx):
    bsz, s, d = x.shape
    tm = ROW_TILE
    nt = s // tm
    nct = n_ctx // tm
    assert (bsz * nt) % TAIL_SUB == 0
    step = TAIL_SUB * tm
    flat = lambda a: a.reshape(bsz * s, a.shape[-1])
    rowp = lambda w: pl.BlockSpec((step, w), lambda p: (p, 0))
    full = lambda shape: pl.BlockSpec(shape, lambda p: (0,) * len(shape))

    def mod_spec(sub):
        def index(p):
            q = p * TAIL_SUB + sub
            return (q // nt, (q % nt >= nct).astype(jnp.int32), 0, 0)
        return pl.BlockSpec((1, 1, 8, d), index)

    tail_in, out_specs, out_shape = _tail_specs(bsz * s, tm, d)
    return pl.pallas_call(
        _even_out_kernel,
        grid=(bsz * nt // TAIL_SUB,),
        in_specs=[pl.BlockSpec((2, step, MLSTM_W), lambda p: (0, p, 0)),
                  rowp(LRU_W), rowp(LRU_W), rowp(MLSTM_W), rowp(LRU_W),
                  full((1, MLSTM_W)), full((MLSTM_W + LRU_W, d)), rowp(d)]
                 + [mod_spec(sub) for sub in range(TAIL_SUB)] + [full((1, d))] + tail_in,
        out_specs=out_specs,
        out_shape=out_shape,
        compiler_params=_cparams(("arbitrary",), 56),
        name="even_out",
    )(hm.reshape(2, bsz * s, MLSTM_W), flat(hl0), flat(hl1), flat(o_pre), flat(yg),
      mnorm_g.reshape(1, MLSTM_W), w_out, flat(x), *([mod] * TAIL_SUB), n2.reshape(1, d),
      w_r.T, b_r.reshape(N_EXPERTS, 1))


def _odd_out_kernel(a_ref, wout_ref, *rest):
    x_refs = rest[:TAIL_SUB]
    mod_refs = rest[TAIL_SUB:2 * TAIL_SUB]
    n2_ref, wr_ref, br_ref, xo_ref, v_ref, route_ref, cnt_ref = rest[2 * TAIL_SUB:]
    tm = a_ref.shape[0] // TAIL_SUB
    for sub in range(TAIL_SUB):
        y = _dot(a_ref[sub * tm:(sub + 1) * tm, :], wout_ref[...])
        _tail(sub, y, x_refs[sub][...], mod_refs[sub][0, 0], n2_ref, wr_ref, br_ref, xo_ref, v_ref, route_ref,
              cnt_ref)


def _odd_out(attn, w_out, x, mod, n2, w_r, b_r, n_ctx):
    bsz, seq, d = attn.shape
    s = x.shape[1]
    tm = ROW_TILE
    ntl = seq // tm
    nta = s // tm
    nct = n_ctx // tm
    assert (bsz * ntl) % TAIL_SUB == 0
    step = TAIL_SUB * tm
    full = lambda shape: pl.BlockSpec(shape, lambda p: (0,) * len(shape))

    def x_spec(sub):
        def index(p):
            q = p * TAIL_SUB + sub
            return ((q // ntl) * nta + nct + q % ntl, 0)
        return pl.BlockSpec((tm, d), index)

    def mod_spec(sub):
        return pl.BlockSpec((1, 1, 8, d), lambda p: ((p * TAIL_SUB + sub) // ntl, 1, 0, 0))

    tail_in, out_specs, out_shape = _tail_specs(bsz * seq, tm, d)
    xf = x.reshape(bsz * s, d)
    return pl.pallas_call(
        _odd_out_kernel,
        grid=(bsz * ntl // TAIL_SUB,),
        in_specs=[pl.BlockSpec((step, d), lambda p: (p, 0)), full((d, d))]
                 + [x_spec(sub) for sub in range(TAIL_SUB)] + [mod_spec(sub) for sub in range(TAIL_SUB)]
                 + [full((1, d))] + tail_in,
        out_specs=out_specs,
        out_shape=out_shape,
        compiler_params=_cparams(("arbitrary",), 48),
        name="odd_out",
    )(attn.reshape(bsz * seq, d), w_out, *([xf] * TAIL_SUB), *([mod] * TAIL_SUB), n2.reshape(1, d),
      w_r.T, b_r.reshape(N_EXPERTS, 1))


def _expert_kernel(te_ref, nu_ref, x_ref, w1_ref, b1_ref, w2_ref, b2_ref, rw_ref, *rest):
    o_ref, w1b, w2b = rest[-3:]
    t = pl.program_id(0)
    tm = x_ref.shape[0]
    used = t < nu_ref[0]
    new_expert = jnp.logical_or(t == 0, te_ref[t] != te_ref[jnp.maximum(t - 1, 0)])

    @pl.when(jnp.logical_and(used, new_expert))
    def _():
        w1b[...] = w1_ref[0, 0].astype(BF16)
        w2b[...] = w2_ref[0, 0].astype(BF16)

    @pl.when(used)
    def _():
        hid = _dot(x_ref[...], w1b[...]) + b1_ref[0, 0]
        gate = jnp.minimum(hid[:, :D_FF], SWIGLU_LIMIT)
        up = jnp.clip(hid[:, D_FF:], -SWIGLU_LIMIT, SWIGLU_LIMIT)
        act = (up + 1.0) * gate * _sigmoid(SWIGLU_ALPHA * gate)
        y = _dot(act.astype(BF16), w2b[...]) + b2_ref[0, 0]
        w_col = jnp.transpose(jnp.broadcast_to(rw_ref[0], (V7X_LANES, tm)))[:, 0:1]
        o_ref[...] = (y * w_col).astype(o_ref.dtype)

    @pl.when(jnp.logical_not(used))
    def _():
        o_ref[...] = jnp.zeros_like(o_ref)


def _experts(xs, row_w, tile_expert, n_used, layer, w1, b1, w2, b2, chunk, ys_buf):
    rows, d = xs.shape
    nt, _, tm = row_w.shape
    nl, ne, _, ff2 = w1.shape
    in_specs = [pl.BlockSpec((tm, d), lambda t, te, nu: (t, 0)),
                pl.BlockSpec((1, 1, d, ff2), lambda t, te, nu: (layer, te[t], 0, 0)),
                pl.BlockSpec((1, 1, 1, ff2), lambda t, te, nu: (layer, te[t], 0, 0)),
                pl.BlockSpec((1, 1, ff2 // 2, d), lambda t, te, nu: (layer, te[t], 0, 0)),
                pl.BlockSpec((1, 1, 1, d), lambda t, te, nu: (layer, te[t], 0, 0)),
                pl.BlockSpec((1, 1, tm), lambda t, te, nu: (t, 0, 0))]
    args = [tile_expert, n_used, xs, w1, b1.reshape(nl, ne, 1, ff2), w2, b2.reshape(nl, ne, 1, d), row_w]
    aliases = {}
    if ys_buf is not None:
        in_specs.append(pl.BlockSpec(memory_space=pl.ANY))
        aliases = {len(args): 0}
        args.append(ys_buf)
    return pl.pallas_call(
        _expert_kernel,
        grid_spec=pltpu.PrefetchScalarGridSpec(
            num_scalar_prefetch=2,
            grid=(nt,),
            in_specs=in_specs,
            out_specs=pl.BlockSpec((tm, d), lambda t, te, nu: (chunk * nt + t, 0)),
            scratch_shapes=[pltpu.VMEM((d, ff2), BF16), pltpu.VMEM((ff2 // 2, d), BF16)],
        ),
        out_shape=jax.ShapeDtypeStruct((MOE_CHUNKS * rows, d), BF16),
        input_output_aliases=aliases,
        compiler_params=_cparams(("arbitrary",), 56),
        name="moe_experts",
    )(*args)


def _moe(v, route, counts, layer, w1, b1, w2, b2):
    t, d = v.shape
    tm = MOE_TILE
    nrows = t * TOP_K
    ntc = -(-(-(-nrows // tm) + N_EXPERTS) // MOE_CHUNKS)
    nt = ntc * MOE_CHUNKS
    idx = route[0:TOP_K].astype(jnp.int32)
    rank = route[TOP_K:2 * TOP_K].astype(jnp.int32)
    weight = route[2 * TOP_K:3 * TOP_K]
    sizes = counts[:, 0].astype(jnp.int32)
    start = jnp.cumsum(sizes) - sizes
    padded = (sizes + tm - 1) // tm * tm
    pad_end = jnp.cumsum(padded)
    pad_start = pad_end - padded
    pair_pos = rank
    for e in range(N_EXPERTS):
        pair_pos = pair_pos + jnp.where(idx == e, pad_start[e], 0)
    bits = max(1, (nrows - 1).bit_length())
    assert N_EXPERTS << bits < 2 ** 31
    pair_id = (jnp.arange(t, dtype=jnp.int32)[None, :] * TOP_K + jnp.arange(TOP_K, dtype=jnp.int32)[:, None])
    order = jnp.sort(((idx << bits) + pair_id).reshape(-1)) & ((1 << bits) - 1)
    tile_row0 = jnp.arange(nt, dtype=jnp.int32) * tm
    tile_expert = jnp.minimum(jnp.sum(pad_end[None, :] <= tile_row0[:, None], axis=1, dtype=jnp.int32),
                              N_EXPERTS - 1)
    n_used = (pad_end[-1] // tm).reshape(1).astype(jnp.int32)
    onehot_te = tile_expert[:, None] == jnp.arange(N_EXPERTS, dtype=jnp.int32)[None, :]
    pick = lambda tbl: jnp.sum(jnp.where(onehot_te, tbl[None, :], 0), axis=1)
    r_in = tile_row0[:, None] + jnp.arange(tm, dtype=jnp.int32)[None, :] - pick(pad_start)[:, None]
    valid = r_in < pick(sizes)[:, None]
    src = jnp.where(valid, pick(start)[:, None] + r_in, 0).reshape(-1)
    pair = order.at[src].get(mode='promise_in_bounds')
    row_token = pair // TOP_K
    flat_w = weight.reshape(-1).at[(pair % TOP_K) * t + row_token].get(mode='promise_in_bounds')
    row_w = jnp.where(valid, flat_w.reshape(nt, tm), 0.0)
    row_token = row_token.reshape(MOE_CHUNKS, ntc * tm)
    row_w = row_w.reshape(MOE_CHUNKS, ntc, 1, tm)
    tile_expert = tile_expert.reshape(MOE_CHUNKS, ntc)
    ys = None
    for c in range(MOE_CHUNKS):
        xs = v.at[row_token[c]].get(mode='promise_in_bounds')
        ys = _experts(xs, row_w[c], tile_expert[c], jnp.clip(n_used - c * ntc, 0, ntc), layer,
                      w1, b1, w2, b2, c, ys)
    return [ys.at[pair_pos[kk]].get(mode='promise_in_bounds') for kk in range(TOP_K)]


def _rope(t, cos, sin, lane_lo):
    swapped = jnp.where(lane_lo, pltpu.roll(t, ATT_DH - ROPE_AXIS_DIM // 2, 1),
                        pltpu.roll(t, ROPE_AXIS_DIM // 2, 1))
    return t * cos + swapped * sin


def _proj_odd_kernel(x_ref, f0, f1, f2, f3, mod0_ref, mod_ref, g_ref, w_ref, qg_ref, kg_ref,
                     cos_ref, sin_ref, h_ref, q_ref, k_ref, v_ref):
    f = (f0[0].astype(F32) + f1[0].astype(F32)) + (f2[0].astype(F32) + f3[0].astype(F32))
    hcur = x_ref[0] + mod0_ref[0, 0][5:6] * f
    h_ref[0] = hcur
    mod = mod_ref[0, 0]
    u = _modnorm(hcur, g_ref[...], mod[1:2], mod[0:1]).astype(BF16)
    z = _dot(u, w_ref[...])
    cos = cos_ref[...]
    sin = sin_ref[...]
    lane = lax.broadcasted_iota(jnp.int32, cos.shape, 1)
    lane_lo = (lane % ROPE_AXIS_DIM) < (ROPE_AXIS_DIM // 2)
    qw = ATT_HEADS * ATT_DH
    kw = ATT_KV_HEADS * ATT_DH
    for hh in range(ATT_HEADS):
        sl = slice(hh * ATT_DH, (hh + 1) * ATT_DH)
        t = _rope(_rms(z[:, sl], qg_ref[...]), cos, sin, lane_lo)
        q_ref[0, :, sl] = (t * (ATT_DH ** -0.5 * LOG2_E)).astype(q_ref.dtype)
    for hh in range(ATT_KV_HEADS):
        sl = slice(hh * ATT_DH, (hh + 1) * ATT_DH)
        t = _rope(_rms(z[:, qw + hh * ATT_DH:qw + (hh + 1) * ATT_DH], kg_ref[...]), cos, sin, lane_lo)
        k_ref[0, :, sl] = t.astype(k_ref.dtype)
    v_ref[0] = z[:, qw + kw:qw + 2 * kw].astype(v_ref.dtype)


def _proj_odd(x, fparts, mod0, mod, g, w, qg, kg, cos_tab, sin_tab, n_ctx):
    bsz, s, d = x.shape
    tm = ROW_TILE
    nt = s // tm
    nct = n_ctx // tm
    n = w.shape[1]
    qw = ATT_HEADS * ATT_DH
    kw = ATT_KV_HEADS * ATT_DH
    row = lambda b, i: (b, i, 0)
    seg = lambda b, i: (b, (i >= nct).astype(jnp.int32), 0, 0)
    full = lambda shape: pl.BlockSpec(shape, lambda b, i: (0,) * len(shape))
    tab = pl.BlockSpec((tm, ATT_DH), lambda b, i: (i, 0))
    return pl.pallas_call(
        _proj_odd_kernel,
        grid=(bsz, nt),
        in_specs=[pl.BlockSpec((1, tm, d), row)] + [pl.BlockSpec((1, tm, d), row)] * TOP_K
                 + [pl.BlockSpec((1, 1, 8, d), seg), pl.BlockSpec((1, 1, 8, d), seg),
                    full((1, d)), full((d, n)), full((1, ATT_DH)), full((1, ATT_DH)), tab, tab],
        out_specs=[pl.BlockSpec((1, tm, d), row), pl.BlockSpec((1, tm, qw), row),
                   pl.BlockSpec((1, tm, kw), row), pl.BlockSpec((1, tm, kw), row)],
        out_shape=[jax.ShapeDtypeStruct((bsz, s, d), F32),
                   jax.ShapeDtypeStruct((bsz, s, qw), BF16),
                   jax.ShapeDtypeStruct((bsz, s, kw), BF16),
                   jax.ShapeDtypeStruct((bsz, s, kw), BF16)],
        compiler_params=_cparams(("arbitrary", "arbitrary"), 48),
        name="proj_odd",
    )(x, *fparts, mod0, mod, g.reshape(1, d), w, qg.reshape(1, ATT_DH), kg.reshape(1, ATT_DH),
      cos_tab, sin_tab)


def _rope_tables(n_ctx, seq):
    rows = seq // GRID_W
    pos_r = jnp.repeat(jnp.arange(rows), GRID_W).astype(F32)
    pos_c = jnp.tile(jnp.arange(GRID_W), rows).astype(F32)
    inv_freq = ROPE_THETA ** (-jnp.arange(0, ROPE_AXIS_DIM, 2, dtype=F32) / ROPE_AXIS_DIM)
    ar = pos_r[:, None] * inv_freq
    ac = pos_c[:, None] * inv_freq
    cos = jnp.concatenate([jnp.cos(ar), jnp.cos(ar), jnp.cos(ac), jnp.cos(ac)], axis=-1)
    sin = jnp.concatenate([-jnp.sin(ar), jnp.sin(ar), -jnp.sin(ac), jnp.sin(ac)], axis=-1)
    cos = jnp.concatenate([jnp.ones((n_ctx, ATT_DH), F32), cos], axis=0)
    sin = jnp.concatenate([jnp.zeros((n_ctx, ATT_DH), F32), sin], axis=0)
    return cos, sin


def _attn_kernel(q_ref, k_ref, v_ref, o_ref):
    k = k_ref[0]
    v = v_ref[0]
    for g in range(ATT_GROUP):
        sl = slice(g * ATT_DH, (g + 1) * ATT_DH)
        s = _dot_nt(q_ref[0, :, sl], k)
        p = jnp.exp2(s - jnp.max(s, axis=1, keepdims=True))
        l = jnp.sum(p, axis=1, keepdims=True)
        o_ref[0, :, sl] = (_dot(p.astype(BF16), v) / l).astype(o_ref.dtype)


def _attention(q, k, v, n_ctx):
    bsz, s, qw = q.shape
    seq = s - n_ctx
    tq = ATT_Q_TILE
    nct = n_ctx // tq
    gw = ATT_GROUP * ATT_DH
    return pl.pallas_call(
        _attn_kernel,
        grid=(bsz, ATT_KV_HEADS, seq // tq),
        in_specs=[pl.BlockSpec((1, tq, gw), lambda b, h, i: (b, i + nct, h)),
                  pl.BlockSpec((1, s, ATT_DH), lambda b, h, i: (b, 0, h)),
                  pl.BlockSpec((1, s, ATT_DH), lambda b, h, i: (b, 0, h))],
        out_specs=pl.BlockSpec((1, tq, gw), lambda b, h, i: (b, i, h)),
        out_shape=jax.ShapeDtypeStruct((bsz, seq, qw), BF16),
        compiler_params=_cparams(("arbitrary", "arbitrary", "arbitrary"), 48),
        name="attention",
    )(q, k, v)


def _final_kernel(x_ref, f0, f1, f2, f3, mod_ref, g_ref, o_ref):
    f = (f0[0].astype(F32) + f1[0].astype(F32)) + (f2[0].astype(F32) + f3[0].astype(F32))
    o_ref[0] = _rms(x_ref[0] + mod_ref[0, 0][5:6] * f, g_ref[...])


def _final(x, fparts, mod, g):
    bsz, seq, d = x.shape
    tm = ROW_TILE
    row = lambda b, i: (b, i, 0)
    return pl.pallas_call(
        _final_kernel,
        grid=(bsz, seq // tm),
        in_specs=[pl.BlockSpec((1, tm, d), row)] * (1 + TOP_K)
                 + [pl.BlockSpec((1, 1, 8, d), lambda b, i: (b, 1, 0, 0)),
                    pl.BlockSpec((1, d), lambda b, i: (0, 0))],
        out_specs=pl.BlockSpec((1, tm, d), row),
        out_shape=jax.ShapeDtypeStruct((bsz, seq, d), F32),
        compiler_params=_cparams(("arbitrary", "arbitrary"), 32),
        name="final_norm",
    )(x, *fparts, mod, g.reshape(1, d))


def _pack_even_w_in(w_in):
    w4 = 4 * MLSTM_W
    ng = 4 * MLSTM_HEADS
    wg = w_in[:, w4:w4 + ng]
    half = ng // 2
    pad = jnp.zeros((w_in.shape[0], GATE_PAD - half), w_in.dtype)
    packed = jnp.concatenate([w_in[:, :w4], w_in[:, w4 + ng:], wg[:, :half], pad, wg[:, half:], pad], axis=1)
    return packed.astype(BF16), wg.T.astype(BF16)


def kernel(x, c, ctx, c_ctx, mod_w, mod_b, norm1_g, norm2_g, final_g, ev_w_in, ev_qk_conv_w, ev_qk_conv_b, ev_gate_b, ev_mnorm_g, ev_lru_conv_w, ev_lru_conv_b, ev_lru_wa, ev_lru_ba, ev_lru_wx, ev_lru_bx, ev_lru_lam, ev_w_out, od_w_in, od_q_norm_g, od_k_norm_g, od_w_out, moe_w_r, moe_b_r, moe_w1, moe_b1, moe_w2, moe_b2):
    bsz, seq, d = x.shape
    n_ctx = ctx.shape[1]
    s = n_ctx + seq
    assert n_ctx % ROW_TILE == 0 and seq % ROW_TILE == 0 and seq % GRID_W == 0
    h = jnp.concatenate([ctx, x], axis=1)

    mod0 = _mod_table(c, c_ctx, mod_w[0], mod_b[0])
    w_packed, wg_t = _pack_even_w_in(ev_w_in[0])
    q, k, v, o_pre, xc, yg, gc, gr = _proj_even(h, mod0, norm1_g[0], w_packed, wg_t, ev_qk_conv_w[0], ev_qk_conv_b[0],
                                                ev_lru_conv_w[0], ev_lru_conv_b[0], n_ctx)
    hm = _mlstm(q, k, v, gc, gr, ev_gate_b[0], n_ctx)
    hl = [_lru(xc, ev_lru_wa[0, dd], ev_lru_wx[0, dd], ev_lru_ba[0, dd], ev_lru_bx[0, dd],
               ev_lru_lam[0, dd], n_ctx, dd == 1) for dd in range(2)]
    x_mid, v0, route0, cnt0 = _even_out(hm, hl[0], hl[1], o_pre, yg, ev_mnorm_g[0], ev_w_out[0].astype(BF16),
                                        h, mod0, norm2_g[0], moe_w_r[0], moe_b_r[0], n_ctx)
    f0 = _moe(v0.reshape(bsz * s, d), route0, cnt0, 0, moe_w1, moe_b1, moe_w2, moe_b2)
    f0 = [p.reshape(bsz, s, d) for p in f0]

    mod1 = _mod_table(c, c_ctx, mod_w[1], mod_b[1])
    cos_tab, sin_tab = _rope_tables(n_ctx, seq)
    h1, q1, k1, v1 = _proj_odd(x_mid.reshape(bsz, s, d), f0, mod0, mod1, norm1_g[1], od_w_in[0].astype(BF16),
                               od_q_norm_g[0], od_k_norm_g[0], cos_tab, sin_tab, n_ctx)
    attn = _attention(q1, k1, v1, n_ctx)
    x2, v2, route2, cnt2 = _odd_out(attn, od_w_out[0].astype(BF16), h1, mod1, norm2_g[1],
                                    moe_w_r[1], moe_b_r[1], n_ctx)
    f1 = _moe(v2.reshape(bsz * seq, d), route2, cnt2, 1, moe_w1, moe_b1, moe_w2, moe_b2)
    f1 = [p.reshape(bsz, seq, d) for p in f1]
    return _final(x2.reshape(bsz, seq, d), f1, mod1, final_g)
```

```python
import functools

import jax
import jax.numpy as jnp
from jax import lax
from jax.experimental import pallas as pl
from jax.experimental.pallas import tpu as pltpu

F32 = jnp.float32
BF16 = jnp.bfloat16
HIGHEST = lax.Precision.HIGHEST

EPS = 1e-6
M_INIT = -1e30
NEG_BIG = -1e30

MLSTM_HEADS = 4
MLSTM_DH = 256
MLSTM_W = MLSTM_HEADS * MLSTM_DH
LRU_W = 1024
LRU_BLOCKS = 16
LRU_BW = LRU_W // LRU_BLOCKS
LRU_C = 8.0
CONV_W = 4
CONV_LEFT = 2
ATT_HEADS = 8
ATT_KV_HEADS = 2
ATT_GROUP = ATT_HEADS // ATT_KV_HEADS
ATT_DH = 128
GRID_W = 64
ROPE_AXIS_DIM = ATT_DH // 2
ROPE_THETA = 10000.0
N_EXPERTS = 32
TOP_K = 4
D_FF = 1024
SWIGLU_ALPHA = 1.702
SWIGLU_LIMIT = 7.0
LOG2_E = 1.4426950408889634

V7X_LANES = 128
V7X_MXU_DIM = 256
V7X_VMEM_BYTES = 64 * 1024 * 1024
MIB = 1024 * 1024

ROW_TILE = 256
HALO = 16
TAIL_SUB = 1
MOE_TILE = 256
MOE_CHUNKS = 8
ROUTE_ROWS = 16
ATT_Q_TILE = 256
GATE_PAD = V7X_LANES


def _cparams(semantics, vmem_mib):
    assert vmem_mib * MIB < V7X_VMEM_BYTES
    return pltpu.CompilerParams(dimension_semantics=semantics, vmem_limit_bytes=vmem_mib * MIB)


def _dot(a, b):
    return jnp.dot(a, b, preferred_element_type=F32)


def _dot_nt(a, b, precision=None):
    return lax.dot_general(a, b, (((1,), (1,)), ((), ())), precision=precision,
                           preferred_element_type=F32)


def _dot_tn(a, b):
    return lax.dot_general(a, b, (((0,), (0,)), ((), ())), preferred_element_type=F32)


def _sigmoid(x):
    return 0.5 * jnp.tanh(0.5 * x) + 0.5


def _log_sigmoid(x):
    return jnp.minimum(x, 0.0) - jnp.log1p(jnp.exp(-jnp.abs(x)))


def _softplus(x):
    return jnp.maximum(x, 0.0) + jnp.log1p(jnp.exp(-jnp.abs(x)))


def _gelu_tanh(x):
    return 0.5 * x * (1.0 + jnp.tanh(0.7978845608028654 * (x + 0.044715 * x * x * x)))


def _rms(x, g):
    return x * lax.rsqrt(jnp.mean(x * x, axis=-1, keepdims=True) + EPS) * g


def _modnorm(x, g, scale, shift):
    return _rms(x, g) * (1.0 + scale) + shift


def _modvec_kernel(c_ref, w_ref, b_ref, o_ref):
    c = c_ref[...]
    s = c * _sigmoid(c)
    o_ref[...] = jnp.dot(s, w_ref[...], precision=HIGHEST, preferred_element_type=F32) + b_ref[...]


def _modvec(cc, w, b):
    rows, d = cc.shape
    n = w.shape[1]
    tn = 1536
    return pl.pallas_call(
        _modvec_kernel,
        grid=(n // tn,),
        in_specs=[pl.BlockSpec((rows, d), lambda j: (0, 0)),
                  pl.BlockSpec((d, tn), lambda j: (0, j)),
                  pl.BlockSpec((1, tn), lambda j: (0, j))],
        out_specs=pl.BlockSpec((rows, tn), lambda j: (0, j)),
        out_shape=jax.ShapeDtypeStruct((rows, n), F32),
        compiler_params=_cparams(("arbitrary",), 32),
        name="modvec",
    )(cc, w, b.reshape(1, n))


def _mod_table(c, c_ctx, mod_w, mod_b):
    bsz, d = c.shape
    rows = ((bsz + 1 + 7) // 8) * 8
    cc = jnp.zeros((rows, d), F32).at[:bsz].set(c).at[bsz].set(c_ctx)
    mod = _modvec(cc, mod_w, mod_b)
    lat = mod[:bsz].reshape(bsz, 6, d)
    ctx = jnp.broadcast_to(mod[bsz].reshape(1, 6, d), (bsz, 6, d))
    tbl = jnp.stack([ctx, lat], axis=1)
    return jnp.pad(tbl, ((0, 0), (0, 0), (0, 2), (0, 0)))


def _proj_even_kernel(nct, x_ref, xp_ref, xn_ref, mod_ref, g_ref, w_ref, wgt_ref,
                      wqk_ref, bqk_ref, wxr_ref, bxr_ref,
                      q_ref, k_ref, v_ref, o_ref, xc_ref, yg_ref, gc_ref, gr_ref):
    i = pl.program_id(1)
    nt = pl.num_programs(1)
    tm = x_ref.shape[1]
    w = MLSTM_W
    first = jnp.logical_or(i == 0, i == nct)
    last = jnp.logical_or(i == nct - 1, i == nt - 1)
    mod = mod_ref[0, 0]
    xe = jnp.concatenate([xp_ref[0], x_ref[0], xn_ref[0]], axis=0)
    ue = _modnorm(xe, g_ref[...], mod[1:2], mod[0:1]).astype(BF16)
    rowi = lax.broadcasted_iota(jnp.int32, (tm + 2 * HALO, 1), 0)
    keep = jnp.where(rowi < HALO, jnp.where(first, 0.0, 1.0),
                     jnp.where(rowi >= HALO + tm, jnp.where(last, 0.0, 1.0), 1.0))

    te = tm + 2 * HALO

    def conv(z, w_ref, b_ref):
        z = z * keep
        acc = b_ref[...]
        for j in range(CONV_W):
            shift = (CONV_LEFT - j) % te
            zj = pltpu.roll(z, shift, 0) if shift else z
            acc = acc + w_ref[j:j + 1, :] * zj[HALO:HALO + tm]
        return acc

    y = conv(_dot(ue, w_ref[:, 0:2 * w]), wqk_ref, bqk_ref)
    y = y * _sigmoid(y)
    q_ref[0] = y[:, :w].astype(q_ref.dtype)
    k_ref[0] = (y[:, w:] * (MLSTM_DH ** -0.5)).astype(k_ref.dtype)
    xc_ref[0] = conv(_dot(ue, w_ref[:, 4 * w:4 * w + LRU_W]), wxr_ref, bxr_ref)
    u = ue[HALO:HALO + tm]
    v_ref[0] = _dot(u, w_ref[:, 2 * w:3 * w]).astype(v_ref.dtype)
    o_ref[0] = _dot(u, w_ref[:, 3 * w:4 * w]).astype(o_ref.dtype)
    yg_ref[0] = _dot(u, w_ref[:, 4 * w + LRU_W:4 * w + 2 * LRU_W]).astype(yg_ref.dtype)
    gc_ref[0] = _dot(u, w_ref[:, 4 * w + 2 * LRU_W:4 * w + 2 * LRU_W + 2 * GATE_PAD])
    gr_ref[0] = _dot_nt(wgt_ref[...], u)


def _proj_even(h, mod, g, w_packed, wg_t, wqk, bqk, wxr, bxr, n_ctx):
    bsz, s, d = h.shape
    tm = ROW_TILE
    nt = s // tm
    nct = n_ctx // tm
    ntot = w_packed.shape[1]
    ng = wg_t.shape[0]
    hb = tm // HALO
    nhb = s // HALO
    cq = 2 * MLSTM_W
    row = lambda b, i: (b, i, 0)
    prev = lambda b, i: (b, jnp.maximum(i * hb - 1, 0), 0)
    nxt = lambda b, i: (b, jnp.minimum((i + 1) * hb, nhb - 1), 0)
    full = lambda shape: pl.BlockSpec(shape, lambda b, i: (0, 0))
    return pl.pallas_call(
        functools.partial(_proj_even_kernel, nct),
        grid=(bsz, nt),
        in_specs=[pl.BlockSpec((1, tm, d), row), pl.BlockSpec((1, HALO, d), prev),
                  pl.BlockSpec((1, HALO, d), nxt),
                  pl.BlockSpec((1, 1, 8, d), lambda b, i: (b, (i >= nct).astype(jnp.int32), 0, 0)),
                  full((1, d)), full((d, ntot)), full((ng, d)),
                  full((CONV_W, cq)), full((1, cq)), full((CONV_W, LRU_W)), full((1, LRU_W))],
        out_specs=[pl.BlockSpec((1, tm, MLSTM_W), row),
                   pl.BlockSpec((1, tm, MLSTM_W), row),
                   pl.BlockSpec((1, tm, MLSTM_W), row),
                   pl.BlockSpec((1, tm, MLSTM_W), row),
                   pl.BlockSpec((1, tm, LRU_W), row),
                   pl.BlockSpec((1, tm, LRU_W), row),
                   pl.BlockSpec((1, tm, 2 * GATE_PAD), row),
                   pl.BlockSpec((1, ng, tm), lambda b, i: (b, 0, i))],
        out_shape=[jax.ShapeDtypeStruct((bsz, s, MLSTM_W), BF16),
                   jax.ShapeDtypeStruct((bsz, s, MLSTM_W), BF16),
                   jax.ShapeDtypeStruct((bsz, s, MLSTM_W), BF16),
                   jax.ShapeDtypeStruct((bsz, s, MLSTM_W), BF16),
                   jax.ShapeDtypeStruct((bsz, s, LRU_W), F32),
                   jax.ShapeDtypeStruct((bsz, s, LRU_W), BF16),
                   jax.ShapeDtypeStruct((bsz, s, 2 * GATE_PAD), F32),
                   jax.ShapeDtypeStruct((bsz, ng, s), F32)],
        compiler_params=_cparams(("arbitrary", "arbitrary"), 56),
        name="proj_even",
    )(h, h, h, mod, g.reshape(1, d), w_packed, wg_t, wqk, bqk.reshape(1, cq), wxr, bxr.reshape(1, LRU_W))


def _mlstm_kernel(q_ref, k_ref, v_ref, gc_ref, gr_ref, bc_ref, br_ref, h_ref, c_scr, n_scr, m_scr):
    d = pl.program_id(1)
    j = pl.program_id(2)
    lc = q_ref.shape[1]
    nh = MLSTM_HEADS
    dh = MLSTM_DH

    @pl.when(j == 0)
    def _():
        c_scr[...] = jnp.zeros_like(c_scr)
        n_scr[...] = jnp.zeros_like(n_scr)
        m_scr[...] = jnp.full(m_scr.shape, M_INIT, F32)

    row = lax.broadcasted_iota(jnp.int32, (lc, lc), 0)
    col = lax.broadcasted_iota(jnp.int32, (lc, lc), 1)
    lo = jnp.where(d == 1, row, col)
    hi = jnp.where(d == 1, col, row)
    tri = lo <= hi
    trif = tri.astype(F32)

    gc = gc_ref[0] + bc_ref[0]
    gr = gr_ref[0, 0] + br_ref[0]
    lfr = _log_sigmoid(gr)
    trib = trif.astype(BF16)
    lf_hi = lfr.astype(BF16)
    lf_r1 = lfr - lf_hi.astype(F32)
    lf_mid = lf_r1.astype(BF16)
    lf_lo = (lf_r1 - lf_mid.astype(F32)).astype(BF16)
    bcum_r = (_dot_nt(lf_hi, trib) + _dot_nt(lf_mid, trib)) + _dot_nt(lf_lo, trib)
    bcum_c = jnp.transpose(jnp.concatenate([bcum_r, jnp.zeros((GATE_PAD - 2 * nh, lc), F32)], axis=0))

    for h in range(nh):
        sl = slice(h * dh, (h + 1) * dh)
        q = q_ref[0, :, sl]
        k = k_ref[0, :, sl]
        v = v_ref[0, :, sl]
        i_col = gc[:, h:h + 1]
        b_col = bcum_c[:, nh + h:nh + h + 1]
        i_row = gr[h:h + 1, :]
        b_row = bcum_r[nh + h:nh + h + 1, :]
        m_prev = m_scr[h][0:1, 0:1]
        c_mat = c_scr[h]
        n_vec = n_scr[h]

        log_intra = jnp.where(tri, b_col - b_row + i_row, NEG_BIG)
        log_inter = b_col + m_prev
        m_t = jnp.maximum(log_inter, jnp.max(log_intra, axis=1, keepdims=True))
        w_inter = jnp.exp(log_inter - m_t)
        scores = _dot_nt(q, k) * jnp.exp(log_intra - m_t)
        num = w_inter * _dot(q, c_mat.astype(BF16)) + _dot(scores.astype(BF16), v)
        den = (w_inter * jnp.sum(q.astype(F32) * n_vec, axis=1, keepdims=True)
               + jnp.sum(scores, axis=1, keepdims=True))
        hh = num / jnp.maximum(jnp.abs(den), jnp.exp(-m_t))
        h_ref[0, 0, :, sl] = hh.astype(h_ref.dtype)

        total_f = jnp.sum(lfr[nh + h:nh + h + 1, :], axis=1, keepdims=True)
        log_w_row = total_f - b_row + i_row
        m_new = jnp.maximum(total_f + m_prev, jnp.max(log_w_row, axis=1, keepdims=True))
        decay = jnp.exp(total_f + m_prev - m_new)
        w_col = jnp.exp(total_f - b_col + i_col - m_new)
        wv = (w_col * v.astype(F32)).astype(BF16)
        c_scr[h] = decay * c_mat + _dot_tn(k, wv)
        n_scr[h] = decay * n_vec + jnp.sum(w_col * k.astype(F32), axis=0, keepdims=True)
        m_scr[h] = jnp.broadcast_to(m_new, m_scr.shape[1:])


def _chunk_order(n_ctx_chunks, nchunks):
    def order(d, j):
        bwd = jnp.where(j < n_ctx_chunks, n_ctx_chunks - 1 - j, nchunks - 1 - (j - n_ctx_chunks))
        return jnp.where(d == 0, j, bwd)
    return order


def _mlstm(q, k, v, gc, gr, gate_b, n_ctx):
    bsz, s, w = q.shape
    lc = ROW_TILE
    nchunks = s // lc
    order = _chunk_order(n_ctx // lc, nchunks)
    nh = MLSTM_HEADS
    gb = gate_b.reshape(2, 2 * nh)
    bc = jnp.pad(gb, ((0, 0), (0, GATE_PAD - 2 * nh))).reshape(2, 1, GATE_PAD)
    br = gb.reshape(2, 2 * nh, 1)
    gr4 = gr.reshape(bsz, 2, 2 * nh, s)
    row = lambda b, d, j: (b, order(d, j), 0)
    return pl.pallas_call(
        _mlstm_kernel,
        grid=(bsz, 2, nchunks),
        in_specs=[pl.BlockSpec((1, lc, w), row), pl.BlockSpec((1, lc, w), row),
                  pl.BlockSpec((1, lc, w), row),
                  pl.BlockSpec((1, lc, GATE_PAD), lambda b, d, j: (b, order(d, j), d)),
                  pl.BlockSpec((1, 1, 2 * nh, lc), lambda b, d, j: (b, d, 0, order(d, j))),
                  pl.BlockSpec((1, 1, GATE_PAD), lambda b, d, j: (d, 0, 0)),
                  pl.BlockSpec((1, 2 * nh, 1), lambda b, d, j: (d, 0, 0))],
        out_specs=pl.BlockSpec((1, 1, lc, w), lambda b, d, j: (d, b, order(d, j), 0)),
        out_shape=jax.ShapeDtypeStruct((2, bsz, s, w), BF16),
        scratch_shapes=[pltpu.VMEM((nh, MLSTM_DH, MLSTM_DH), F32),
                        pltpu.VMEM((nh, 1, MLSTM_DH), F32),
                        pltpu.VMEM((nh, 8, V7X_LANES), F32)],
        compiler_params=_cparams(("arbitrary", "arbitrary", "arbitrary"), 48),
        name="mlstm",
    )(q, k, v, gc, gr4, bc, br)


def _lru_kernel(xf_ref, xb_ref, wa_ref, wx_ref, ba_ref, bx_ref, lam_ref, hf_ref, hb_ref, a_scr, b_scr, carry):
    j = pl.program_id(1)
    t_rows = xf_ref.shape[1]
    bw = V7X_MXU_DIM

    @pl.when(j == 0)
    def _():
        carry[...] = jnp.zeros_like(carry)

    for d, x_ref in enumerate((xf_ref, xb_ref)):
        x = x_ref[0]
        xb = x.astype(BF16)
        sp = _softplus(-lam_ref[d])
        for jj in range(LRU_W // bw):
            sl = slice(jj * bw, (jj + 1) * bw)
            r = _sigmoid(_dot(xb[:, sl], wa_ref[d, jj]) + ba_ref[d, :, sl])
            gi = _sigmoid(_dot(xb[:, sl], wx_ref[d, jj]) + bx_ref[d, :, sl])
            a = jnp.exp(-LRU_C * r * sp[:, sl])
            a_scr[d, :, sl] = a
            b_scr[d, :, sl] = jnp.sqrt(1.0 - a * a) * gi * x[:, sl]

    def body(t, hc):
        hf, hb = hc
        tb = t_rows - 1 - t
        hf = a_scr[0, pl.ds(t, 1), :] * hf + b_scr[0, pl.ds(t, 1), :]
        hb = a_scr[1, pl.ds(tb, 1), :] * hb + b_scr[1, pl.ds(tb, 1), :]
        b_scr[0, pl.ds(t, 1), :] = hf
        b_scr[1, pl.ds(tb, 1), :] = hb
        return hf, hb

    hf, hb = lax.fori_loop(0, t_rows, body, (carry[0], carry[1]), unroll=8)
    carry[0] = hf
    carry[1] = hb
    hf_ref[0] = b_scr[0].astype(hf_ref.dtype)
    hb_ref[0] = b_scr[1].astype(hb_ref.dtype)


def _lru_blockdiag(w):
    per = V7X_MXU_DIM // LRU_BW
    nt = LRU_BLOCKS // per
    w4 = w.reshape(nt, per, LRU_BW, LRU_BW)
    eye = jnp.eye(per, dtype=w.dtype)
    t = jnp.einsum('tpcd,pq->tpcqd', w4, eye)
    return t.reshape(nt, V7X_MXU_DIM, V7X_MXU_DIM).astype(BF16)


def _lru(xc, wa, wx, ba, bx, lam, n_ctx):
    bsz, s, w = xc.shape
    tm = ROW_TILE
    nchunks = s // tm
    order = _chunk_order(n_ctx // tm, nchunks)
    ntile = w // V7X_MXU_DIM
    full4 = pl.BlockSpec((2, ntile, V7X_MXU_DIM, V7X_MXU_DIM), lambda b, j: (0, 0, 0, 0))
    vec = pl.BlockSpec((2, 1, w), lambda b, j: (0, 0, 0))
    fwd = pl.BlockSpec((1, tm, w), lambda b, j: (b, order(0, j), 0))
    bwd = pl.BlockSpec((1, tm, w), lambda b, j: (b, order(1, j), 0))
    blockdiag = lambda wts: jnp.stack([_lru_blockdiag(wts[0]), _lru_blockdiag(wts[1])])
    return pl.pallas_call(
        _lru_kernel,
        grid=(bsz, nchunks),
        in_specs=[fwd, bwd, full4, full4, vec, vec, vec],
        out_specs=[fwd, bwd],
        out_shape=[jax.ShapeDtypeStruct((bsz, s, w), BF16), jax.ShapeDtypeStruct((bsz, s, w), BF16)],
        scratch_shapes=[pltpu.VMEM((2, tm, w), F32), pltpu.VMEM((2, tm, w), F32), pltpu.VMEM((2, 1, w), F32)],
        compiler_params=_cparams(("arbitrary", "arbitrary"), 48),
        name="lru",
    )(xc, xc, blockdiag(wa), blockdiag(wx), ba.reshape(2, 1, w), bx.reshape(2, 1, w), lam.reshape(2, 1, w))


def _route(lt, first, route_ref, cnt_ref):
    ne, tm = lt.shape
    erow = lax.broadcasted_iota(jnp.int32, (ne, tm), 0).astype(F32)
    lg = lt
    tops, hots = [], []
    for _ in range(TOP_K):
        m = jnp.max(lg, axis=0, keepdims=True)
        idx = jnp.min(jnp.where(lg == m, erow, float(ne)), axis=0, keepdims=True)
        hot = erow == idx
        lg = jnp.where(hot, -jnp.inf, lg)
        tops.append((m, idx))
        hots.append(hot)
    es = [jnp.exp(m - tops[0][0]) for m, _ in tops]
    denom = es[0]
    for e in es[1:]:
        denom = denom + e

    if first is not None:
        @pl.when(first)
        def _():
            cnt_ref[...] = jnp.zeros_like(cnt_ref)

    chosen = hots[0]
    for hot in hots[1:]:
        chosen = jnp.logical_or(chosen, hot)
    chosen_f = jnp.where(chosen, 1.0, 0.0)
    row = lax.broadcasted_iota(jnp.int32, (tm, tm), 0)
    col = lax.broadcasted_iota(jnp.int32, (tm, tm), 1)
    before = jnp.where(row < col, 1.0, 0.0).astype(BF16)
    ranks = _dot(chosen_f.astype(BF16), before) + cnt_ref[:, 0:1]
    cnt_ref[...] = cnt_ref[...] + jnp.sum(chosen_f, axis=1, keepdims=True)
    srow = lax.broadcasted_iota(jnp.int32, (route_ref.shape[0], tm), 0)
    out = jnp.zeros((route_ref.shape[0], tm), F32)
    for kk in range(TOP_K):
        rank = jnp.sum(jnp.where(hots[kk], ranks, 0.0), axis=0, keepdims=True)
        out = jnp.where(srow == kk, tops[kk][1], out)
        out = jnp.where(srow == TOP_K + kk, rank, out)
        out = jnp.where(srow == 2 * TOP_K + kk, es[kk] / denom, out)
    route_ref[...] = out


def _tail(sub, y, x, mod, n2_ref, wrt_ref, br_ref, xo_ref, v_ref, route_ref, cnt_ref):
    tm = y.shape[0]
    rows = slice(sub * tm, (sub + 1) * tm)
    xn = x + mod[2:3] * y
    xo_ref[rows, :] = xn
    v = _modnorm(xn, n2_ref[...], mod[4:5], mod[3:4])
    v_hi = v.astype(BF16)
    v_ref[rows, :] = v_hi
    v_lo = (v - v_hi.astype(F32)).astype(BF16)
    wrt = wrt_ref[...]
    w_hi = wrt.astype(BF16)
    w_lo = (wrt - w_hi.astype(F32)).astype(BF16)
    logits_t = (_dot_nt(w_hi, v_hi) + _dot_nt(w_hi, v_lo)) + _dot_nt(w_lo, v_hi) + br_ref[...]
    first = (pl.program_id(0) == 0) if sub == 0 else None
    _route(logits_t, first, route_ref.at[:, rows], cnt_ref)


def _tail_specs(n_rows, tm, d):
    full = lambda shape: pl.BlockSpec(shape, lambda p: (0,) * len(shape))
    step = TAIL_SUB * tm
    in_specs = [full((N_EXPERTS, d)), full((N_EXPERTS, 1))]
    out_specs = [pl.BlockSpec((step, d), lambda p: (p, 0)), pl.BlockSpec((step, d), lambda p: (p, 0)),
                 pl.BlockSpec((ROUTE_ROWS, step), lambda p: (0, p)),
                 full((N_EXPERTS, V7X_LANES))]
    out_shape = [jax.ShapeDtypeStruct((n_rows, d), F32),
                 jax.ShapeDtypeStruct((n_rows, d), BF16),
                 jax.ShapeDtypeStruct((ROUTE_ROWS, n_rows), F32),
                 jax.ShapeDtypeStruct((N_EXPERTS, V7X_LANES), F32)]
    return in_specs, out_specs, out_shape


def _even_out_kernel(hm_ref, hl0_ref, hl1_ref, o_ref, yg_ref, mg_ref, wout_ref, x_ref, *rest):
    mod_refs = rest[:TAIL_SUB]
    n2_ref, wr_ref, br_ref, xo_ref, v_ref, route_ref, cnt_ref = rest[TAIL_SUB:]
    tm = x_ref.shape[0] // TAIL_SUB
    for sub in range(TAIL_SUB):
        rows = slice(sub * tm, (sub + 1) * tm)
        hm = hm_ref[0, rows, :].astype(F32) + hm_ref[1, rows, :].astype(F32)
        parts = []
        for h in range(MLSTM_HEADS):
            sl = slice(h * MLSTM_DH, (h + 1) * MLSTM_DH)
            parts.append(_rms(hm[:, sl], mg_ref[:, sl]))
        hmn = jnp.concatenate(parts, axis=1) * _sigmoid(o_ref[rows, :].astype(F32))
        hl = ((hl0_ref[rows, :].astype(F32) + hl1_ref[rows, :].astype(F32))
              * _gelu_tanh(yg_ref[rows, :].astype(F32)))
        y = (_dot(hmn.astype(BF16), wout_ref[0:MLSTM_W, :])
             + _dot(hl.astype(BF16), wout_ref[MLSTM_W:MLSTM_W + LRU_W, :]))
        _tail(sub, y, x_ref[rows, :], mod_refs[sub][0, 0], n2_ref, wr_ref, br_ref, xo_ref, v_ref, route_ref,
              cnt_ref)


def _even_out(hm, hl0, hl1, o_pre, yg, mnorm_g, w_out, x, mod, n2, w_r, b_r, n_ctx):
    bsz, s, d = x.shape
    tm = ROW_TILE
    nt = s // tm
    nct = n_ctx // tm
    assert (bsz * nt) % TAIL_SUB == 0
    step = TAIL_SUB * tm
    flat = lambda a: a.reshape(bsz * s, a.shape[-1])
    rowp = lambda w: pl.BlockSpec((step, w), lambda p: (p, 0))
    full = lambda shape: pl.BlockSpec(shape, lambda p: (0,) * len(shape))

    def mod_spec(sub):
        def index(p):
            q = p * TAIL_SUB + sub
            return (q // nt, (q % nt >= nct).astype(jnp.int32), 0, 0)
        return pl.BlockSpec((1, 1, 8, d), index)

    tail_in, out_specs, out_shape = _tail_specs(bsz * s, tm, d)
    return pl.pallas_call(
        _even_out_kernel,
        grid=(bsz * nt // TAIL_SUB,),
        in_specs=[pl.BlockSpec((2, step, MLSTM_W), lambda p: (0, p, 0)),
                  rowp(LRU_W), rowp(LRU_W), rowp(MLSTM_W), rowp(LRU_W),
                  full((1, MLSTM_W)), full((MLSTM_W + LRU_W, d)), rowp(d)]
                 + [mod_spec(sub) for sub in range(TAIL_SUB)] + [full((1, d))] + tail_in,
        out_specs=out_specs,
        out_shape=out_shape,
        compiler_params=_cparams(("arbitrary",), 56),
        name="even_out",
    )(hm.reshape(2, bsz * s, MLSTM_W), flat(hl0), flat(hl1), flat(o_pre), flat(yg),
      mnorm_g.reshape(1, MLSTM_W), w_out, flat(x), *([mod] * TAIL_SUB), n2.reshape(1, d),
      w_r.T, b_r.reshape(N_EXPERTS, 1))


def _odd_out_kernel(a_ref, wout_ref, *rest):
    x_refs = rest[:TAIL_SUB]
    mod_refs = rest[TAIL_SUB:2 * TAIL_SUB]
    n2_ref, wr_ref, br_ref, xo_ref, v_ref, route_ref, cnt_ref = rest[2 * TAIL_SUB:]
    tm = a_ref.shape[0] // TAIL_SUB
    for sub in range(TAIL_SUB):
        y = _dot(a_ref[sub * tm:(sub + 1) * tm, :], wout_ref[...])
        _tail(sub, y, x_refs[sub][...], mod_refs[sub][0, 0], n2_ref, wr_ref, br_ref, xo_ref, v_ref, route_ref,
              cnt_ref)


def _odd_out(attn, w_out, x, mod, n2, w_r, b_r, n_ctx):
    bsz, seq, d = attn.shape
    s = x.shape[1]
    tm = ROW_TILE
    ntl = seq // tm
    nta = s // tm
    nct = n_ctx // tm
    assert (bsz * ntl) % TAIL_SUB == 0
    step = TAIL_SUB * tm
    full = lambda shape: pl.BlockSpec(shape, lambda p: (0,) * len(shape))

    def x_spec(sub):
        def index(p):
            q = p * TAIL_SUB + sub
            return ((q // ntl) * nta + nct + q % ntl, 0)
        return pl.BlockSpec((tm, d), index)

    def mod_spec(sub):
        return pl.BlockSpec((1, 1, 8, d), lambda p: ((p * TAIL_SUB + sub) // ntl, 1, 0, 0))

    tail_in, out_specs, out_shape = _tail_specs(bsz * seq, tm, d)
    xf = x.reshape(bsz * s, d)
    return pl.pallas_call(
        _odd_out_kernel,
        grid=(bsz * ntl // TAIL_SUB,),
        in_specs=[pl.BlockSpec((step, d), lambda p: (p, 0)), full((d, d))]
                 + [x_spec(sub) for sub in range(TAIL_SUB)] + [mod_spec(sub) for sub in range(TAIL_SUB)]
                 + [full((1, d))] + tail_in,
        out_specs=out_specs,
        out_shape=out_shape,
        compiler_params=_cparams(("arbitrary",), 48),
        name="odd_out",
    )(attn.reshape(bsz * seq, d), w_out, *([xf] * TAIL_SUB), *([mod] * TAIL_SUB), n2.reshape(1, d),
      w_r.T, b_r.reshape(N_EXPERTS, 1))


def _expert_kernel(te_ref, nu_ref, x_ref, w1_ref, b1_ref, w2_ref, b2_ref, rw_ref, *rest):
    o_ref, w1b, w2b = rest[-3:]
    t = pl.program_id(0)
    tm = x_ref.shape[0]
    used = t < nu_ref[0]
    new_expert = jnp.logical_or(t == 0, te_ref[t] != te_ref[jnp.maximum(t - 1, 0)])

    @pl.when(jnp.logical_and(used, new_expert))
    def _():
        w1b[...] = w1_ref[0, 0].astype(BF16)
        w2b[...] = w2_ref[0, 0].astype(BF16)

    @pl.when(used)
    def _():
        hid = _dot(x_ref[...], w1b[...]) + b1_ref[0, 0]
        gate = jnp.minimum(hid[:, :D_FF], SWIGLU_LIMIT)
        up = jnp.clip(hid[:, D_FF:], -SWIGLU_LIMIT, SWIGLU_LIMIT)
        act = (up + 1.0) * gate * _sigmoid(SWIGLU_ALPHA * gate)
        y = _dot(act.astype(BF16), w2b[...]) + b2_ref[0, 0]
        w_col = jnp.transpose(jnp.broadcast_to(rw_ref[0], (V7X_LANES, tm)))[:, 0:1]
        o_ref[...] = (y * w_col).astype(o_ref.dtype)

    @pl.when(jnp.logical_not(used))
    def _():
        o_ref[...] = jnp.zeros_like(o_ref)


def _experts(xs, row_w, tile_expert, n_used, layer, w1, b1, w2, b2, chunk, ys_buf):
    rows, d = xs.shape
    nt, _, tm = row_w.shape
    nl, ne, _, ff2 = w1.shape
    in_specs = [pl.BlockSpec((tm, d), lambda t, te, nu: (t, 0)),
                pl.BlockSpec((1, 1, d, ff2), lambda t, te, nu: (layer, te[t], 0, 0)),
                pl.BlockSpec((1, 1, 1, ff2), lambda t, te, nu: (layer, te[t], 0, 0)),
                pl.BlockSpec((1, 1, ff2 // 2, d), lambda t, te, nu: (layer, te[t], 0, 0)),
                pl.BlockSpec((1, 1, 1, d), lambda t, te, nu: (layer, te[t], 0, 0)),
                pl.BlockSpec((1, 1, tm), lambda t, te, nu: (t, 0, 0))]
    args = [tile_expert, n_used, xs, w1, b1.reshape(nl, ne, 1, ff2), w2, b2.reshape(nl, ne, 1, d), row_w]
    aliases = {}
    if ys_buf is not None:
        in_specs.append(pl.BlockSpec(memory_space=pl.ANY))
        aliases = {len(args): 0}
        args.append(ys_buf)
    return pl.pallas_call(
        _expert_kernel,
        grid_spec=pltpu.PrefetchScalarGridSpec(
            num_scalar_prefetch=2,
            grid=(nt,),
            in_specs=in_specs,
            out_specs=pl.BlockSpec((tm, d), lambda t, te, nu: (chunk * nt + t, 0)),
            scratch_shapes=[pltpu.VMEM((d, ff2), BF16), pltpu.VMEM((ff2 // 2, d), BF16)],
        ),
        out_shape=jax.ShapeDtypeStruct((MOE_CHUNKS * rows, d), BF16),
        input_output_aliases=aliases,
        compiler_params=_cparams(("arbitrary",), 56),
        name="moe_experts",
    )(*args)


def _moe(v, route, counts, layer, w1, b1, w2, b2):
    t, d = v.shape
    tm = MOE_TILE
    nrows = t * TOP_K
    ntc = -(-(-(-nrows // tm) + N_EXPERTS) // MOE_CHUNKS)
    nt = ntc * MOE_CHUNKS
    idx = route[0:TOP_K].astype(jnp.int32)
    rank = route[TOP_K:2 * TOP_K].astype(jnp.int32)
    weight = route[2 * TOP_K:3 * TOP_K]
    sizes = counts[:, 0].astype(jnp.int32)
    start = jnp.cumsum(sizes) - sizes
    padded = (sizes + tm - 1) // tm * tm
    pad_end = jnp.cumsum(padded)
    pad_start = pad_end - padded
    pair_pos = rank
    for e in range(N_EXPERTS):
        pair_pos = pair_pos + jnp.where(idx == e, pad_start[e], 0)
    bits = max(1, (nrows - 1).bit_length())
    assert N_EXPERTS << bits < 2 ** 31
    pair_id = (jnp.arange(t, dtype=jnp.int32)[None, :] * TOP_K + jnp.arange(TOP_K, dtype=jnp.int32)[:, None])
    order = jnp.sort(((idx << bits) + pair_id).reshape(-1)) & ((1 << bits) - 1)
    tile_row0 = jnp.arange(nt, dtype=jnp.int32) * tm
    tile_expert = jnp.minimum(jnp.sum(pad_end[None, :] <= tile_row0[:, None], axis=1, dtype=jnp.int32),
                              N_EXPERTS - 1)
    n_used = (pad_end[-1] // tm).reshape(1).astype(jnp.int32)
    onehot_te = tile_expert[:, None] == jnp.arange(N_EXPERTS, dtype=jnp.int32)[None, :]
    pick = lambda tbl: jnp.sum(jnp.where(onehot_te, tbl[None, :], 0), axis=1)
    r_in = tile_row0[:, None] + jnp.arange(tm, dtype=jnp.int32)[None, :] - pick(pad_start)[:, None]
    valid = r_in < pick(sizes)[:, None]
    src = jnp.where(valid, pick(start)[:, None] + r_in, 0).reshape(-1)
    pair = order.at[src].get(mode='promise_in_bounds')
    row_token = pair // TOP_K
    flat_w = weight.reshape(-1).at[(pair % TOP_K) * t + row_token].get(mode='promise_in_bounds')
    row_w = jnp.where(valid, flat_w.reshape(nt, tm), 0.0)
    row_token = row_token.reshape(MOE_CHUNKS, ntc * tm)
    row_w = row_w.reshape(MOE_CHUNKS, ntc, 1, tm)
    tile_expert = tile_expert.reshape(MOE_CHUNKS, ntc)
    ys = None
    for c in range(MOE_CHUNKS):
        xs = v.at[row_token[c]].get(mode='promise_in_bounds')
        ys = _experts(xs, row_w[c], tile_expert[c], jnp.clip(n_used - c * ntc, 0, ntc), layer,
                      w1, b1, w2, b2, c, ys)
    return [ys.at[pair_pos[kk]].get(mode='promise_in_bounds') for kk in range(TOP_K)]


def _rope(t, cos, sin, lane_lo):
    swapped = jnp.where(lane_lo, pltpu.roll(t, ATT_DH - ROPE_AXIS_DIM // 2, 1),
                        pltpu.roll(t, ROPE_AXIS_DIM // 2, 1))
    return t * cos + swapped * sin


def _proj_odd_kernel(x_ref, f0, f1, f2, f3, mod0_ref, mod_ref, g_ref, w_ref, qg_ref, kg_ref,
                     cos_ref, sin_ref, h_ref, q_ref, k_ref, v_ref):
    f = (f0[0].astype(F32) + f1[0].astype(F32)) + (f2[0].astype(F32) + f3[0].astype(F32))
    hcur = x_ref[0] + mod0_ref[0, 0][5:6] * f
    h_ref[0] = hcur
    mod = mod_ref[0, 0]
    u = _modnorm(hcur, g_ref[...], mod[1:2], mod[0:1]).astype(BF16)
    z = _dot(u, w_ref[...])
    cos = cos_ref[...]
    sin = sin_ref[...]
    lane = lax.broadcasted_iota(jnp.int32, cos.shape, 1)
    lane_lo = (lane % ROPE_AXIS_DIM) < (ROPE_AXIS_DIM // 2)
    qw = ATT_HEADS * ATT_DH
    kw = ATT_KV_HEADS * ATT_DH
    for hh in range(ATT_HEADS):
        sl = slice(hh * ATT_DH, (hh + 1) * ATT_DH)
        t = _rope(_rms(z[:, sl], qg_ref[...]), cos, sin, lane_lo)
        q_ref[0, :, sl] = (t * (ATT_DH ** -0.5 * LOG2_E)).astype(q_ref.dtype)
    for hh in range(ATT_KV_HEADS):
        sl = slice(hh * ATT_DH, (hh + 1) * ATT_DH)
        t = _rope(_rms(z[:, qw + hh * ATT_DH:qw + (hh + 1) * ATT_DH], kg_ref[...]), cos, sin, lane_lo)
        k_ref[0, :, sl] = t.astype(k_ref.dtype)
    v_ref[0] = z[:, qw + kw:qw + 2 * kw].astype(v_ref.dtype)


def _proj_odd(x, fparts, mod0, mod, g, w, qg, kg, cos_tab, sin_tab, n_ctx):
    bsz, s, d = x.shape
    tm = ROW_TILE
    nt = s // tm
    nct = n_ctx // tm
    n = w.shape[1]
    qw = ATT_HEADS * ATT_DH
    kw = ATT_KV_HEADS * ATT_DH
    row = lambda b, i: (b, i, 0)
    seg = lambda b, i: (b, (i >= nct).astype(jnp.int32), 0, 0)
    full = lambda shape: pl.BlockSpec(shape, lambda b, i: (0,) * len(shape))
    tab = pl.BlockSpec((tm, ATT_DH), lambda b, i: (i, 0))
    return pl.pallas_call(
        _proj_odd_kernel,
        grid=(bsz, nt),
        in_specs=[pl.BlockSpec((1, tm, d), row)] + [pl.BlockSpec((1, tm, d), row)] * TOP_K
                 + [pl.BlockSpec((1, 1, 8, d), seg), pl.BlockSpec((1, 1, 8, d), seg),
                    full((1, d)), full((d, n)), full((1, ATT_DH)), full((1, ATT_DH)), tab, tab],
        out_specs=[pl.BlockSpec((1, tm, d), row), pl.BlockSpec((1, tm, qw), row),
                   pl.BlockSpec((1, tm, kw), row), pl.BlockSpec((1, tm, kw), row)],
        out_shape=[jax.ShapeDtypeStruct((bsz, s, d), F32),
                   jax.ShapeDtypeStruct((bsz, s, qw), BF16),
                   jax.ShapeDtypeStruct((bsz, s, kw), BF16),
                   jax.ShapeDtypeStruct((bsz, s, kw), BF16)],
        compiler_params=_cparams(("arbitrary", "arbitrary"), 48),
        name="proj_odd",
    )(x, *fparts, mod0, mod, g.reshape(1, d), w, qg.reshape(1, ATT_DH), kg.reshape(1, ATT_DH),
      cos_tab, sin_tab)


def _rope_tables(n_ctx, seq):
    rows = seq // GRID_W
    pos_r = jnp.repeat(jnp.arange(rows), GRID_W).astype(F32)
    pos_c = jnp.tile(jnp.arange(GRID_W), rows).astype(F32)
    inv_freq = ROPE_THETA ** (-jnp.arange(0, ROPE_AXIS_DIM, 2, dtype=F32) / ROPE_AXIS_DIM)
    ar = pos_r[:, None] * inv_freq
    ac = pos_c[:, None] * inv_freq
    cos = jnp.concatenate([jnp.cos(ar), jnp.cos(ar), jnp.cos(ac), jnp.cos(ac)], axis=-1)
    sin = jnp.concatenate([-jnp.sin(ar), jnp.sin(ar), -jnp.sin(ac), jnp.sin(ac)], axis=-1)
    cos = jnp.concatenate([jnp.ones((n_ctx, ATT_DH), F32), cos], axis=0)
    sin = jnp.concatenate([jnp.zeros((n_ctx, ATT_DH), F32), sin], axis=0)
    return cos, sin


def _attn_kernel(q_ref, k_ref, v_ref, o_ref):
    k = k_ref[0]
    v = v_ref[0]
    for g in range(ATT_GROUP):
        sl = slice(g * ATT_DH, (g + 1) * ATT_DH)
        s = _dot_nt(q_ref[0, :, sl], k)
        p = jnp.exp2(s - jnp.max(s, axis=1, keepdims=True))
        l = jnp.sum(p, axis=1, keepdims=True)
        o_ref[0, :, sl] = (_dot(p.astype(BF16), v) / l).astype(o_ref.dtype)


def _attention(q, k, v, n_ctx):
    bsz, s, qw = q.shape
    seq = s - n_ctx
    tq = ATT_Q_TILE
    nct = n_ctx // tq
    gw = ATT_GROUP * ATT_DH
    return pl.pallas_call(
        _attn_kernel,
        grid=(bsz, ATT_KV_HEADS, seq // tq),
        in_specs=[pl.BlockSpec((1, tq, gw), lambda b, h, i: (b, i + nct, h)),
                  pl.BlockSpec((1, s, ATT_DH), lambda b, h, i: (b, 0, h)),
                  pl.BlockSpec((1, s, ATT_DH), lambda b, h, i: (b, 0, h))],
        out_specs=pl.BlockSpec((1, tq, gw), lambda b, h, i: (b, i, h)),
        out_shape=jax.ShapeDtypeStruct((bsz, seq, qw), BF16),
        compiler_params=_cparams(("arbitrary", "arbitrary", "arbitrary"), 48),
        name="attention",
    )(q, k, v)


def _final_kernel(x_ref, f0, f1, f2, f3, mod_ref, g_ref, o_ref):
    f = (f0[0].astype(F32) + f1[0].astype(F32)) + (f2[0].astype(F32) + f3[0].astype(F32))
    o_ref[0] = _rms(x_ref[0] + mod_ref[0, 0][5:6] * f, g_ref[...])


def _final(x, fparts, mod, g):
    bsz, seq, d = x.shape
    tm = ROW_TILE
    row = lambda b, i: (b, i, 0)
    return pl.pallas_call(
        _final_kernel,
        grid=(bsz, seq // tm),
        in_specs=[pl.BlockSpec((1, tm, d), row)] * (1 + TOP_K)
                 + [pl.BlockSpec((1, 1, 8, d), lambda b, i: (b, 1, 0, 0)),
                    pl.BlockSpec((1, d), lambda b, i: (0, 0))],
        out_specs=pl.BlockSpec((1, tm, d), row),
        out_shape=jax.ShapeDtypeStruct((bsz, seq, d), F32),
        compiler_params=_cparams(("arbitrary", "arbitrary"), 32),
        name="final_norm",
    )(x, *fparts, mod, g.reshape(1, d))


def _pack_even_w_in(w_in):
    w4 = 4 * MLSTM_W
    ng = 4 * MLSTM_HEADS
    wg = w_in[:, w4:w4 + ng]
    half = ng // 2
    pad = jnp.zeros((w_in.shape[0], GATE_PAD - half), w_in.dtype)
    packed = jnp.concatenate([w_in[:, :w4], w_in[:, w4 + ng:], wg[:, :half], pad, wg[:, half:], pad], axis=1)
    return packed.astype(BF16), wg.T.astype(BF16)


def kernel(x, c, ctx, c_ctx, mod_w, mod_b, norm1_g, norm2_g, final_g, ev_w_in, ev_qk_conv_w, ev_qk_conv_b, ev_gate_b, ev_mnorm_g, ev_lru_conv_w, ev_lru_conv_b, ev_lru_wa, ev_lru_ba, ev_lru_wx, ev_lru_bx, ev_lru_lam, ev_w_out, od_w_in, od_q_norm_g, od_k_norm_g, od_w_out, moe_w_r, moe_b_r, moe_w1, moe_b1, moe_w2, moe_b2):
    bsz, seq, d = x.shape
    n_ctx = ctx.shape[1]
    s = n_ctx + seq
    assert n_ctx % ROW_TILE == 0 and seq % ROW_TILE == 0 and seq % GRID_W == 0
    h = jnp.concatenate([ctx, x], axis=1)

    mod0 = _mod_table(c, c_ctx, mod_w[0], mod_b[0])
    w_packed, wg_t = _pack_even_w_in(ev_w_in[0])
    q, k, v, o_pre, xc, yg, gc, gr = _proj_even(h, mod0, norm1_g[0], w_packed, wg_t, ev_qk_conv_w[0], ev_qk_conv_b[0],
                                                ev_lru_conv_w[0], ev_lru_conv_b[0], n_ctx)
    hm = _mlstm(q, k, v, gc, gr, ev_gate_b[0], n_ctx)
    hl = _lru(xc, ev_lru_wa[0], ev_lru_wx[0], ev_lru_ba[0].reshape(2, LRU_W), ev_lru_bx[0].reshape(2, LRU_W),
              ev_lru_lam[0], n_ctx)
    x_mid, v0, route0, cnt0 = _even_out(hm, hl[0], hl[1], o_pre, yg, ev_mnorm_g[0], ev_w_out[0].astype(BF16),
                                        h, mod0, norm2_g[0], moe_w_r[0], moe_b_r[0], n_ctx)
    f0 = _moe(v0.reshape(bsz * s, d), route0, cnt0, 0, moe_w1, moe_b1, moe_w2, moe_b2)
    f0 = [p.reshape(bsz, s, d) for p in f0]

    mod1 = _mod_table(c, c_ctx, mod_w[1], mod_b[1])
    cos_tab, sin_tab = _rope_tables(n_ctx, seq)
    h1, q1, k1, v1 = _proj_odd(x_mid.reshape(bsz, s, d), f0, mod0, mod1, norm1_g[1], od_w_in[0].astype(BF16),
                               od_q_norm_g[0], od_k_norm_g[0], cos_tab, sin_tab, n_ctx)
    attn = _attention(q1, k1, v1, n_ctx)
    x2, v2, route2, cnt2 = _odd_out(attn, od_w_out[0].astype(BF16), h1, mod1, norm2_g[1],
                                    moe_w_r[1], moe_b_r[1], n_ctx)
    f1 = _moe(v2.reshape(bsz * seq, d), route2, cnt2, 1, moe_w1, moe_b1, moe_w2, moe_b2)
    f1 = [p.reshape(bsz, seq, d) for p in f1]
    return _final(x2.reshape(bsz, seq, d), f1, mod1, final_g)
```
